```python
import math
import jax, jax.numpy as jnp
from jax import lax
import numpy as np

D_MODEL = 1024
BATCH = 32
SEQ = 2048
DEPTH = 2
DEC_BATCH = 8
DEC_SEQ = 64
PAST_LEN = 4096

CHUNK = 64
N_META = 16
Q_BLOCK = 128
PAD = Q_BLOCK - N_META
N_EVEN = (DEPTH + 1) // 2
N_ODD = DEPTH // 2
W_MIX = D_MODEL
W_HALF = W_MIX // 2
S5_GROUP_CH = 16
S5_GROUPS = W_HALF // S5_GROUP_CH
S5_STATE = 64
RET_DK = 64
RET_HEADS = W_HALF // RET_DK
SB_DH = 64
SB_HEADS = W_HALF // SB_DH
DIFF_DH = 64
DIFF_HEADS = W_HALF // (2 * DIFF_DH)
W_IN_EVEN = 5 * W_HALF
W_IN_ODD = 6 * W_HALF
N_EXPERTS = 64
N_GROUPS = 8
TOPK_GROUPS = 4
TOP_K = 6
D_EXPERT = D_MODEL // 4
D_SHARED = D_MODEL // 4
ROUTED_SCALE = 2.5
EXPERT_BLOCK = 256
ALPHA = (2 * DEPTH) ** 0.25
BETA = (8 * DEPTH) ** -0.25
EPS = 1e-5
NEG_BIG = -1e30

kernel_name = 'hybrid_stream_encoder_step'


def layer_norm(x, g, b):
    xf = x.astype(jnp.float32)
    mu = jnp.mean(xf, -1, keepdims=True)
    var = jnp.mean(jnp.square(xf - mu), -1, keepdims=True)
    y = (xf - mu) * lax.rsqrt(var + EPS) * g.astype(jnp.float32) + b.astype(jnp.float32)
    return y.astype(x.dtype)


def head_norm(x):
    mu = jnp.mean(x, -1, keepdims=True)
    var = jnp.mean(jnp.square(x - mu), -1, keepdims=True)
    return (x - mu) * lax.rsqrt(var + EPS)


def rms_norm(x, g):
    return x * lax.rsqrt(jnp.mean(jnp.square(x), -1, keepdims=True) + EPS) * g.astype(jnp.float32)


def to_chunks(x, pad, clen):
    x = jnp.pad(x, [(0, 0), (pad, 0)] + [(0, 0)] * (x.ndim - 2))
    b, lp = x.shape[:2]
    x = x.reshape((b, lp // clen, clen) + x.shape[2:])
    return jnp.moveaxis(x, 1, 0)


def from_chunks(y, pad):
    y = jnp.moveaxis(y, 0, 1)
    y = y.reshape((y.shape[0], -1) + y.shape[3:])
    return y[:, pad:]


def s5_discretise(lam_re, lam_im, log_dt, b_re, b_im):
    dt = jnp.exp(log_dt)[:, None]
    mag = jnp.exp(lam_re * dt)
    lb_re = mag * jnp.cos(lam_im * dt)
    lb_im = mag * jnp.sin(lam_im * dt)
    nr, ni = lb_re - 1.0, lb_im
    den = jnp.square(lam_re) + jnp.square(lam_im)
    cr = (nr * lam_re + ni * lam_im) / den
    ci = (ni * lam_re - nr * lam_im) / den
    bb_re = cr[..., None] * b_re - ci[..., None] * b_im
    bb_im = cr[..., None] * b_im + ci[..., None] * b_re
    return lb_re, lb_im, bb_re, bb_im


def complex_affine_combine(e1, e2):
    a1r, a1i, b1r, b1i = e1
    a2r, a2i, b2r, b2i = e2
    return (a2r * a1r - a2i * a1i,
            a2r * a1i + a2i * a1r,
            a2r * b1r - a2i * b1i + b2r,
            a2r * b1i + a2i * b1r + b2i)


def s5_scan(u, carry_re, carry_im, lb_re, lb_im, bb_re, bb_im, c_re, c_im, d, pad, clen):
    def step(carry, u_c):
        cre, cim = carry
        bu_re = jnp.einsum('btgi,gpi->btgp', u_c, bb_re)
        bu_im = jnp.einsum('btgi,gpi->btgp', u_c, bb_im)
        a_re = jnp.broadcast_to(lb_re, bu_re.shape)
        a_im = jnp.broadcast_to(lb_im, bu_im.shape)
        acc_re, acc_im, loc_re, loc_im = lax.associative_scan(
            complex_affine_combine, (a_re, a_im, bu_re, bu_im), axis=1)
        x_re = loc_re + acc_re * cre[:, None] - acc_im * cim[:, None]
        x_im = loc_im + acc_re * cim[:, None] + acc_im * cre[:, None]
        y = (jnp.einsum('btgp,gop->btgo', x_re, c_re)
             - jnp.einsum('btgp,gop->btgo', x_im, c_im) + d * u_c)
        return (x_re[:, -1], x_im[:, -1]), y
    (fin_re, fin_im), y = lax.scan(step, (carry_re, carry_im), to_chunks(u, pad, clen))
    return from_chunks(y, pad), fin_re, fin_im


def rotary(x, pos):
    d = x.shape[-1]
    inv = 1.0 / (10000.0 ** jnp.linspace(0.0, 1.0, d // 2, dtype=jnp.float32))
    ang = pos.astype(jnp.float32)[:, None] * inv[None]
    cos = jnp.cos(ang)[None, :, None, :]
    sin = jnp.sin(ang)[None, :, None, :]
    xf = x.astype(jnp.float32)
    x1, x2 = xf[..., :d // 2], xf[..., d // 2:]
    return jnp.concatenate([x1 * cos - x2 * sin, x1 * sin + x2 * cos], -1)


def retention_scan(q, k, v, carry, log_gamma, pad, clen):
    def step(r, qkv):
        qc, kc, vc = qkv
        t = qc.shape[1]
        n = jnp.arange(t, dtype=jnp.float32)
        diff = n[:, None] - n[None, :]
        causal = diff >= 0
        dmat = jnp.where(causal[None], jnp.exp(jnp.where(causal, diff, 0.0)[None] * log_gamma[:, None, None]), 0.0)
        scores = jnp.einsum('bnhd,bmhd->bhnm', qc, kc) * dmat[None]
        o = jnp.einsum('bhnm,bmhe->bnhe', scores, vc)
        inter = jnp.exp((n[:, None] + 1.0) * log_gamma[None])
        o = o + jnp.einsum('bnhd,bhde->bnhe', qc, r) * inter[None, :, :, None]
        zeta = jnp.exp((t - 1.0 - n)[:, None] * log_gamma[None])
        r = (jnp.exp(t * log_gamma)[None, :, None, None] * r
             + jnp.einsum('bmhd,bmhe,mh->bhde', kc, vc, zeta))
        return r, o
    r, o = lax.scan(step, carry, (to_chunks(q, pad, clen), to_chunks(k, pad, clen), to_chunks(v, pad, clen)))
    return from_chunks(o, pad), r


def even_mixer(h, pos0, pad, clen, s5_re0, s5_im0, ret0, w_in, lam_re, lam_im, log_dt,
               b_re, b_im, c_re, c_im, s5_d, w_glu, w_out):
    f32 = jnp.float32
    bsz, t, _ = h.shape
    u, q, k, v, g = jnp.split(h @ w_in, 5, axis=-1)
    lb_re, lb_im, bb_re, bb_im = s5_discretise(lam_re.astype(f32), lam_im.astype(f32), log_dt.astype(f32),
                                               b_re.astype(f32), b_im.astype(f32))
    u_g = u.astype(f32).reshape(bsz, t, S5_GROUPS, S5_GROUP_CH)
    y5, s5_re, s5_im = s5_scan(u_g, s5_re0, s5_im0, lb_re, lb_im, bb_re, bb_im,
                               c_re.astype(f32), c_im.astype(f32), s5_d.astype(f32), pad, clen)
    y5 = jax.nn.gelu(y5.reshape(bsz, t, W_HALF))
    y5 = y5 * jax.nn.sigmoid(y5 @ w_glu.astype(f32))
    pos = pos0 + jnp.arange(t)
    heads = (bsz, t, RET_HEADS, RET_DK)
    qh = rotary(q.reshape(heads), pos)
    kh = rotary(k.reshape(heads), pos) * RET_DK ** -0.5
    vh = v.astype(f32).reshape(heads)
    log_gamma = jnp.log(1.0 - 2.0 ** (-5.0 - jnp.arange(RET_HEADS, dtype=f32)))
    yr, ret = retention_scan(qh, kh, vh, ret0, log_gamma, pad, clen)
    yr = head_norm(yr).reshape(bsz, t, W_HALF) * jax.nn.silu(g.astype(f32))
    out = jnp.concatenate([y5, yr], -1).astype(h.dtype) @ w_out
    return out, s5_re, s5_im, ret


def sb_diff_attend(q_sb, q_d, qpos, k_sb, v_sb, k_d, v_d, kpos, kvalid, lam):
    z = jnp.einsum('bqhd,bkhd->bhqk', q_sb, k_sb) * SB_DH ** -0.5
    sb_mask = kvalid[None, :] & (kpos[None, :] < qpos[:, None])
    log_beta = jnp.where(sb_mask, jax.nn.log_sigmoid(z), -jnp.inf)
    log_1mb = jnp.where(sb_mask, jax.nn.log_sigmoid(-z), 0.0)
    log_w = log_beta + lax.cumsum(log_1mb, axis=3, reverse=True) - log_1mb
    o_sb = jnp.einsum('bhqk,bkhd->bqhd', jnp.exp(log_w), v_sb)
    ch_mask = kvalid[None, :] & (kpos[None, :] // CHUNK <= qpos[:, None] // CHUNK)
    scale = DIFF_DH ** -0.5

    def probs(q, k):
        s = jnp.einsum('bqhd,bkhd->bhqk', q, k) * scale
        return jax.nn.softmax(jnp.where(ch_mask, s, NEG_BIG), axis=-1)
    a = (probs(q_d[..., :DIFF_DH], k_d[..., :DIFF_DH])
         - lam * probs(q_d[..., DIFF_DH:], k_d[..., DIFF_DH:]))
    o_d = jnp.einsum('bhqk,bkhe->bqhe', a, v_d)
    return o_sb, o_d


def odd_mixer(h, layer, caches, w_in, lq1, lk1, lq2, lk2, norm_g, w_out):
    f32 = jnp.float32
    bsz, t, _ = h.shape
    qs, ks_, vs, qd, kd, vd = jnp.split(h @ w_in, 6, axis=-1)
    sb_shape = (bsz, t, SB_HEADS, SB_DH)
    d_shape = (bsz, t, DIFF_HEADS, 2 * DIFF_DH)
    q_sb = qs.reshape(sb_shape).astype(f32)
    k_sb, v_sb = ks_.reshape(sb_shape), vs.reshape(sb_shape)
    q_d = qd.reshape(d_shape).astype(f32)
    k_d, v_d = kd.reshape(d_shape), vd.reshape(d_shape)
    lam_init = 0.8 - 0.6 * math.exp(-0.3 * layer)
    lam = (jnp.exp(jnp.sum(lq1.astype(f32) * lk1.astype(f32)))
           - jnp.exp(jnp.sum(lq2.astype(f32) * lk2.astype(f32))) + lam_init)
    if caches is None:
        lp = t + PAD
        keys = tuple(jnp.pad(a.astype(f32), ((0, 0), (PAD, 0), (0, 0), (0, 0))) for a in (k_sb, v_sb, k_d, v_d))
        kpos = jnp.arange(lp)
        kvalid = kpos >= PAD

        def block(args):
            qs_b, qd_b, qpos_b = args
            return sb_diff_attend(qs_b, qd_b, qpos_b, *keys, kpos, kvalid, lam)
        o_sb, o_d = lax.map(block, (to_chunks(q_sb, PAD, Q_BLOCK), to_chunks(q_d, PAD, Q_BLOCK),
                                    kpos.reshape(lp // Q_BLOCK, Q_BLOCK)))
        o_sb, o_d = from_chunks(o_sb, PAD), from_chunks(o_d, PAD)
    else:
        c_k_sb, c_v_sb, c_k_d, c_v_d = caches
        p = c_k_sb.shape[1]
        keys = tuple(jnp.concatenate([c.astype(f32), a.astype(f32)], 1)
                     for c, a in ((c_k_sb, k_sb), (c_v_sb, v_sb), (c_k_d, k_d), (c_v_d, v_d)))
        kpos = jnp.arange(p + t)
        kvalid = jnp.ones((p + t,), bool)
        o_sb, o_d = sb_diff_attend(q_sb, q_d, p + jnp.arange(t), *keys, kpos, kvalid, lam)
    o_d = rms_norm(o_d, norm_g) * (1.0 - lam_init)
    out = jnp.concatenate([o_sb.reshape(bsz, t, W_HALF), o_d.reshape(bsz, t, W_HALF)], -1).astype(h.dtype) @ w_out
    return out, k_sb, v_sb, k_d, v_d


def swiglu(x, w1, w3, w2):
    return (jax.nn.silu(x @ w1) * (x @ w3)) @ w2


def routed_experts(x2, eidx, gate, w1, w3, w2):
    n_tok, d = x2.shape
    n_as = n_tok * TOP_K
    flat_e = eidx.reshape(-1).astype(jnp.int32)
    flat_t = jnp.repeat(jnp.arange(n_tok, dtype=jnp.int32), TOP_K)
    counts = jnp.bincount(flat_e, length=N_EXPERTS)
    padded = (counts + EXPERT_BLOCK - 1) // EXPERT_BLOCK * EXPERT_BLOCK
    pad_end = jnp.cumsum(padded)
    pad_start = pad_end - padded
    start = jnp.cumsum(counts) - counts
    order = jnp.argsort(flat_e, stable=True)
    sorted_e = flat_e[order]
    dest_sorted = (pad_start[sorted_e] + jnp.arange(n_as) - start[sorted_e]).astype(jnp.int32)
    dest = jnp.zeros((n_as,), jnp.int32).at[order].set(dest_sorted)
    n_rows = -(-(n_as + N_EXPERTS * (EXPERT_BLOCK - 1)) // EXPERT_BLOCK) * EXPERT_BLOCK
    n_blocks = n_rows // EXPERT_BLOCK
    row_tok = jnp.zeros((n_rows,), jnp.int32).at[dest].set(flat_t)
    xs = x2[row_tok].reshape(n_blocks, EXPERT_BLOCK, d)
    blk_e = jnp.minimum(jnp.searchsorted(pad_end, jnp.arange(n_blocks) * EXPERT_BLOCK, side='right'), N_EXPERTS - 1)

    def run_block(args):
        xb, e = args
        return swiglu(xb, w1[e], w3[e], w2[e])
    ys = lax.map(run_block, (xs, blk_e)).reshape(n_rows, d)
    y_as = ys[dest].reshape(n_tok, TOP_K, d)
    return jnp.einsum('tkd,tk->td', y_as, gate.astype(x2.dtype))


def moe(x, w_router, r_bias, w1, w3, w2, sw1, sw3, sw2):
    shp = x.shape
    x2 = x.reshape(-1, shp[-1])
    n_tok = x2.shape[0]
    scores = jax.nn.sigmoid((x2 @ w_router).astype(jnp.float32))
    sel = scores + r_bias.astype(jnp.float32)
    grp = sel.reshape(n_tok, N_GROUPS, N_EXPERTS // N_GROUPS)
    grp_score = jnp.sum(lax.top_k(grp, 2)[0], -1)
    _, gidx = lax.top_k(grp_score, TOPK_GROUPS)
    gmask = jnp.zeros((n_tok, N_GROUPS), bool).at[jnp.arange(n_tok)[:, None], gidx].set(True)
    emask = jnp.repeat(gmask, N_EXPERTS // N_GROUPS, axis=1)
    _, eidx = lax.top_k(jnp.where(emask, sel, -jnp.inf), TOP_K)
    gate = jnp.take_along_axis(scores, eidx, axis=1)
    gate = gate / jnp.sum(gate, -1, keepdims=True) * ROUTED_SCALE
    y = routed_experts(x2, eidx, gate, w1, w3, w2) + swiglu(x2, sw1, sw3, sw2)
    return y.reshape(shp)


def setup_inputs(seed: int = 0) -> dict:
    key = jax.random.key(seed)
    ks = iter(jax.random.split(key, 64))
    f32 = jnp.float32

    def nrm(shape, scale=1.0):
        return jax.random.normal(next(ks), shape, f32) * scale
    lam_im = jnp.pi * jnp.arange(S5_STATE, dtype=f32)[None, None, :]
    return {
        'x_prompt': nrm((BATCH, SEQ, D_MODEL)),
        'x_sample': nrm((DEC_BATCH, DEC_SEQ, D_MODEL)),
        'cache_sb_k': nrm((N_ODD, DEC_BATCH, PAST_LEN, SB_HEADS, SB_DH)),
        'cache_sb_v': nrm((N_ODD, DEC_BATCH, PAST_LEN, SB_HEADS, SB_DH)),
        'cache_diff_k': nrm((N_ODD, DEC_BATCH, PAST_LEN, DIFF_HEADS, 2 * DIFF_DH)),
        'cache_diff_v': nrm((N_ODD, DEC_BATCH, PAST_LEN, DIFF_HEADS, 2 * DIFF_DH)),
        'state_s5_re': nrm((N_EVEN, DEC_BATCH, S5_GROUPS, S5_STATE), 0.5),
        'state_s5_im': nrm((N_EVEN, DEC_BATCH, S5_GROUPS, S5_STATE), 0.5),
        'state_ret': nrm((N_EVEN, DEC_BATCH, RET_HEADS, RET_DK, RET_DK), 0.1),
        'meta_tokens': nrm((N_META, D_MODEL)),
        'w_in_e': nrm((N_EVEN, D_MODEL, W_IN_EVEN), D_MODEL ** -0.5),
        's5_lam_re': -0.5 + nrm((N_EVEN, S5_GROUPS, S5_STATE), 0.01),
        's5_lam_im': lam_im + nrm((N_EVEN, S5_GROUPS, S5_STATE), 0.01),
        's5_log_dt': jax.random.uniform(next(ks), (N_EVEN, S5_GROUPS), f32, math.log(1e-3), math.log(1e-1)),
        's5_b_re': nrm((N_EVEN, S5_GROUPS, S5_STATE, S5_GROUP_CH), (2 * S5_GROUP_CH) ** -0.5),
        's5_b_im': nrm((N_EVEN, S5_GROUPS, S5_STATE, S5_GROUP_CH), (2 * S5_GROUP_CH) ** -0.5),
        's5_c_re': nrm((N_EVEN, S5_GROUPS, S5_GROUP_CH, S5_STATE), S5_STATE ** -0.5),
        's5_c_im': nrm((N_EVEN, S5_GROUPS, S5_GROUP_CH, S5_STATE), S5_STATE ** -0.5),
        's5_d': nrm((N_EVEN, S5_GROUPS, S5_GROUP_CH)),
        's5_w_glu': nrm((N_EVEN, W_HALF, W_HALF), W_HALF ** -0.5),
        'w_out_e': nrm((N_EVEN, W_MIX, D_MODEL), BETA * W_MIX ** -0.5),
        'w_in_o': nrm((N_ODD, D_MODEL, W_IN_ODD), D_MODEL ** -0.5),
        'diff_lq1': nrm((N_ODD, DIFF_DH), 0.1),
        'diff_lk1': nrm((N_ODD, DIFF_DH), 0.1),
        'diff_lq2': nrm((N_ODD, DIFF_DH), 0.1),
        'diff_lk2': nrm((N_ODD, DIFF_DH), 0.1),
        'diff_norm_g': 1.0 + nrm((N_ODD, 2 * DIFF_DH), 0.01),
        'w_out_o': nrm((N_ODD, W_MIX, D_MODEL), BETA * W_MIX ** -0.5),
        'ln1_g': 1.0 + nrm((DEPTH, D_MODEL), 0.01),
        'ln1_b': nrm((DEPTH, D_MODEL), 0.01),
        'ln2_g': 1.0 + nrm((DEPTH, D_MODEL), 0.01),
        'ln2_b': nrm((DEPTH, D_MODEL), 0.01),
        'router_w': nrm((DEPTH, D_MODEL, N_EXPERTS), D_MODEL ** -0.5),
        'router_bias': nrm((DEPTH, N_EXPERTS), 0.01),
        'exp_w1': nrm((DEPTH, N_EXPERTS, D_MODEL, D_EXPERT), D_MODEL ** -0.5),
        'exp_w3': nrm((DEPTH, N_EXPERTS, D_MODEL, D_EXPERT), D_MODEL ** -0.5),
        'exp_w2': nrm((DEPTH, N_EXPERTS, D_EXPERT, D_MODEL), BETA * D_EXPERT ** -0.5),
        'sh_w1': nrm((DEPTH, D_MODEL, D_SHARED), D_MODEL ** -0.5),
        'sh_w3': nrm((DEPTH, D_MODEL, D_SHARED), D_MODEL ** -0.5),
        'sh_w2': nrm((DEPTH, D_SHARED, D_MODEL), BETA * D_SHARED ** -0.5),
    }


def reference(x_prompt, x_sample, cache_sb_k, cache_sb_v, cache_diff_k, cache_diff_v,
              state_s5_re, state_s5_im, state_ret, meta_tokens, w_in_e, s5_lam_re, s5_lam_im,
              s5_log_dt, s5_b_re, s5_b_im, s5_c_re, s5_c_im, s5_d, s5_w_glu, w_out_e, w_in_o,
              diff_lq1, diff_lk1, diff_lq2, diff_lk2, diff_norm_g, w_out_o, ln1_g, ln1_b, ln2_g,
              ln2_b, router_w, router_bias, exp_w1, exp_w3, exp_w2, sh_w1, sh_w3, sh_w2):
    f32 = jnp.float32
    bsz = x_prompt.shape[0]
    bsz_s, t_s = x_sample.shape[:2]
    meta = jnp.broadcast_to(meta_tokens[None].astype(x_prompt.dtype), (bsz, N_META, D_MODEL))
    hp = jnp.concatenate([meta, x_prompt], 1)
    hs = x_sample
    sbk_p, sbv_p, dk_p, dv_p, s5r_p, s5i_p, ret_p = [], [], [], [], [], [], []
    sbk_s, sbv_s, dk_s, dv_s, s5r_s, s5i_s, ret_s = [], [], [], [], [], [], []
    for layer in range(DEPTH):
        i = layer // 2
        if layer % 2 == 0:
            ew = (w_in_e[i], s5_lam_re[i], s5_lam_im[i], s5_log_dt[i], s5_b_re[i], s5_b_im[i],
                  s5_c_re[i], s5_c_im[i], s5_d[i], s5_w_glu[i], w_out_e[i])
            z5 = jnp.zeros((bsz, S5_GROUPS, S5_STATE), f32)
            zr = jnp.zeros((bsz, RET_HEADS, RET_DK, RET_DK), f32)
            mp, a_re, a_im, a_ret = even_mixer(hp, 0, PAD, CHUNK, z5, z5, zr, *ew)
            ms, b_re, b_im, b_ret = even_mixer(hs, PAST_LEN, 0, t_s, state_s5_re[i].astype(f32),
                                               state_s5_im[i].astype(f32), state_ret[i].astype(f32), *ew)
            s5r_p.append(a_re); s5i_p.append(a_im); ret_p.append(a_ret)
            s5r_s.append(b_re); s5i_s.append(b_im); ret_s.append(b_ret)
        else:
            ow = (w_in_o[i], diff_lq1[i], diff_lk1[i], diff_lq2[i], diff_lk2[i], diff_norm_g[i], w_out_o[i])
            mp, k1, v1, k2, v2 = odd_mixer(hp, layer, None, *ow)
            ms, k3, v3, k4, v4 = odd_mixer(hs, layer, (cache_sb_k[i], cache_sb_v[i], cache_diff_k[i], cache_diff_v[i]), *ow)
            sbk_p.append(k1); sbv_p.append(v1); dk_p.append(k2); dv_p.append(v2)
            sbk_s.append(k3); sbv_s.append(v3); dk_s.append(k4); dv_s.append(v4)
        mw = (router_w[layer], router_bias[layer], exp_w1[layer], exp_w3[layer], exp_w2[layer],
              sh_w1[layer], sh_w3[layer], sh_w2[layer])
        hp = layer_norm(ALPHA * hp + mp, ln1_g[layer], ln1_b[layer])
        hp = layer_norm(ALPHA * hp + moe(hp, *mw), ln2_g[layer], ln2_b[layer])
        hs = layer_norm(ALPHA * hs + ms, ln1_g[layer], ln1_b[layer])
        hs = layer_norm(ALPHA * hs + moe(hs, *mw), ln2_g[layer], ln2_b[layer])
    y_prompt = hp[:, N_META:]
    return (y_prompt, hs,
            jnp.stack(sbk_p), jnp.stack(sbv_p), jnp.stack(dk_p), jnp.stack(dv_p),
            jnp.stack(s5r_p), jnp.stack(s5i_p), jnp.stack(ret_p),
            jnp.stack(sbk_s), jnp.stack(sbv_s), jnp.stack(dk_s), jnp.stack(dv_s),
            jnp.stack(s5r_s), jnp.stack(s5i_s), jnp.stack(ret_s))
```

```python
import functools
import math

import jax
import jax.numpy as jnp
from jax import lax
from jax.experimental import pallas as pl
from jax.experimental.pallas import tpu as pltpu

F32 = jnp.float32
BF16 = jnp.bfloat16
I32 = jnp.int32
HIGHEST = lax.Precision.HIGHEST

D_MODEL = 1024
W_HALF = 512
N_META = 16
Q_BLOCK = 128
PAD = Q_BLOCK - N_META
CHUNK = 64
CHUNK_SHIFT = 6
S5_GROUPS = 32
S5_GROUP_CH = 16
S5_STATE = 64
S5_WIDTH = S5_GROUPS * S5_STATE
RET_HEADS = 8
RET_DK = 64
SB_HEADS = 8
SB_DH = 64
DIFF_HEADS = 4
DIFF_DH = 64
N_EXPERTS = 64
N_GROUPS = 8
GROUP_SIZE = N_EXPERTS // N_GROUPS
TOPK_GROUPS = 4
TOP_K = 6
ROUTED_SCALE = 2.5
EXPERT_BLOCK = 256
EPS = 1e-5
NEG_BIG = -1e30
KEY_BLOCK = 128
VMEM_LIMIT = 56 * 1024 * 1024

NT_DIMS = (((1,), (1,)), ((), ()))


def _params(*sem):
    return pltpu.CompilerParams(dimension_semantics=sem, vmem_limit_bytes=VMEM_LIMIT)


def _tile(n, cands=(512, 256, 128, 64)):
    for c in cands:
        if n % c == 0:
            return c
    raise ValueError(f"no tile for {n}")


def _layer_norm(r, g, b):
    mu = jnp.mean(r, -1, keepdims=True)
    d = r - mu
    var = jnp.mean(d * d, -1, keepdims=True)
    return d * lax.rsqrt(var + EPS) * g + b


def _proj_kernel(x_ref, w_ref, *o_refs, cols, precise):
    x = x_ref[...]
    xb = x.astype(F32) if precise else x.astype(BF16)
    for o_ref, c0 in zip(o_refs, cols):
        wd = o_ref.shape[-1]
        w = w_ref[:, c0:c0 + wd]
        if precise:
            y = jnp.dot(xb, w, precision=HIGHEST, preferred_element_type=F32)
        else:
            y = jnp.dot(xb, w, preferred_element_type=F32)
        o_ref[...] = y.astype(o_ref.dtype)


def _project(x, w, outs, precise=False):
    n, k = x.shape
    tm = _tile(n)
    cols = tuple(o[0] for o in outs)
    return pl.pallas_call(
        functools.partial(_proj_kernel, cols=cols, precise=precise),
        grid=(n // tm,),
        in_specs=[pl.BlockSpec((tm, k), lambda i: (i, 0)),
                  pl.BlockSpec(w.shape, lambda i: (0, 0))],
        out_specs=[pl.BlockSpec((tm, o[1]), lambda i: (i, 0)) for o in outs],
        out_shape=[jax.ShapeDtypeStruct((n, o[1]), o[2]) for o in outs],
        compiler_params=_params("parallel"),
        name="in_proj",
    )(x, w)


def _s5_kernel(u_ref, x0r_ref, x0i_ref, lr_ref, li_ref, bre_ref, bim_ref, cre_ref, cim_ref, d_ref,
               wglu_ref, y_ref, fr_ref, fi_ref, xr_s, xi_s, sr_s, si_s, *, tc, precise):
    c = pl.program_id(1)

    @pl.when(c == 0)
    def _():
        xr_s[...] = x0r_ref[0]
        xi_s[...] = x0i_ref[0]

    def mm(a, b):
        if precise:
            return jnp.dot(a, b, precision=HIGHEST, preferred_element_type=F32)
        return jnp.dot(a.astype(BF16), b, preferred_element_type=F32)

    u = u_ref[...]
    sr_s[...] = mm(u, bre_ref[...])
    si_s[...] = mm(u, bim_ref[...])
    lr = lr_ref[...]
    li = li_ref[...]

    def step(t, carry):
        xr, xi = carry
        nr = lr * xr - li * xi + sr_s[pl.ds(t, 1), :]
        ni = lr * xi + li * xr + si_s[pl.ds(t, 1), :]
        sr_s[pl.ds(t, 1), :] = nr
        si_s[pl.ds(t, 1), :] = ni
        return nr, ni

    xr, xi = lax.fori_loop(0, tc, step, (xr_s[...], xi_s[...]), unroll=8)
    xr_s[...] = xr
    xi_s[...] = xi
    y = mm(sr_s[...], cre_ref[...]) - mm(si_s[...], cim_ref[...]) + d_ref[...] * u
    y = jax.nn.gelu(y)
    y = y * jax.nn.sigmoid(mm(y, wglu_ref[...]))
    y_ref[...] = y.astype(y_ref.dtype)

    @pl.when(c == pl.num_programs(1) - 1)
    def _():
        fr_ref[0] = xr
        fi_ref[0] = xi


def _s5_mixer(u, x0r, x0i, p, tc, precise):
    b, l, _ = u.shape
    nc = l // tc
    wdt = F32 if precise else BF16
    full = lambda a: pl.BlockSpec(a.shape, lambda i, j: (0,) * a.ndim)
    st_spec = pl.BlockSpec((1, 1, S5_WIDTH), lambda i, j: (i, 0, 0))
    mats = [p["bre"].astype(wdt), p["bim"].astype(wdt), p["cre"].astype(wdt), p["cim"].astype(wdt)]
    wglu = p["wglu"].astype(wdt)
    return pl.pallas_call(
        functools.partial(_s5_kernel, tc=tc, precise=precise),
        grid=(b, nc),
        in_specs=[pl.BlockSpec((None, tc, W_HALF), lambda i, j: (i, j, 0)), st_spec, st_spec,
                  full(p["lr"]), full(p["li"])] + [full(m) for m in mats] + [full(p["d"]), full(wglu)],
        out_specs=[pl.BlockSpec((None, tc, W_HALF), lambda i, j: (i, j, 0)), st_spec, st_spec],
        out_shape=[jax.ShapeDtypeStruct((b, l, W_HALF), BF16),
                   jax.ShapeDtypeStruct((b, 1, S5_WIDTH), F32),
                   jax.ShapeDtypeStruct((b, 1, S5_WIDTH), F32)],
        scratch_shapes=[pltpu.VMEM((1, S5_WIDTH), F32), pltpu.VMEM((1, S5_WIDTH), F32),
                        pltpu.VMEM((tc, S5_WIDTH), F32), pltpu.VMEM((tc, S5_WIDTH), F32)],
        compiler_params=_params("parallel", "arbitrary"),
        name="s5_mixer",
    )(u, x0r, x0i, p["lr"], p["li"], *mats, p["d"], wglu)


def _s5_prepare(lam_re, lam_im, log_dt, b_re, b_im, c_re, c_im, d, w_glu):
    dt = jnp.exp(log_dt)[:, None]
    mag = jnp.exp(lam_re * dt)
    lb_re = mag * jnp.cos(lam_im * dt)
    lb_im = mag * jnp.sin(lam_im * dt)
    nr, ni = lb_re - 1.0, lb_im
    den = jnp.square(lam_re) + jnp.square(lam_im)
    cr = (nr * lam_re + ni * lam_im) / den
    ci = (ni * lam_re - nr * lam_im) / den
    bb_re = cr[..., None] * b_re - ci[..., None] * b_im
    bb_im = cr[..., None] * b_im + ci[..., None] * b_re
    eye = jnp.eye(S5_GROUPS, dtype=F32)
    blk_in = lambda m: jnp.einsum("gpi,gh->gihp", m, eye).reshape(W_HALF, S5_WIDTH)
    blk_out = lambda m: jnp.einsum("gop,gh->gpho", m, eye).reshape(S5_WIDTH, W_HALF)
    return dict(lr=lb_re.reshape(1, S5_WIDTH), li=lb_im.reshape(1, S5_WIDTH),
                bre=blk_in(bb_re), bim=blk_in(bb_im), cre=blk_out(c_re), cim=blk_out(c_im),
                d=d.reshape(1, W_HALF), wglu=w_glu)


def _ret_kernel(q_ref, k_ref, v_ref, g_ref, cos_ref, sin_ref, dmat_ref, inter_ref, zeta_ref, gc_ref,
                r0_ref, y_ref, rf_ref, r_s, *, tc):
    c = pl.program_id(1)

    @pl.when(c == 0)
    def _():
        r_s[...] = r0_ref[0]

    lane = lax.broadcasted_iota(I32, (tc, W_HALF), 1)
    first_half = (lane & (RET_DK - 1)) < (RET_DK // 2)
    cos = cos_ref[...]
    sin = sin_ref[...]

    def rot(x):
        ahead = pltpu.roll(x, W_HALF - RET_DK // 2, 1)
        behind = pltpu.roll(x, RET_DK // 2, 1)
        return x * cos + jnp.where(first_half, ahead, behind) * sin

    q = rot(q_ref[...])
    k = rot(k_ref[...]) * (RET_DK ** -0.5)
    v = v_ref[...]
    vz = v * zeta_ref[...]
    g = g_ref[...]
    gate = g * jax.nn.sigmoid(g)
    for h in range(RET_HEADS):
        sl = slice(h * RET_DK, (h + 1) * RET_DK)
        qh = q[:, sl].astype(BF16)
        kh = k[:, sl].astype(BF16)
        s = lax.dot_general(qh, kh, NT_DIMS, preferred_element_type=F32) * dmat_ref[h]
        rh = r_s[h]
        o = jnp.dot(s.astype(BF16), v[:, sl].astype(BF16), preferred_element_type=F32)
        o = o + jnp.dot(qh, rh.astype(BF16), preferred_element_type=F32) * inter_ref[:, sl]
        kt = k[:, sl].T.astype(BF16)
        r_s[h] = gc_ref[h] * rh + jnp.dot(kt, vz[:, sl].astype(BF16), preferred_element_type=F32)
        mu = jnp.mean(o, -1, keepdims=True)
        dlt = o - mu
        var = jnp.mean(dlt * dlt, -1, keepdims=True)
        y_ref[:, sl] = (dlt * lax.rsqrt(var + EPS) * gate[:, sl]).astype(y_ref.dtype)

    @pl.when(c == pl.num_programs(1) - 1)
    def _():
        rf_ref[0] = r_s[...]


def _ret_mixer(q, k, v, g, r0, pos, tc):
    b, l, _ = q.shape
    nc = l // tc
    inv = 1.0 / (10000.0 ** jnp.linspace(0.0, 1.0, RET_DK // 2, dtype=F32))
    ang = pos.astype(F32)[:, None] * inv[None]
    cos = jnp.tile(jnp.concatenate([jnp.cos(ang), jnp.cos(ang)], 1), (1, RET_HEADS))
    sin = jnp.tile(jnp.concatenate([-jnp.sin(ang), jnp.sin(ang)], 1), (1, RET_HEADS))
    log_gamma = jnp.log(1.0 - 2.0 ** (-5.0 - jnp.arange(RET_HEADS, dtype=F32)))
    n = jnp.arange(tc, dtype=F32)
    diff = n[:, None] - n[None, :]
    causal = diff >= 0
    dmat = jnp.where(causal[None], jnp.exp(jnp.where(causal, diff, 0.0)[None] * log_gamma[:, None, None]), 0.0)
    per_lane = lambda m: jnp.repeat(m, RET_DK, axis=1)
    inter = per_lane(jnp.exp((n[:, None] + 1.0) * log_gamma[None]))
    zeta = per_lane(jnp.exp((tc - 1.0 - n)[:, None] * log_gamma[None]))
    gc = jnp.broadcast_to(jnp.exp(tc * log_gamma)[:, None, None], (RET_HEADS, RET_DK, RET_DK))
    seq = pl.BlockSpec((None, tc, W_HALF), lambda i, j: (i, j, 0))
    tab = pl.BlockSpec((tc, W_HALF), lambda i, j: (j, 0))
    full = lambda a: pl.BlockSpec(a.shape, lambda i, j: (0,) * a.ndim)
    st = pl.BlockSpec((1, RET_HEADS, RET_DK, RET_DK), lambda i, j: (i, 0, 0, 0))
    return pl.pallas_call(
        functools.partial(_ret_kernel, tc=tc),
        grid=(b, nc),
        in_specs=[seq, seq, seq, seq, tab, tab, full(dmat), full(inter), full(zeta), full(gc), st],
        out_specs=[seq, st],
        out_shape=[jax.ShapeDtypeStruct((b, l, W_HALF), BF16),
                   jax.ShapeDtypeStruct((b, RET_HEADS, RET_DK, RET_DK), F32)],
        scratch_shapes=[pltpu.VMEM((RET_HEADS, RET_DK, RET_DK), F32)],
        compiler_params=_params("parallel", "arbitrary"),
        name="ret_mixer",
    )(q, k, v, g, cos, sin, dmat, inter, zeta, gc, r0)


def _attn_kernel(lq1_ref, lk1_ref, lq2_ref, lk2_ref, ng_ref, qs_ref, qd_ref, ks_ref, vs_ref, kd_ref,
                 vd_ref, osb_ref, od_ref, *, bq, q_base, kv_lo, n_key_blocks, lam_init):
    qi = pl.program_id(1)
    q0 = q_base + qi * bq
    nkb = jnp.minimum((q0 + bq + KEY_BLOCK - 1) // KEY_BLOCK, n_key_blocks)
    qpos = q0 + lax.broadcasted_iota(I32, (bq, KEY_BLOCK), 0)
    lane = lax.broadcasted_iota(I32, (bq, KEY_BLOCK), 1)
    row = lax.broadcasted_iota(I32, (KEY_BLOCK, KEY_BLOCK), 0)
    col = lax.broadcasted_iota(I32, (KEY_BLOCK, KEY_BLOCK), 1)
    later = (row > col).astype(BF16)

    for h in range(SB_HEADS):
        sl = slice(h * SB_DH, (h + 1) * SB_DH)
        qh = qs_ref[:, sl]

        def sb_body(j, carry, sl=sl, qh=qh):
            acc, tail = carry
            k0 = pl.multiple_of((nkb - 1 - j) * KEY_BLOCK, KEY_BLOCK)
            kpos = k0 + lane
            mask = (kpos >= kv_lo) & (kpos < qpos)
            z = lax.dot_general(qh, ks_ref[pl.ds(k0, KEY_BLOCK), sl], NT_DIMS,
                                preferred_element_type=F32) * (SB_DH ** -0.5)
            sp = jnp.log1p(jnp.exp(-jnp.abs(z)))
            log_1mb = jnp.where(mask, -(jnp.maximum(z, 0.0) + sp), 0.0)
            log_beta = jnp.minimum(z, 0.0) - sp
            hi = log_1mb.astype(BF16)
            lo = (log_1mb - hi.astype(F32)).astype(BF16)
            newer = (jnp.dot(hi, later, preferred_element_type=F32)
                     + jnp.dot(lo, later, preferred_element_type=F32))
            w = jnp.where(mask, jnp.exp(log_beta + newer + tail), 0.0)
            acc = acc + jnp.dot(w.astype(BF16), vs_ref[pl.ds(k0, KEY_BLOCK), sl],
                                preferred_element_type=F32)
            tail = tail + jnp.sum(log_1mb, axis=1, keepdims=True)
            return acc, tail

        acc, _ = lax.fori_loop(0, nkb, sb_body,
                               (jnp.zeros((bq, SB_DH), F32), jnp.zeros((bq, 1), F32)))
        osb_ref[:, sl] = acc.astype(osb_ref.dtype)

    lam = (jnp.exp(jnp.sum(lq1_ref[...] * lk1_ref[...], axis=1, keepdims=True))
           - jnp.exp(jnp.sum(lq2_ref[...] * lk2_ref[...], axis=1, keepdims=True)) + lam_init)
    for h in range(DIFF_HEADS):
        c0 = h * 2 * DIFF_DH
        s1 = slice(c0, c0 + DIFF_DH)
        s2 = slice(c0 + DIFF_DH, c0 + 2 * DIFF_DH)
        sv = slice(c0, c0 + 2 * DIFF_DH)
        q1 = qd_ref[:, s1]
        q2 = qd_ref[:, s2]

        def d_body(kb, carry, s1=s1, s2=s2, sv=sv, q1=q1, q2=q2):
            k0 = pl.multiple_of(kb * KEY_BLOCK, KEY_BLOCK)
            kpos = k0 + lane
            mask = (kpos >= kv_lo) & (jnp.right_shift(kpos, CHUNK_SHIFT) <= jnp.right_shift(qpos, CHUNK_SHIFT))
            vblk = vd_ref[pl.ds(k0, KEY_BLOCK), sv]

            def upd(q, ksl, m, l, a):
                s = lax.dot_general(q, kd_ref[pl.ds(k0, KEY_BLOCK), ksl], NT_DIMS,
                                    preferred_element_type=F32) * (DIFF_DH ** -0.5)
                s = jnp.where(mask, s, NEG_BIG)
                mn = jnp.maximum(m, jnp.max(s, axis=1, keepdims=True))
                p = jnp.exp(s - mn)
                al = jnp.exp(m - mn)
                l = al * l + jnp.sum(p, axis=1, keepdims=True)
                a = al * a + jnp.dot(p.astype(BF16), vblk, preferred_element_type=F32)
                return mn, l, a

            m1, l1, a1, m2, l2, a2 = carry
            m1, l1, a1 = upd(q1, s1, m1, l1, a1)
            m2, l2, a2 = upd(q2, s2, m2, l2, a2)
            return m1, l1, a1, m2, l2, a2

        neg = jnp.full((bq, 1), NEG_BIG, F32)
        zl = jnp.zeros((bq, 1), F32)
        za = jnp.zeros((bq, 2 * DIFF_DH), F32)
        _, l1, a1, _, l2, a2 = lax.fori_loop(0, nkb, d_body, (neg, zl, za, neg, zl, za))
        o = a1 / l1 - lam * (a2 / l2)
        o = o * lax.rsqrt(jnp.mean(o * o, -1, keepdims=True) + EPS) * ng_ref[...]
        od_ref[:, sv] = (o * (1.0 - lam_init)).astype(od_ref.dtype)


def _attention(qs, qd, ks, vs, kd, vd, lam_vecs, norm_g, *, bq, q_base, kv_lo, lam_init):
    b, lq, _ = qs.shape
    lk = ks.shape[1]
    qspec = pl.BlockSpec((None, bq, W_HALF), lambda i, j: (i, j, 0))
    kspec = pl.BlockSpec((None, lk, W_HALF), lambda i, j: (i, 0, 0))
    vec = pl.BlockSpec((1, DIFF_DH), lambda i, j: (0, 0))
    return pl.pallas_call(
        functools.partial(_attn_kernel, bq=bq, q_base=q_base, kv_lo=kv_lo,
                          n_key_blocks=lk // KEY_BLOCK, lam_init=lam_init),
        grid=(b, lq // bq),
        in_specs=[vec, vec, vec, vec, pl.BlockSpec((1, 2 * DIFF_DH), lambda i, j: (0, 0)),
                  qspec, qspec, kspec, kspec, kspec, kspec],
        out_specs=[qspec, qspec],
        out_shape=[jax.ShapeDtypeStruct((b, lq, W_HALF), BF16), jax.ShapeDtypeStruct((b, lq, W_HALF), BF16)],
        compiler_params=_params("parallel", "arbitrary"),
        name="sb_diff_attention",
    )(*lam_vecs, norm_g, qs, qd, ks, vs, kd, vd)


def _outproj_ln_kernel(ya_ref, yb_ref, h_ref, w_ref, g_ref, b_ref, o_ref, *, alpha):
    m = (jnp.dot(ya_ref[...].astype(BF16), w_ref[:W_HALF, :], preferred_element_type=F32)
         + jnp.dot(yb_ref[...].astype(BF16), w_ref[W_HALF:, :], preferred_element_type=F32))
    o_ref[...] = _layer_norm(alpha * h_ref[...] + m, g_ref[...], b_ref[...])


def _outproj_ln(ya, yb, h, w, g, b, alpha):
    n = h.shape[0]
    tm = _tile(n)
    row = lambda wd: pl.BlockSpec((tm, wd), lambda i: (i, 0))
    full = lambda a: pl.BlockSpec(a.shape, lambda i: (0, 0))
    return pl.pallas_call(
        functools.partial(_outproj_ln_kernel, alpha=alpha),
        grid=(n // tm,),
        in_specs=[row(W_HALF), row(W_HALF), row(D_MODEL), full(w), full(g), full(b)],
        out_specs=row(D_MODEL),
        out_shape=jax.ShapeDtypeStruct((n, D_MODEL), F32),
        compiler_params=_params("parallel"),
        name="out_proj_ln",
    )(ya, yb, h, w, g, b)


def _router_kernel(x_ref, wt_ref, bias_ref, eidx_ref, gate_ref, rank_ref, cnt_ref, run_s, *, tm):
    i = pl.program_id(0)

    @pl.when(i == 0)
    def _():
        run_s[...] = jnp.zeros_like(run_s)

    logits = lax.dot_general(wt_ref[...], x_ref[...], NT_DIMS, precision=HIGHEST,
                             preferred_element_type=F32)
    scores = jax.nn.sigmoid(logits)
    sel = scores + bias_ref[...]
    ninf = -jnp.inf

    sel3 = sel.reshape(N_GROUPS, GROUP_SIZE, tm)
    within = lax.broadcasted_iota(I32, (N_GROUPS, GROUP_SIZE, tm), 1)
    m1 = jnp.max(sel3, axis=1, keepdims=True)
    first = jnp.min(jnp.where(sel3 == m1, within, GROUP_SIZE), axis=1, keepdims=True)
    m2 = jnp.max(jnp.where(within == first, ninf, sel3), axis=1, keepdims=True)
    gscore = (m1 + m2).reshape(N_GROUPS, tm)

    gid = lax.broadcasted_iota(I32, (N_GROUPS, tm), 0)
    gsel = jnp.zeros((N_GROUPS, tm), jnp.bool_)
    cur = gscore
    for _ in range(TOPK_GROUPS):
        m = jnp.max(cur, axis=0, keepdims=True)
        f = jnp.min(jnp.where(cur == m, gid, N_GROUPS), axis=0, keepdims=True)
        pick = gid == f
        gsel = gsel | pick
        cur = jnp.where(pick, ninf, cur)
    emask = jnp.broadcast_to(gsel.reshape(N_GROUPS, 1, tm), (N_GROUPS, GROUP_SIZE, tm)).reshape(N_EXPERTS, tm)

    eid = lax.broadcasted_iota(I32, (N_EXPERTS, tm), 0)
    cur = jnp.where(emask, sel, ninf)
    picks, idxs, gates = [], [], []
    for _ in range(TOP_K):
        m = jnp.max(cur, axis=0, keepdims=True)
        f = jnp.min(jnp.where(cur == m, eid, N_EXPERTS), axis=0, keepdims=True)
        pick = eid == f
        picks.append(pick)
        idxs.append(f)
        gates.append(jnp.sum(jnp.where(pick, scores, 0.0), axis=0, keepdims=True))
        cur = jnp.where(pick, ninf, cur)
    gsum = gates[0]
    for gk in gates[1:]:
        gsum = gsum + gk

    chosen = picks[0]
    for pk in picks[1:]:
        chosen = chosen | pk
    chosen_f = chosen.astype(F32)
    r_ = lax.broadcasted_iota(I32, (tm, tm), 0)
    c_ = lax.broadcasted_iota(I32, (tm, tm), 1)
    earlier = (r_ < c_).astype(BF16)
    before = run_s[...] + jnp.dot(chosen_f.astype(BF16), earlier, preferred_element_type=F32)
    zrow_i = jnp.zeros((1, tm), I32)
    zrow_f = jnp.zeros((1, tm), F32)
    ranks = [jnp.sum(jnp.where(pk, before, 0.0), axis=0, keepdims=True).astype(I32) for pk in picks]
    eidx_ref[...] = jnp.concatenate(idxs + [zrow_i, zrow_i], axis=0)
    gate_ref[...] = jnp.concatenate([gk / gsum * ROUTED_SCALE for gk in gates] + [zrow_f, zrow_f], axis=0)
    rank_ref[...] = jnp.concatenate(ranks + [zrow_i, zrow_i], axis=0)
    run_s[...] = run_s[...] + jnp.sum(chosen_f, axis=1, keepdims=True)
    cnt_ref[...] = jnp.broadcast_to(run_s[...], cnt_ref.shape)


def _router(x, wt, bias):
    n = x.shape[0]
    tm = _tile(n, (512, 256, 128))
    lane_blk = pl.BlockSpec((8, tm), lambda i: (0, i))
    return pl.pallas_call(
        functools.partial(_router_kernel, tm=tm),
        grid=(n // tm,),
        in_specs=[pl.BlockSpec((tm, D_MODEL), lambda i: (i, 0)),
                  pl.BlockSpec(wt.shape, lambda i: (0, 0)),
                  pl.BlockSpec(bias.shape, lambda i: (0, 0))],
        out_specs=[lane_blk, lane_blk, lane_blk, pl.BlockSpec((N_EXPERTS, 128), lambda i: (0, 0))],
        out_shape=[jax.ShapeDtypeStruct((8, n), I32), jax.ShapeDtypeStruct((8, n), F32),
                   jax.ShapeDtypeStruct((8, n), I32), jax.ShapeDtypeStruct((N_EXPERTS, 128), F32)],
        scratch_shapes=[pltpu.VMEM((N_EXPERTS, 1), F32)],
        compiler_params=_params("arbitrary"),
        name="moe_router",
    )(x, wt, bias)


def _row_copy(src_hbm, dst_hbm, sem, s, d):
    return pltpu.make_async_copy(src_hbm.at[pl.ds(s, 1)], dst_hbm.at[pl.ds(d, 1)], sem)


def _dispatch_kernel(tok_ref, x_hbm, o_hbm, sem, *, rows):
    i = pl.program_id(0)
    base = i * rows

    def start(j, carry):
        _row_copy(x_hbm, o_hbm, sem, tok_ref[0, 0, j], base + j).start()
        return carry

    def wait(j, carry):
        _row_copy(x_hbm, o_hbm, sem, 0, 0).wait()
        return carry

    lax.fori_loop(0, rows, start, 0, unroll=8)

    @pl.when(i > 0)
    def _():
        lax.fori_loop(0, rows, wait, 0)

    @pl.when(i == pl.num_programs(0) - 1)
    def _():
        lax.fori_loop(0, rows, wait, 0)


def _dispatch(x, row_tok, n_rows):
    rows = EXPERT_BLOCK
    steps = n_rows // rows
    return pl.pallas_call(
        functools.partial(_dispatch_kernel, rows=rows),
        grid=(steps,),
        in_specs=[pl.BlockSpec((1, 1, rows), lambda i: (i, 0, 0), memory_space=pltpu.SMEM),
                  pl.BlockSpec(memory_space=pl.ANY)],
        out_specs=pl.BlockSpec(memory_space=pl.ANY),
        out_shape=jax.ShapeDtypeStruct((n_rows, D_MODEL), x.dtype),
        scratch_shapes=[pltpu.SemaphoreType.DMA(())],
        compiler_params=_params("arbitrary"),
        name="moe_dispatch",
    )(row_tok.reshape(steps, 1, rows), x)


def _ffn_kernel(be_ref, xs_ref, w1_ref, w3_ref, w2_ref, o_ref):
    x = xs_ref[...].astype(BF16)
    a = jnp.dot(x, w1_ref[0], preferred_element_type=F32)
    b = jnp.dot(x, w3_ref[0], preferred_element_type=F32)
    hdn = (a * jax.nn.sigmoid(a) * b).astype(BF16)
    o_ref[...] = jnp.dot(hdn, w2_ref[0], preferred_element_type=F32).astype(o_ref.dtype)


def _expert_ffn(xs, blk_e, w1, w3, w2):
    n_rows = xs.shape[0]
    nb = n_rows // EXPERT_BLOCK
    de = w1.shape[-1]
    grid_spec = pltpu.PrefetchScalarGridSpec(
        num_scalar_prefetch=1,
        grid=(nb,),
        in_specs=[pl.BlockSpec((EXPERT_BLOCK, D_MODEL), lambda i, be: (i, 0)),
                  pl.BlockSpec((1, D_MODEL, de), lambda i, be: (be[i], 0, 0)),
                  pl.BlockSpec((1, D_MODEL, de), lambda i, be: (be[i], 0, 0)),
                  pl.BlockSpec((1, de, D_MODEL), lambda i, be: (be[i], 0, 0))],
        out_specs=pl.BlockSpec((EXPERT_BLOCK, D_MODEL), lambda i, be: (i, 0)),
    )
    return pl.pallas_call(
        _ffn_kernel,
        grid_spec=grid_spec,
        out_shape=jax.ShapeDtypeStruct((n_rows, D_MODEL), F32),
        compiler_params=_params("arbitrary"),
        name="moe_expert_ffn",
    )(blk_e, xs, w1, w3, w2)


def _combine_kernel(dest_ref, gate_ref, x_ref, ys_hbm, sw1_ref, sw3_ref, sw2_ref, g_ref, b_ref, o_ref,
                    buf, sem, *, tm, alpha):
    def row_copy(k, j, src):
        return pltpu.make_async_copy(ys_hbm.at[pl.ds(src, 1)], buf.at[k, pl.ds(j, 1)], sem)

    for k in range(TOP_K):
        def start(j, carry, k=k):
            row_copy(k, j, dest_ref[0, k, j]).start()
            return carry
        lax.fori_loop(0, tm, start, 0, unroll=8)

    x = x_ref[...]
    xb = x.astype(BF16)
    a = jnp.dot(xb, sw1_ref[...], preferred_element_type=F32)
    b = jnp.dot(xb, sw3_ref[...], preferred_element_type=F32)
    shared = jnp.dot((a * jax.nn.sigmoid(a) * b).astype(BF16), sw2_ref[...], preferred_element_type=F32)

    def wait(j, carry):
        row_copy(0, 0, 0).wait()
        return carry
    lax.fori_loop(0, TOP_K * tm, wait, 0)

    gate = gate_ref[...]
    routed = gate[:, 0:1] * buf[0]
    for k in range(1, TOP_K):
        routed = routed + gate[:, k:k + 1] * buf[k]
    o_ref[...] = _layer_norm(alpha * x + routed + shared, g_ref[...], b_ref[...])


def _combine(dest, gate_t, x, ys, sw1, sw3, sw2, g, b, alpha):
    n = x.shape[0]
    tm = _tile(n, (256, 128))
    steps = n // tm
    dest3 = dest.reshape(8, steps, tm).transpose(1, 0, 2)
    full = lambda a: pl.BlockSpec(a.shape, lambda i: (0,) * a.ndim)
    return pl.pallas_call(
        functools.partial(_combine_kernel, tm=tm, alpha=alpha),
        grid=(steps,),
        in_specs=[pl.BlockSpec((1, 8, tm), lambda i: (i, 0, 0), memory_space=pltpu.SMEM),
                  pl.BlockSpec((tm, 8), lambda i: (i, 0)),
                  pl.BlockSpec((tm, D_MODEL), lambda i: (i, 0)),
                  pl.BlockSpec(memory_space=pl.ANY),
                  full(sw1), full(sw3), full(sw2), full(g), full(b)],
        out_specs=pl.BlockSpec((tm, D_MODEL), lambda i: (i, 0)),
        out_shape=jax.ShapeDtypeStruct((n, D_MODEL), F32),
        scratch_shapes=[pltpu.VMEM((TOP_K, tm, D_MODEL), F32), pltpu.SemaphoreType.DMA(())],
        compiler_params=_params("arbitrary"),
        name="moe_combine_ln",
    )(dest3, gate_t, x, ys, sw1, sw3, sw2, g, b)


def _moe(x, rw_t, r_bias, w1, w3, w2, sw1, sw3, sw2, g, b, alpha):
    n = x.shape[0]
    eidx, gate, rank, cnt = _router(x, rw_t, r_bias)
    counts = cnt[:, 0].astype(I32)
    padded = (counts + EXPERT_BLOCK - 1) // EXPERT_BLOCK * EXPERT_BLOCK
    pad_end = jnp.cumsum(padded)
    pad_start = pad_end - padded
    dest = jnp.take(pad_start, eidx, axis=0) + rank
    n_rows = -(-(n * TOP_K + N_EXPERTS * (EXPERT_BLOCK - 1)) // EXPERT_BLOCK) * EXPERT_BLOCK
    nb = n_rows // EXPERT_BLOCK
    blk_e = jnp.minimum(jnp.searchsorted(pad_end, jnp.arange(nb, dtype=I32) * EXPERT_BLOCK, side="right"),
                        N_EXPERTS - 1).astype(I32)
    tok = jnp.broadcast_to(jnp.arange(n, dtype=I32)[None], (TOP_K, n))
    row_tok = jnp.zeros((n_rows,), I32).at[dest[:TOP_K].reshape(-1)].set(tok.reshape(-1))
    xs = _dispatch(x, row_tok, n_rows)
    ys = _expert_ffn(xs, blk_e, w1, w3, w2)
    return _combine(dest, gate.T, x, ys, sw1, sw3, sw2, g, b, alpha)


def kernel(x_prompt, x_sample, cache_sb_k, cache_sb_v, cache_diff_k, cache_diff_v, state_s5_re, state_s5_im, state_ret, meta_tokens, w_in_e, s5_lam_re, s5_lam_im, s5_log_dt, s5_b_re, s5_b_im, s5_c_re, s5_c_im, s5_d, s5_w_glu, w_out_e, w_in_o, diff_lq1, diff_lk1, diff_lq2, diff_lk2, diff_norm_g, w_out_o, ln1_g, ln1_b, ln2_g, ln2_b, router_w, router_bias, exp_w1, exp_w3, exp_w2, sh_w1, sh_w3, sh_w2):
    bsz, seq, _ = x_prompt.shape
    bsz_s, t_s, _ = x_sample.shape
    past = cache_sb_k.shape[2]
    depth = ln1_g.shape[0]
    alpha = (2 * depth) ** 0.25
    lp = PAD + N_META + seq
    n_p = bsz * lp
    n_s = bsz_s * t_s
    lt = N_META + seq

    meta = jnp.broadcast_to(meta_tokens[None].astype(F32), (bsz, N_META, D_MODEL))
    hp = jnp.concatenate([jnp.zeros((bsz, PAD, D_MODEL), F32), meta, x_prompt], 1)
    h = jnp.concatenate([hp.reshape(n_p, D_MODEL), x_sample.reshape(n_s, D_MODEL)], 0)

    outs_p = {k: [] for k in ("sbk", "sbv", "dk", "dv", "s5r", "s5i", "ret")}
    outs_s = {k: [] for k in ("sbk", "sbv", "dk", "dv", "s5r", "s5i", "ret")}
    half = lambda c0, dt=F32: (c0 * W_HALF, W_HALF, dt)

    for layer in range(depth):
        i = layer // 2
        h_p, h_s = h[:n_p], h[n_p:]
        if layer % 2 == 0:
            w_in = w_in_e[i]
            splits = [half(c) for c in range(5)]
            u_p, q_p, k_p, v_p, g_p = _project(h_p, w_in.astype(BF16), splits)
            u_s, q_s, k_s, v_s, g_s = _project(h_s, w_in, splits, precise=True)
            s5p = _s5_prepare(s5_lam_re[i], s5_lam_im[i], s5_log_dt[i], s5_b_re[i], s5_b_im[i],
                              s5_c_re[i], s5_c_im[i], s5_d[i], s5_w_glu[i])
            seq_p = lambda a: a.reshape(bsz, lp, W_HALF)
            seq_s = lambda a: a.reshape(bsz_s, t_s, W_HALF)
            z5 = jnp.zeros((bsz, 1, S5_WIDTH), F32)
            y5_p, fr_p, fi_p = _s5_mixer(seq_p(u_p), z5, z5, s5p, Q_BLOCK, False)
            y5_s, fr_s, fi_s = _s5_mixer(seq_s(u_s), state_s5_re[i].reshape(bsz_s, 1, S5_WIDTH),
                                         state_s5_im[i].reshape(bsz_s, 1, S5_WIDTH), s5p, t_s, True)
            yr_p, rt_p = _ret_mixer(seq_p(q_p), seq_p(k_p), seq_p(v_p), seq_p(g_p),
                                    jnp.zeros((bsz, RET_HEADS, RET_DK, RET_DK), F32),
                                    jnp.arange(lp) - PAD, Q_BLOCK)
            yr_s, rt_s = _ret_mixer(seq_s(q_s), seq_s(k_s), seq_s(v_s), seq_s(g_s),
                                    state_ret[i].astype(F32), past + jnp.arange(t_s), t_s)
            ya = jnp.concatenate([y5_p.reshape(n_p, W_HALF), y5_s.reshape(n_s, W_HALF)], 0)
            yb = jnp.concatenate([yr_p.reshape(n_p, W_HALF), yr_s.reshape(n_s, W_HALF)], 0)
            w_out = w_out_e[i]
            st = lambda a, nb_: a.reshape(nb_, S5_GROUPS, S5_STATE)
            outs_p["s5r"].append(st(fr_p, bsz)); outs_p["s5i"].append(st(fi_p, bsz)); outs_p["ret"].append(rt_p)
            outs_s["s5r"].append(st(fr_s, bsz_s)); outs_s["s5i"].append(st(fi_s, bsz_s)); outs_s["ret"].append(rt_s)
        else:
            splits = [half(0, BF16), half(1), half(2), half(3, BF16), half(4), half(5),
                      half(1, BF16), half(2, BF16), half(4, BF16), half(5, BF16)]
            (qs, ksf, vsf, qd, kdf, vdf, ksb, vsb, kdb, vdb) = _project(h, w_in_o[i].astype(BF16), splits)
            lam_init = 0.8 - 0.6 * math.exp(-0.3 * layer)
            lam_vecs = [v[i].reshape(1, DIFF_DH).astype(F32) for v in (diff_lq1, diff_lk1, diff_lq2, diff_lk2)]
            norm_g = diff_norm_g[i].reshape(1, 2 * DIFF_DH).astype(F32)
            seq_p = lambda a: a[:n_p].reshape(bsz, lp, W_HALF)
            seq_s = lambda a: a[n_p:].reshape(bsz_s, t_s, W_HALF)
            osb_p, od_p = _attention(seq_p(qs), seq_p(qd), seq_p(ksb), seq_p(vsb), seq_p(kdb), seq_p(vdb),
                                     lam_vecs, norm_g, bq=Q_BLOCK, q_base=0, kv_lo=PAD, lam_init=lam_init)
            lk_s = -(-(past + t_s) // KEY_BLOCK) * KEY_BLOCK

            def keys(cache, new):
                c = cache.reshape(bsz_s, past, W_HALF).astype(BF16)
                z = jnp.zeros((bsz_s, lk_s - past - t_s, W_HALF), BF16)
                return jnp.concatenate([c, seq_s(new), z], 1)
            osb_s, od_s = _attention(seq_s(qs), seq_s(qd), keys(cache_sb_k[i], ksb), keys(cache_sb_v[i], vsb),
                                     keys(cache_diff_k[i], kdb), keys(cache_diff_v[i], vdb),
                                     lam_vecs, norm_g, bq=t_s, q_base=past, kv_lo=0, lam_init=lam_init)
            ya = jnp.concatenate([osb_p.reshape(n_p, W_HALF), osb_s.reshape(n_s, W_HALF)], 0)
            yb = jnp.concatenate([od_p.reshape(n_p, W_HALF), od_s.reshape(n_s, W_HALF)], 0)
            w_out = w_out_o[i]
            kv_p = lambda a, hh, dd: a[:n_p].reshape(bsz, lp, W_HALF)[:, PAD:].reshape(bsz, lt, hh, dd)
            kv_s = lambda a, hh, dd: a[n_p:].reshape(bsz_s, t_s, hh, dd)
            outs_p["sbk"].append(kv_p(ksf, SB_HEADS, SB_DH)); outs_p["sbv"].append(kv_p(vsf, SB_HEADS, SB_DH))
            outs_p["dk"].append(kv_p(kdf, DIFF_HEADS, 2 * DIFF_DH)); outs_p["dv"].append(kv_p(vdf, DIFF_HEADS, 2 * DIFF_DH))
            outs_s["sbk"].append(kv_s(ksf, SB_HEADS, SB_DH)); outs_s["sbv"].append(kv_s(vsf, SB_HEADS, SB_DH))
            outs_s["dk"].append(kv_s(kdf, DIFF_HEADS, 2 * DIFF_DH)); outs_s["dv"].append(kv_s(vdf, DIFF_HEADS, 2 * DIFF_DH))

        row = lambda a: a[layer].reshape(1, D_MODEL).astype(F32)
        h = _outproj_ln(ya, yb, h, w_out.astype(BF16), row(ln1_g), row(ln1_b), alpha)
        h = _moe(h, router_w[layer].T.astype(F32), router_bias[layer].reshape(N_EXPERTS, 1).astype(F32),
                 exp_w1[layer].astype(BF16), exp_w3[layer].astype(BF16), exp_w2[layer].astype(BF16),
                 sh_w1[layer].astype(BF16), sh_w3[layer].astype(BF16), sh_w2[layer].astype(BF16),
                 row(ln2_g), row(ln2_b), alpha)

    y_prompt = h[:n_p].reshape(bsz, lp, D_MODEL)[:, PAD + N_META:]
    y_sample = h[n_p:].reshape(bsz_s, t_s, D_MODEL)
    order = ("sbk", "sbv", "dk", "dv", "s5r", "s5i", "ret")
    return ((y_prompt, y_sample) + tuple(jnp.stack(outs_p[k]) for k in order)
            + tuple(jnp.stack(outs_s[k]) for k in order))
```

```python
import functools
import math

import jax
import jax.numpy as jnp
from jax import lax
from jax.experimental import pallas as pl
from jax.experimental.pallas import tpu as pltpu

F32 = jnp.float32
BF16 = jnp.bfloat16
I32 = jnp.int32
HIGHEST = lax.Precision.HIGHEST

D_MODEL = 1024
W_HALF = 512
N_META = 16
Q_BLOCK = 128
PAD = Q_BLOCK - N_META
CHUNK = 64
CHUNK_SHIFT = 6
S5_GROUPS = 32
S5_GROUP_CH = 16
S5_STATE = 64
S5_WIDTH = S5_GROUPS * S5_STATE
RET_HEADS = 8
RET_DK = 64
SB_HEADS = 8
SB_DH = 64
DIFF_HEADS = 4
DIFF_DH = 64
N_EXPERTS = 64
N_GROUPS = 8
GROUP_SIZE = N_EXPERTS // N_GROUPS
TOPK_GROUPS = 4
TOP_K = 6
ROUTED_SCALE = 2.5
EXPERT_BLOCK = 256
EPS = 1e-5
NEG_BIG = -1e30
KEY_BLOCK = 128
KEY_SHIFT = 7
SUBLANES = 8
LANES = 128
VMEM_LIMIT = 56 * 1024 * 1024

NT_DIMS = (((1,), (1,)), ((), ()))


def _params(*sem):
    return pltpu.CompilerParams(dimension_semantics=sem, vmem_limit_bytes=VMEM_LIMIT)


def _tile(n, cands=(512, 256, 128, 64)):
    for c in cands:
        if n % c == 0:
            return c
    raise ValueError(f"no tile for {n}")


def _layer_norm(r, g, b):
    mu = jnp.mean(r, -1, keepdims=True)
    d = r - mu
    var = jnp.mean(d * d, -1, keepdims=True)
    return d * lax.rsqrt(var + EPS) * g + b


def _proj_kernel(x_ref, w_ref, *o_refs, cols, precise):
    x = x_ref[...]
    xb = x.astype(F32) if precise else x.astype(BF16)
    for o_ref, c0 in zip(o_refs, cols):
        wd = o_ref.shape[-1]
        w = w_ref[:, c0:c0 + wd]
        if precise:
            y = jnp.dot(xb, w, precision=HIGHEST, preferred_element_type=F32)
        else:
            y = jnp.dot(xb, w, preferred_element_type=F32)
        o_ref[...] = y.astype(o_ref.dtype)


def _project(x, w, outs, precise=False):
    n, k = x.shape
    tm = _tile(n)
    cols = tuple(o[0] for o in outs)
    return pl.pallas_call(
        functools.partial(_proj_kernel, cols=cols, precise=precise),
        grid=(n // tm,),
        in_specs=[pl.BlockSpec((tm, k), lambda i: (i, 0)),
                  pl.BlockSpec(w.shape, lambda i: (0, 0))],
        out_specs=[pl.BlockSpec((tm, o[1]), lambda i: (i, 0)) for o in outs],
        out_shape=[jax.ShapeDtypeStruct((n, o[1]), o[2]) for o in outs],
        compiler_params=_params("parallel"),
        name="in_proj",
    )(x, w)


def _s5_kernel(u_ref, x0r_ref, x0i_ref, lr_ref, li_ref, bre_ref, bim_ref, cre_ref, cim_ref, d_ref,
               wglu_ref, y_ref, fr_ref, fi_ref, xr_s, xi_s, sr_s, si_s, *, tc, precise):
    c = pl.program_id(1)

    @pl.when(c == 0)
    def _():
        xr_s[...] = x0r_ref[0]
        xi_s[...] = x0i_ref[0]

    def mm(a, b):
        if precise:
            return jnp.dot(a, b, precision=HIGHEST, preferred_element_type=F32)
        return jnp.dot(a.astype(BF16), b, preferred_element_type=F32)

    u = u_ref[...]
    sr_s[...] = mm(u, bre_ref[...])
    si_s[...] = mm(u, bim_ref[...])
    lr = lr_ref[...]
    li = li_ref[...]

    def step(t, carry):
        xr, xi = carry
        nr = lr * xr - li * xi + sr_s[pl.ds(t, 1), :]
        ni = lr * xi + li * xr + si_s[pl.ds(t, 1), :]
        sr_s[pl.ds(t, 1), :] = nr
        si_s[pl.ds(t, 1), :] = ni
        return nr, ni

    xr, xi = lax.fori_loop(0, tc, step, (xr_s[...], xi_s[...]), unroll=8)
    xr_s[...] = xr
    xi_s[...] = xi
    y = mm(sr_s[...], cre_ref[...]) - mm(si_s[...], cim_ref[...]) + d_ref[...] * u
    y = jax.nn.gelu(y)
    y = y * jax.nn.sigmoid(mm(y, wglu_ref[...]))
    y_ref[...] = y.astype(y_ref.dtype)

    @pl.when(c == pl.num_programs(1) - 1)
    def _():
        fr_ref[0] = xr
        fi_ref[0] = xi


def _s5_mixer(u, x0r, x0i, p, tc, precise):
    b, l, _ = u.shape
    nc = l // tc
    wdt = F32 if precise else BF16
    full = lambda a: pl.BlockSpec(a.shape, lambda i, j: (0,) * a.ndim)
    st_spec = pl.BlockSpec((1, 1, S5_WIDTH), lambda i, j: (i, 0, 0))
    mats = [p["bre"].astype(wdt), p["bim"].astype(wdt), p["cre"].astype(wdt), p["cim"].astype(wdt)]
    wglu = p["wglu"].astype(wdt)
    return pl.pallas_call(
        functools.partial(_s5_kernel, tc=tc, precise=precise),
        grid=(b, nc),
        in_specs=[pl.BlockSpec((None, tc, W_HALF), lambda i, j: (i, j, 0)), st_spec, st_spec,
                  full(p["lr"]), full(p["li"])] + [full(m) for m in mats] + [full(p["d"]), full(wglu)],
        out_specs=[pl.BlockSpec((None, tc, W_HALF), lambda i, j: (i, j, 0)), st_spec, st_spec],
        out_shape=[jax.ShapeDtypeStruct((b, l, W_HALF), BF16),
                   jax.ShapeDtypeStruct((b, 1, S5_WIDTH), F32),
                   jax.ShapeDtypeStruct((b, 1, S5_WIDTH), F32)],
        scratch_shapes=[pltpu.VMEM((1, S5_WIDTH), F32), pltpu.VMEM((1, S5_WIDTH), F32),
                        pltpu.VMEM((tc, S5_WIDTH), F32), pltpu.VMEM((tc, S5_WIDTH), F32)],
        compiler_params=_params("parallel", "arbitrary"),
        name="s5_mixer",
    )(u, x0r, x0i, p["lr"], p["li"], *mats, p["d"], wglu)


def _s5_prepare(lam_re, lam_im, log_dt, b_re, b_im, c_re, c_im, d, w_glu):
    dt = jnp.exp(log_dt)[:, None]
    mag = jnp.exp(lam_re * dt)
    lb_re = mag * jnp.cos(lam_im * dt)
    lb_im = mag * jnp.sin(lam_im * dt)
    nr, ni = lb_re - 1.0, lb_im
    den = jnp.square(lam_re) + jnp.square(lam_im)
    cr = (nr * lam_re + ni * lam_im) / den
    ci = (ni * lam_re - nr * lam_im) / den
    bb_re = cr[..., None] * b_re - ci[..., None] * b_im
    bb_im = cr[..., None] * b_im + ci[..., None] * b_re
    eye = jnp.eye(S5_GROUPS, dtype=F32)
    blk_in = lambda m: jnp.einsum("gpi,gh->gihp", m, eye).reshape(W_HALF, S5_WIDTH)
    blk_out = lambda m: jnp.einsum("gop,gh->gpho", m, eye).reshape(S5_WIDTH, W_HALF)
    return dict(lr=lb_re.reshape(1, S5_WIDTH), li=lb_im.reshape(1, S5_WIDTH),
                bre=blk_in(bb_re), bim=blk_in(bb_im), cre=blk_out(c_re), cim=blk_out(c_im),
                d=d.reshape(1, W_HALF), wglu=w_glu)


def _ret_kernel(q_ref, k_ref, v_ref, g_ref, cos_ref, sin_ref, dmat_ref, inter_ref, zeta_ref, gc_ref,
                r0_ref, y_ref, rf_ref, r_s, *, tc):
    c = pl.program_id(1)

    @pl.when(c == 0)
    def _():
        r_s[...] = r0_ref[0]

    lane = lax.broadcasted_iota(I32, (tc, W_HALF), 1)
    first_half = (lane & (RET_DK - 1)) < (RET_DK // 2)
    cos = cos_ref[...]
    sin = sin_ref[...]

    def rot(x):
        ahead = pltpu.roll(x, W_HALF - RET_DK // 2, 1)
        behind = pltpu.roll(x, RET_DK // 2, 1)
        return x * cos + jnp.where(first_half, ahead, behind) * sin

    q = rot(q_ref[...])
    k = rot(k_ref[...]) * (RET_DK ** -0.5)
    v = v_ref[...]
    vz = v * zeta_ref[...]
    g = g_ref[...]
    gate = g * jax.nn.sigmoid(g)
    for h in range(RET_HEADS):
        sl = slice(h * RET_DK, (h + 1) * RET_DK)
        qh = q[:, sl].astype(BF16)
        kh = k[:, sl].astype(BF16)
        s = lax.dot_general(qh, kh, NT_DIMS, preferred_element_type=F32) * dmat_ref[h]
        rh = r_s[h]
        o = jnp.dot(s.astype(BF16), v[:, sl].astype(BF16), preferred_element_type=F32)
        o = o + jnp.dot(qh, rh.astype(BF16), preferred_element_type=F32) * inter_ref[:, sl]
        kt = k[:, sl].T.astype(BF16)
        r_s[h] = gc_ref[h] * rh + jnp.dot(kt, vz[:, sl].astype(BF16), preferred_element_type=F32)
        mu = jnp.mean(o, -1, keepdims=True)
        dlt = o - mu
        var = jnp.mean(dlt * dlt, -1, keepdims=True)
        y_ref[:, sl] = (dlt * lax.rsqrt(var + EPS) * gate[:, sl]).astype(y_ref.dtype)

    @pl.when(c == pl.num_programs(1) - 1)
    def _():
        rf_ref[0] = r_s[...]


def _ret_mixer(q, k, v, g, r0, pos, tc):
    b, l, _ = q.shape
    nc = l // tc
    inv = 1.0 / (10000.0 ** jnp.linspace(0.0, 1.0, RET_DK // 2, dtype=F32))
    ang = pos.astype(F32)[:, None] * inv[None]
    cos = jnp.tile(jnp.concatenate([jnp.cos(ang), jnp.cos(ang)], 1), (1, RET_HEADS))
    sin = jnp.tile(jnp.concatenate([-jnp.sin(ang), jnp.sin(ang)], 1), (1, RET_HEADS))
    log_gamma = jnp.log(1.0 - 2.0 ** (-5.0 - jnp.arange(RET_HEADS, dtype=F32)))
    n = jnp.arange(tc, dtype=F32)
    diff = n[:, None] - n[None, :]
    causal = diff >= 0
    dmat = jnp.where(causal[None], jnp.exp(jnp.where(causal, diff, 0.0)[None] * log_gamma[:, None, None]), 0.0)
    per_lane = lambda m: jnp.repeat(m, RET_DK, axis=1)
    inter = per_lane(jnp.exp((n[:, None] + 1.0) * log_gamma[None]))
    zeta = per_lane(jnp.exp((tc - 1.0 - n)[:, None] * log_gamma[None]))
    gc = jnp.broadcast_to(jnp.exp(tc * log_gamma)[:, None, None], (RET_HEADS, RET_DK, RET_DK))
    seq = pl.BlockSpec((None, tc, W_HALF), lambda i, j: (i, j, 0))
    tab = pl.BlockSpec((tc, W_HALF), lambda i, j: (j, 0))
    full = lambda a: pl.BlockSpec(a.shape, lambda i, j: (0,) * a.ndim)
    st = pl.BlockSpec((1, RET_HEADS, RET_DK, RET_DK), lambda i, j: (i, 0, 0, 0))
    return pl.pallas_call(
        functools.partial(_ret_kernel, tc=tc),
        grid=(b, nc),
        in_specs=[seq, seq, seq, seq, tab, tab, full(dmat), full(inter), full(zeta), full(gc), st],
        out_specs=[seq, st],
        out_shape=[jax.ShapeDtypeStruct((b, l, W_HALF), BF16),
                   jax.ShapeDtypeStruct((b, RET_HEADS, RET_DK, RET_DK), F32)],
        scratch_shapes=[pltpu.VMEM((RET_HEADS, RET_DK, RET_DK), F32)],
        compiler_params=_params("parallel", "arbitrary"),
        name="ret_mixer",
    )(q, k, v, g, cos, sin, dmat, inter, zeta, gc, r0)


def _attn_kernel(lq1_ref, lk1_ref, lq2_ref, lk2_ref, ng_ref, qs_ref, qd_ref, ks_ref, vst_ref, kd_ref,
                 vdt_ref, osb_ref, od_ref, *, bq, q_base, kv_lo, n_key_blocks, lam_init):
    qi = pl.program_id(1)
    q0 = q_base + qi * bq
    nkb = jnp.minimum(jnp.right_shift(q0 + bq + KEY_BLOCK - 1, KEY_SHIFT), n_key_blocks)
    qpos = q0 + lax.broadcasted_iota(I32, (KEY_BLOCK, bq), 1)
    koff = lax.broadcasted_iota(I32, (KEY_BLOCK, bq), 0)
    qchunk = jnp.right_shift(qpos, CHUNK_SHIFT)
    row = lax.broadcasted_iota(I32, (KEY_BLOCK, KEY_BLOCK), 0)
    col = lax.broadcasted_iota(I32, (KEY_BLOCK, KEY_BLOCK), 1)
    newer_keys = (col > row).astype(BF16)

    qs = (qs_ref[...].astype(F32) * (SB_DH ** -0.5)).astype(BF16)
    qd = (qd_ref[...].astype(F32) * (DIFF_DH ** -0.5)).astype(BF16)
    sb_q = [qs[:, h * SB_DH:(h + 1) * SB_DH] for h in range(SB_HEADS)]
    d_q = [qd[:, c * DIFF_DH:(c + 1) * DIFF_DH] for c in range(2 * DIFF_HEADS)]

    def body(j, carry):
        sb_acc, sb_tail, d_m, d_l, d_a = carry
        kb = nkb - 1 - j
        kpos = kb * KEY_BLOCK + koff
        mask = (kpos >= kv_lo) & (kpos < qpos)
        kblk = ks_ref[kb]
        dmask_pos = j * KEY_BLOCK + koff
        dmask = (dmask_pos >= kv_lo) & (jnp.right_shift(dmask_pos, CHUNK_SHIFT) <= qchunk)
        dblk = kd_ref[j]
        zs = [lax.dot_general(kblk[:, h * SB_DH:(h + 1) * SB_DH], sb_q[h], NT_DIMS,
                              preferred_element_type=F32) for h in range(SB_HEADS)]
        ss = [lax.dot_general(dblk[:, c * DIFF_DH:(c + 1) * DIFF_DH], d_q[c], NT_DIMS,
                              preferred_element_type=F32) for c in range(2 * DIFF_HEADS)]
        log_betas, splits, new_tail = [], [], []
        for h in range(SB_HEADS):
            z = zs[h]
            sp = jnp.log(1.0 + jnp.exp(-jnp.abs(z)))
            log_1mb = jnp.where(mask, -(jnp.maximum(z, 0.0) + sp), 0.0)
            log_betas.append(jnp.minimum(z, 0.0) - sp)
            hi = log_1mb.astype(BF16)
            lo = (log_1mb - hi.astype(F32)).astype(BF16)
            splits.append(jnp.concatenate([hi, lo], axis=1))
            new_tail.append(sb_tail[h] + jnp.sum(log_1mb, axis=0, keepdims=True))
        new_m, new_l, alphas, probs = [], [], [], []
        for c in range(2 * DIFF_HEADS):
            s = jnp.where(dmask, ss[c], NEG_BIG)
            mn = jnp.maximum(d_m[c], jnp.max(s, axis=0, keepdims=True))
            p = jnp.exp(s - mn)
            al = jnp.exp(d_m[c] - mn)
            new_m.append(mn)
            new_l.append(al * d_l[c] + jnp.sum(p, axis=0, keepdims=True))
            alphas.append(al)
            probs.append(p.astype(BF16))
        newer = [jnp.dot(newer_keys, splits[h], preferred_element_type=F32) for h in range(SB_HEADS)]
        new_a = [alphas[c] * d_a[c]
                 + jnp.dot(vdt_ref[j, (c // 2) * 2 * DIFF_DH:(c // 2 + 1) * 2 * DIFF_DH, :], probs[c],
                           preferred_element_type=F32) for c in range(2 * DIFF_HEADS)]
        ws = [jnp.where(mask, jnp.exp(log_betas[h] + newer[h][:, :bq] + newer[h][:, bq:] + sb_tail[h]),
                        0.0).astype(BF16) for h in range(SB_HEADS)]
        new_acc = [sb_acc[h] + jnp.dot(vst_ref[kb, h * SB_DH:(h + 1) * SB_DH, :], ws[h],
                                       preferred_element_type=F32) for h in range(SB_HEADS)]
        return tuple(new_acc), tuple(new_tail), tuple(new_m), tuple(new_l), tuple(new_a)

    zrow = jnp.zeros((1, bq), F32)
    init = (tuple(jnp.zeros((SB_DH, bq), F32) for _ in range(SB_HEADS)),
            tuple(zrow for _ in range(SB_HEADS)),
            tuple(jnp.full((1, bq), NEG_BIG, F32) for _ in range(2 * DIFF_HEADS)),
            tuple(zrow for _ in range(2 * DIFF_HEADS)),
            tuple(jnp.zeros((2 * DIFF_DH, bq), F32) for _ in range(2 * DIFF_HEADS)))
    sb_acc, _, _, d_l, d_a = lax.fori_loop(0, nkb, body, init)

    osb_ref[...] = jnp.concatenate(sb_acc, axis=0).T.astype(osb_ref.dtype)
    lam = (jnp.exp(jnp.sum(lq1_ref[...] * lk1_ref[...], axis=1, keepdims=True))
           - jnp.exp(jnp.sum(lq2_ref[...] * lk2_ref[...], axis=1, keepdims=True)) + lam_init)
    outs = []
    for h in range(DIFF_HEADS):
        o = d_a[2 * h] / d_l[2 * h] - lam * (d_a[2 * h + 1] / d_l[2 * h + 1])
        o = o * lax.rsqrt(jnp.mean(o * o, axis=0, keepdims=True) + EPS) * ng_ref[...]
        outs.append(o * (1.0 - lam_init))
    od_ref[...] = jnp.concatenate(outs, axis=0).T.astype(od_ref.dtype)


def _attention(qs, qd, ks, vs, kd, vd, lam_vecs, norm_g, *, bq, q_base, kv_lo, lam_init):
    b, lq, _ = qs.shape
    lk = ks.shape[1]
    nkb = lk // KEY_BLOCK
    kblocks = lambda a: a.reshape(b, nkb, KEY_BLOCK, W_HALF)
    vblocks_t = lambda a: a.reshape(b, nkb, KEY_BLOCK, W_HALF).transpose(0, 1, 3, 2)
    qspec = pl.BlockSpec((None, bq, W_HALF), lambda i, j: (i, j, 0))
    kspec = pl.BlockSpec((None, nkb, KEY_BLOCK, W_HALF), lambda i, j: (i, 0, 0, 0))
    vspec = pl.BlockSpec((None, nkb, W_HALF, KEY_BLOCK), lambda i, j: (i, 0, 0, 0))
    vec = pl.BlockSpec((1, DIFF_DH), lambda i, j: (0, 0))
    return pl.pallas_call(
        functools.partial(_attn_kernel, bq=bq, q_base=q_base, kv_lo=kv_lo,
                          n_key_blocks=nkb, lam_init=lam_init),
        grid=(b, lq // bq),
        in_specs=[vec, vec, vec, vec, pl.BlockSpec((2 * DIFF_DH, 1), lambda i, j: (0, 0)),
                  qspec, qspec, kspec, vspec, kspec, vspec],
        out_specs=[qspec, qspec],
        out_shape=[jax.ShapeDtypeStruct((b, lq, W_HALF), BF16), jax.ShapeDtypeStruct((b, lq, W_HALF), BF16)],
        compiler_params=_params("parallel", "arbitrary"),
        name="sb_diff_attention",
    )(*lam_vecs, norm_g, qs, qd, kblocks(ks), vblocks_t(vs), kblocks(kd), vblocks_t(vd))


def _outproj_ln_kernel(ya_ref, yb_ref, h_ref, w_ref, g_ref, b_ref, o_ref, *, alpha):
    m = (jnp.dot(ya_ref[...].astype(BF16), w_ref[:W_HALF, :], preferred_element_type=F32)
         + jnp.dot(yb_ref[...].astype(BF16), w_ref[W_HALF:, :], preferred_element_type=F32))
    o_ref[...] = _layer_norm(alpha * h_ref[...] + m, g_ref[...], b_ref[...])


def _outproj_ln(ya, yb, h, w, g, b, alpha):
    n = h.shape[0]
    tm = _tile(n)
    row = lambda wd: pl.BlockSpec((tm, wd), lambda i: (i, 0))
    full = lambda a: pl.BlockSpec(a.shape, lambda i: (0, 0))
    return pl.pallas_call(
        functools.partial(_outproj_ln_kernel, alpha=alpha),
        grid=(n // tm,),
        in_specs=[row(W_HALF), row(W_HALF), row(D_MODEL), full(w), full(g), full(b)],
        out_specs=row(D_MODEL),
        out_shape=jax.ShapeDtypeStruct((n, D_MODEL), F32),
        compiler_params=_params("parallel"),
        name="out_proj_ln",
    )(ya, yb, h, w, g, b)


def _router_kernel(x_ref, wt_ref, bias_ref, eidx_ref, gate_ref, rank_ref, cnt_ref, run_s, *, tm):
    i = pl.program_id(0)

    @pl.when(i == 0)
    def _():
        run_s[...] = jnp.zeros_like(run_s)

    logits = lax.dot_general(wt_ref[...], x_ref[...], NT_DIMS, precision=HIGHEST,
                             preferred_element_type=F32)
    scores = jax.nn.sigmoid(logits)
    sel = scores + bias_ref[...]
    ninf = -jnp.inf

    sel3 = sel.reshape(N_GROUPS, GROUP_SIZE, tm)
    within = lax.broadcasted_iota(I32, (N_GROUPS, GROUP_SIZE, tm), 1)
    m1 = jnp.max(sel3, axis=1, keepdims=True)
    first = jnp.min(jnp.where(sel3 == m1, within, GROUP_SIZE), axis=1, keepdims=True)
    m2 = jnp.max(jnp.where(within == first, ninf, sel3), axis=1, keepdims=True)
    gscore = (m1 + m2).reshape(N_GROUPS, tm)

    gid = lax.broadcasted_iota(I32, (N_GROUPS, tm), 0)
    gsel = jnp.zeros((N_GROUPS, tm), jnp.bool_)
    cur = gscore
    for _ in range(TOPK_GROUPS):
        m = jnp.max(cur, axis=0, keepdims=True)
        f = jnp.min(jnp.where(cur == m, gid, N_GROUPS), axis=0, keepdims=True)
        pick = gid == f
        gsel = gsel | pick
        cur = jnp.where(pick, ninf, cur)
    emask = jnp.broadcast_to(gsel.reshape(N_GROUPS, 1, tm), (N_GROUPS, GROUP_SIZE, tm)).reshape(N_EXPERTS, tm)

    eid = lax.broadcasted_iota(I32, (N_EXPERTS, tm), 0)
    cur = jnp.where(emask, sel, ninf)
    picks, idxs, gates = [], [], []
    for _ in range(TOP_K):
        m = jnp.max(cur, axis=0, keepdims=True)
        f = jnp.min(jnp.where(cur == m, eid, N_EXPERTS), axis=0, keepdims=True)
        pick = eid == f
        picks.append(pick)
        idxs.append(f)
        gates.append(jnp.sum(jnp.where(pick, scores, 0.0), axis=0, keepdims=True))
        cur = jnp.where(pick, ninf, cur)
    gsum = gates[0]
    for gk in gates[1:]:
        gsum = gsum + gk

    chosen = picks[0]
    for pk in picks[1:]:
        chosen = chosen | pk
    chosen_f = chosen.astype(F32)
    r_ = lax.broadcasted_iota(I32, (tm, tm), 0)
    c_ = lax.broadcasted_iota(I32, (tm, tm), 1)
    earlier = (r_ < c_).astype(BF16)
    before = run_s[...] + jnp.dot(chosen_f.astype(BF16), earlier, preferred_element_type=F32)
    zrow_i = jnp.zeros((1, tm), I32)
    zrow_f = jnp.zeros((1, tm), F32)
    ranks = [jnp.sum(jnp.where(pk, before, 0.0), axis=0, keepdims=True).astype(I32) for pk in picks]
    eidx_ref[...] = jnp.concatenate(idxs + [zrow_i, zrow_i], axis=0)
    gate_ref[...] = jnp.concatenate([gk / gsum * ROUTED_SCALE for gk in gates] + [zrow_f, zrow_f], axis=0)
    rank_ref[...] = jnp.concatenate(ranks + [zrow_i, zrow_i], axis=0)
    run_s[...] = run_s[...] + jnp.sum(chosen_f, axis=1, keepdims=True)
    cnt_ref[...] = jnp.broadcast_to(run_s[...], cnt_ref.shape)


def _router(x, wt, bias):
    n = x.shape[0]
    tm = _tile(n, (512, 256, 128))
    lane_blk = pl.BlockSpec((8, tm), lambda i: (0, i))
    return pl.pallas_call(
        functools.partial(_router_kernel, tm=tm),
        grid=(n // tm,),
        in_specs=[pl.BlockSpec((tm, D_MODEL), lambda i: (i, 0)),
                  pl.BlockSpec(wt.shape, lambda i: (0, 0)),
                  pl.BlockSpec(bias.shape, lambda i: (0, 0))],
        out_specs=[lane_blk, lane_blk, lane_blk, pl.BlockSpec((N_EXPERTS, 128), lambda i: (0, 0))],
        out_shape=[jax.ShapeDtypeStruct((8, n), I32), jax.ShapeDtypeStruct((8, n), F32),
                   jax.ShapeDtypeStruct((8, n), I32), jax.ShapeDtypeStruct((N_EXPERTS, 128), F32)],
        scratch_shapes=[pltpu.VMEM((N_EXPERTS, 1), F32)],
        compiler_params=_params("arbitrary"),
        name="moe_router",
    )(x, wt, bias)


def _to_tile_rows(dst_ref, x, r0=0):
    rows = x.shape[0]
    for c in range(SUBLANES):
        dst_ref[pl.ds(r0 * SUBLANES + c, rows, stride=SUBLANES), :] = x[:, c * LANES:(c + 1) * LANES]


def _from_tile_rows(src_ref, rows, r0=0):
    return jnp.concatenate([src_ref[pl.ds(r0 * SUBLANES + c, rows, stride=SUBLANES), :]
                            for c in range(SUBLANES)], axis=1)


def _dispatch_kernel(dest_ref, x_ref, o_hbm, buf, sem, *, tm):
    i = pl.program_id(0)
    slot = i % 2
    stage = buf.at[slot]
    _to_tile_rows(stage, x_ref[...])

    def wait_all():
        for _ in range(TOP_K):
            pltpu.make_async_copy(stage, o_hbm.at[pl.ds(0, tm * SUBLANES)], sem).wait()

    @pl.when(i > 0)
    def _():
        wait_all()

    for k in range(TOP_K):
        def start(j, carry, k=k):
            d = pl.multiple_of(dest_ref[0, k, j] * SUBLANES, SUBLANES)
            s = pl.multiple_of(j * SUBLANES, SUBLANES)
            pltpu.make_async_copy(stage.at[pl.ds(s, SUBLANES)], o_hbm.at[pl.ds(d, SUBLANES)], sem).start()
            return carry
        lax.fori_loop(0, tm, start, 0, unroll=8)

    @pl.when(i == pl.num_programs(0) - 1)
    def _():
        wait_all()


def _dispatch(x, dest3, n_rows, tm):
    n = x.shape[0]
    return pl.pallas_call(
        functools.partial(_dispatch_kernel, tm=tm),
        grid=(n // tm,),
        in_specs=[pl.BlockSpec((1, 8, tm), lambda i: (i, 0, 0), memory_space=pltpu.SMEM),
                  pl.BlockSpec((tm, D_MODEL), lambda i: (i, 0))],
        out_specs=pl.BlockSpec(memory_space=pl.ANY),
        out_shape=jax.ShapeDtypeStruct((n_rows * SUBLANES, LANES), F32),
        scratch_shapes=[pltpu.VMEM((2, tm * SUBLANES, LANES), F32), pltpu.SemaphoreType.DMA(())],
        compiler_params=_params("arbitrary"),
        name="moe_dispatch",
    )(dest3, x)


def _ffn_kernel(be_ref, nv_ref, xs_ref, w1_ref, w3_ref, w2_ref, o_ref):
    i = pl.program_id(0)
    x = _from_tile_rows(xs_ref, EXPERT_BLOCK)
    live = lax.broadcasted_iota(I32, (EXPERT_BLOCK, 1), 0) < nv_ref[i]
    x = jnp.where(live, x, 0.0).astype(BF16)
    a = jnp.dot(x, w1_ref[0], preferred_element_type=F32)
    b = jnp.dot(x, w3_ref[0], preferred_element_type=F32)
    hdn = (a * jax.nn.sigmoid(a) * b).astype(BF16)
    _to_tile_rows(o_ref, jnp.dot(hdn, w2_ref[0], preferred_element_type=F32))


def _expert_ffn(xs, blk_e, blk_valid, w1, w3, w2):
    nb = xs.shape[0] // (EXPERT_BLOCK * SUBLANES)
    de = w1.shape[-1]
    rows = pl.BlockSpec((EXPERT_BLOCK * SUBLANES, LANES), lambda i, be, nv: (i, 0))
    grid_spec = pltpu.PrefetchScalarGridSpec(
        num_scalar_prefetch=2,
        grid=(nb,),
        in_specs=[rows,
                  pl.BlockSpec((1, D_MODEL, de), lambda i, be, nv: (be[i], 0, 0)),
                  pl.BlockSpec((1, D_MODEL, de), lambda i, be, nv: (be[i], 0, 0)),
                  pl.BlockSpec((1, de, D_MODEL), lambda i, be, nv: (be[i], 0, 0))],
        out_specs=rows,
    )
    return pl.pallas_call(
        _ffn_kernel,
        grid_spec=grid_spec,
        out_shape=jax.ShapeDtypeStruct(xs.shape, F32),
        compiler_params=_params("arbitrary"),
        name="moe_expert_ffn",
    )(blk_e, blk_valid, xs, w1, w3, w2)


def _combine_kernel(dest_ref, gate_ref, x_ref, ys_hbm, sw1_ref, sw3_ref, sw2_ref, g_ref, b_ref, o_ref,
                    buf, sem, *, tm, alpha):
    for k in range(TOP_K):
        def start(j, carry, k=k):
            s = pl.multiple_of(dest_ref[0, k, j] * SUBLANES, SUBLANES)
            d = pl.multiple_of((k * tm + j) * SUBLANES, SUBLANES)
            pltpu.make_async_copy(ys_hbm.at[pl.ds(s, SUBLANES)], buf.at[pl.ds(d, SUBLANES)], sem).start()
            return carry
        lax.fori_loop(0, tm, start, 0, unroll=8)

    x = x_ref[...]
    xb = x.astype(BF16)
    a = jnp.dot(xb, sw1_ref[...], preferred_element_type=F32)
    b = jnp.dot(xb, sw3_ref[...], preferred_element_type=F32)
    shared = jnp.dot((a * jax.nn.sigmoid(a) * b).astype(BF16), sw2_ref[...], preferred_element_type=F32)

    for k in range(TOP_K):
        grp = pl.ds(k * tm * SUBLANES, tm * SUBLANES)
        pltpu.make_async_copy(ys_hbm.at[pl.ds(0, tm * SUBLANES)], buf.at[grp], sem).wait()

    gate = gate_ref[...]
    routed = gate[:, 0:1] * _from_tile_rows(buf, tm, 0)
    for k in range(1, TOP_K):
        routed = routed + gate[:, k:k + 1] * _from_tile_rows(buf, tm, k * tm)
    o_ref[...] = _layer_norm(alpha * x + routed + shared, g_ref[...], b_ref[...])


def _combine(dest3, gate_t, x, ys, sw1, sw3, sw2, g, b, alpha, tm):
    n = x.shape[0]
    full = lambda a: pl.BlockSpec(a.shape, lambda i: (0,) * a.ndim)
    return pl.pallas_call(
        functools.partial(_combine_kernel, tm=tm, alpha=alpha),
        grid=(n // tm,),
        in_specs=[pl.BlockSpec((1, 8, tm), lambda i: (i, 0, 0), memory_space=pltpu.SMEM),
                  pl.BlockSpec((tm, 8), lambda i: (i, 0)),
                  pl.BlockSpec((tm, D_MODEL), lambda i: (i, 0)),
                  pl.BlockSpec(memory_space=pl.ANY),
                  full(sw1), full(sw3), full(sw2), full(g), full(b)],
        out_specs=pl.BlockSpec((tm, D_MODEL), lambda i: (i, 0)),
        out_shape=jax.ShapeDtypeStruct((n, D_MODEL), F32),
        scratch_shapes=[pltpu.VMEM((TOP_K * tm * SUBLANES, LANES), F32), pltpu.SemaphoreType.DMA(())],
        compiler_params=_params("arbitrary"),
        name="moe_combine_ln",
    )(dest3, gate_t, x, ys, sw1, sw3, sw2, g, b)


def _moe(x, rw_t, r_bias, w1, w3, w2, sw1, sw3, sw2, g, b, alpha):
    n = x.shape[0]
    eidx, gate, rank, cnt = _router(x, rw_t, r_bias)
    counts = cnt[:, 0].astype(I32)
    padded = (counts + EXPERT_BLOCK - 1) // EXPERT_BLOCK * EXPERT_BLOCK
    pad_end = jnp.cumsum(padded)
    pad_start = pad_end - padded
    onehot = eidx[:, :, None] == jnp.arange(N_EXPERTS, dtype=I32)[None, None, :]
    dest = jnp.sum(jnp.where(onehot, pad_start[None, None, :], 0), axis=-1) + rank
    n_rows = -(-(n * TOP_K + N_EXPERTS * (EXPERT_BLOCK - 1)) // EXPERT_BLOCK) * EXPERT_BLOCK
    nb = n_rows // EXPERT_BLOCK
    blk_lo = jnp.arange(nb, dtype=I32) * EXPERT_BLOCK
    blk_e = jnp.minimum(jnp.sum((pad_end[None, :] <= blk_lo[:, None]).astype(I32), axis=1), N_EXPERTS - 1)
    blk_valid = jnp.clip(jnp.take(pad_start + counts, blk_e) - blk_lo, 0, EXPERT_BLOCK).astype(I32)
    tm = _tile(n, (256, 128))
    dest3 = dest.reshape(8, n // tm, tm).transpose(1, 0, 2)
    xs = _dispatch(x, dest3, n_rows, tm)
    ys = _expert_ffn(xs, blk_e, blk_valid, w1, w3, w2)
    return _combine(dest3, gate.T, x, ys, sw1, sw3, sw2, g, b, alpha, tm)


def kernel(x_prompt, x_sample, cache_sb_k, cache_sb_v, cache_diff_k, cache_diff_v, state_s5_re, state_s5_im, state_ret, meta_tokens, w_in_e, s5_lam_re, s5_lam_im, s5_log_dt, s5_b_re, s5_b_im, s5_c_re, s5_c_im, s5_d, s5_w_glu, w_out_e, w_in_o, diff_lq1, diff_lk1, diff_lq2, diff_lk2, diff_norm_g, w_out_o, ln1_g, ln1_b, ln2_g, ln2_b, router_w, router_bias, exp_w1, exp_w3, exp_w2, sh_w1, sh_w3, sh_w2):
    bsz, seq, _ = x_prompt.shape
    bsz_s, t_s, _ = x_sample.shape
    past = cache_sb_k.shape[2]
    depth = ln1_g.shape[0]
    alpha = (2 * depth) ** 0.25
    lp = PAD + N_META + seq
    n_p = bsz * lp
    n_s = bsz_s * t_s
    lt = N_META + seq

    meta = jnp.broadcast_to(meta_tokens[None].astype(F32), (bsz, N_META, D_MODEL))
    hp = jnp.concatenate([jnp.zeros((bsz, PAD, D_MODEL), F32), meta, x_prompt], 1)
    h = jnp.concatenate([hp.reshape(n_p, D_MODEL), x_sample.reshape(n_s, D_MODEL)], 0)

    outs_p = {k: [] for k in ("sbk", "sbv", "dk", "dv", "s5r", "s5i", "ret")}
    outs_s = {k: [] for k in ("sbk", "sbv", "dk", "dv", "s5r", "s5i", "ret")}
    half = lambda c0, dt=F32: (c0 * W_HALF, W_HALF, dt)

    for layer in range(depth):
        i = layer // 2
        h_p, h_s = h[:n_p], h[n_p:]
        if layer % 2 == 0:
            w_in = w_in_e[i]
            splits = [half(c) for c in range(5)]
            u_p, q_p, k_p, v_p, g_p = _project(h_p, w_in.astype(BF16), splits)
            u_s, q_s, k_s, v_s, g_s = _project(h_s, w_in, splits, precise=True)
            s5p = _s5_prepare(s5_lam_re[i], s5_lam_im[i], s5_log_dt[i], s5_b_re[i], s5_b_im[i],
                              s5_c_re[i], s5_c_im[i], s5_d[i], s5_w_glu[i])
            seq_p = lambda a: a.reshape(bsz, lp, W_HALF)
            seq_s = lambda a: a.reshape(bsz_s, t_s, W_HALF)
            z5 = jnp.zeros((bsz, 1, S5_WIDTH), F32)
            y5_p, fr_p, fi_p = _s5_mixer(seq_p(u_p), z5, z5, s5p, Q_BLOCK, False)
            y5_s, fr_s, fi_s = _s5_mixer(seq_s(u_s), state_s5_re[i].reshape(bsz_s, 1, S5_WIDTH),
                                         state_s5_im[i].reshape(bsz_s, 1, S5_WIDTH), s5p, t_s, True)
            yr_p, rt_p = _ret_mixer(seq_p(q_p), seq_p(k_p), seq_p(v_p), seq_p(g_p),
                                    jnp.zeros((bsz, RET_HEADS, RET_DK, RET_DK), F32),
                                    jnp.arange(lp) - PAD, Q_BLOCK)
            yr_s, rt_s = _ret_mixer(seq_s(q_s), seq_s(k_s), seq_s(v_s), seq_s(g_s),
                                    state_ret[i].astype(F32), past + jnp.arange(t_s), t_s)
            ya = jnp.concatenate([y5_p.reshape(n_p, W_HALF), y5_s.reshape(n_s, W_HALF)], 0)
            yb = jnp.concatenate([yr_p.reshape(n_p, W_HALF), yr_s.reshape(n_s, W_HALF)], 0)
            w_out = w_out_e[i]
            st = lambda a, nb_: a.reshape(nb_, S5_GROUPS, S5_STATE)
            outs_p["s5r"].append(st(fr_p, bsz)); outs_p["s5i"].append(st(fi_p, bsz)); outs_p["ret"].append(rt_p)
            outs_s["s5r"].append(st(fr_s, bsz_s)); outs_s["s5i"].append(st(fi_s, bsz_s)); outs_s["ret"].append(rt_s)
        else:
            splits = [half(0, BF16), half(1), half(2), half(3, BF16), half(4), half(5),
                      half(1, BF16), half(2, BF16), half(4, BF16), half(5, BF16)]
            (qs, ksf, vsf, qd, kdf, vdf, ksb, vsb, kdb, vdb) = _project(h, w_in_o[i].astype(BF16), splits)
            lam_init = 0.8 - 0.6 * math.exp(-0.3 * layer)
            lam_vecs = [v[i].reshape(1, DIFF_DH).astype(F32) for v in (diff_lq1, diff_lk1, diff_lq2, diff_lk2)]
            norm_g = diff_norm_g[i].reshape(2 * DIFF_DH, 1).astype(F32)
            seq_p = lambda a: a[:n_p].reshape(bsz, lp, W_HALF)
            seq_s = lambda a: a[n_p:].reshape(bsz_s, t_s, W_HALF)
            osb_p, od_p = _attention(seq_p(qs), seq_p(qd), seq_p(ksb), seq_p(vsb), seq_p(kdb), seq_p(vdb),
                                     lam_vecs, norm_g, bq=Q_BLOCK, q_base=0, kv_lo=PAD, lam_init=lam_init)
            bq_s = -(-t_s // Q_BLOCK) * Q_BLOCK
            lk_s = -(-(past + bq_s) // KEY_BLOCK) * KEY_BLOCK
            qpad = lambda a: jnp.pad(seq_s(a), ((0, 0), (0, bq_s - t_s), (0, 0)))

            def keys(cache, new):
                c = cache.reshape(bsz_s, past, W_HALF).astype(BF16)
                z = jnp.zeros((bsz_s, lk_s - past - t_s, W_HALF), BF16)
                return jnp.concatenate([c, seq_s(new), z], 1)
            osb_s, od_s = _attention(qpad(qs), qpad(qd), keys(cache_sb_k[i], ksb), keys(cache_sb_v[i], vsb),
                                     keys(cache_diff_k[i], kdb), keys(cache_diff_v[i], vdb),
                                     lam_vecs, norm_g, bq=bq_s, q_base=past, kv_lo=0, lam_init=lam_init)
            ya = jnp.concatenate([osb_p.reshape(n_p, W_HALF), osb_s[:, :t_s].reshape(n_s, W_HALF)], 0)
            yb = jnp.concatenate([od_p.reshape(n_p, W_HALF), od_s[:, :t_s].reshape(n_s, W_HALF)], 0)
            w_out = w_out_o[i]
            kv_p = lambda a, hh, dd: a[:n_p].reshape(bsz, lp, W_HALF)[:, PAD:].reshape(bsz, lt, hh, dd)
            kv_s = lambda a, hh, dd: a[n_p:].reshape(bsz_s, t_s, hh, dd)
            outs_p["sbk"].append(kv_p(ksf, SB_HEADS, SB_DH)); outs_p["sbv"].append(kv_p(vsf, SB_HEADS, SB_DH))
            outs_p["dk"].append(kv_p(kdf, DIFF_HEADS, 2 * DIFF_DH)); outs_p["dv"].append(kv_p(vdf, DIFF_HEADS, 2 * DIFF_DH))
            outs_s["sbk"].append(kv_s(ksf, SB_HEADS, SB_DH)); outs_s["sbv"].append(kv_s(vsf, SB_HEADS, SB_DH))
            outs_s["dk"].append(kv_s(kdf, DIFF_HEADS, 2 * DIFF_DH)); outs_s["dv"].append(kv_s(vdf, DIFF_HEADS, 2 * DIFF_DH))

        row = lambda a: a[layer].reshape(1, D_MODEL).astype(F32)
        h = _outproj_ln(ya, yb, h, w_out.astype(BF16), row(ln1_g), row(ln1_b), alpha)
        h = _moe(h, router_w[layer].T.astype(F32), router_bias[layer].reshape(N_EXPERTS, 1).astype(F32),
                 exp_w1[layer].astype(BF16), exp_w3[layer].astype(BF16), exp_w2[layer].astype(BF16),
                 sh_w1[layer].astype(BF16), sh_w3[layer].astype(BF16), sh_w2[layer].astype(BF16),
                 row(ln2_g), row(ln2_b), alpha)

    y_prompt = h[:n_p].reshape(bsz, lp, D_MODEL)[:, PAD + N_META:]
    y_sample = h[n_p:].reshape(bsz_s, t_s, D_MODEL)
    order = ("sbk", "sbv", "dk", "dv", "s5r", "s5i", "ret")
    return ((y_prompt, y_sample) + tuple(jnp.stack(outs_p[k]) for k in order)
            + tuple(jnp.stack(outs_s[k]) for k in order))
```

```python
import functools
import math

import jax
import jax.numpy as jnp
from jax import lax
from jax.experimental import pallas as pl
from jax.experimental.pallas import tpu as pltpu

F32 = jnp.float32
BF16 = jnp.bfloat16
I32 = jnp.int32
HIGHEST = lax.Precision.HIGHEST

D_MODEL = 1024
W_HALF = 512
N_META = 16
Q_BLOCK = 128
PAD = Q_BLOCK - N_META
CHUNK = 64
CHUNK_SHIFT = 6
S5_GROUPS = 32
S5_GROUP_CH = 16
S5_STATE = 64
S5_WIDTH = S5_GROUPS * S5_STATE
S5_BLOCKS = 4
RET_HEADS = 8
RET_DK = 64
SB_HEADS = 8
SB_DH = 64
DIFF_HEADS = 4
DIFF_DH = 64
N_EXPERTS = 64
N_GROUPS = 8
GROUP_SIZE = N_EXPERTS // N_GROUPS
TOPK_GROUPS = 4
TOP_K = 6
ROUTED_SCALE = 2.5
EXPERT_BLOCK = 256
EPS = 1e-5
NEG_BIG = -1e30
KEY_BLOCK = 128
KEY_SHIFT = 7
SUBLANES = 8
LANES = 128
VMEM_LIMIT = 56 * 1024 * 1024

NT_DIMS = (((1,), (1,)), ((), ()))


def _params(*sem):
    return pltpu.CompilerParams(dimension_semantics=sem, vmem_limit_bytes=VMEM_LIMIT)


def _tile(n, cands=(512, 256, 128, 64)):
    for c in cands:
        if n % c == 0:
            return c
    raise ValueError(f"no tile for {n}")


def _layer_norm(r, g, b):
    mu = jnp.mean(r, -1, keepdims=True)
    d = r - mu
    var = jnp.mean(d * d, -1, keepdims=True)
    return d * lax.rsqrt(var + EPS) * g + b


def _proj_kernel(x_ref, w_ref, *o_refs, cols, precise):
    x = x_ref[...]
    xb = x.astype(F32) if precise else x.astype(BF16)
    for o_ref, c0 in zip(o_refs, cols):
        wd = o_ref.shape[-1]
        w = w_ref[:, c0:c0 + wd]
        if precise:
            y = jnp.dot(xb, w, precision=HIGHEST, preferred_element_type=F32)
        else:
            y = jnp.dot(xb, w, preferred_element_type=F32)
        o_ref[...] = y.astype(o_ref.dtype)


def _project(x, w, outs, row0, n, precise=False):
    k = x.shape[1]
    tm = _tile(math.gcd(n, row0) if row0 else n)
    blk0 = row0 // tm
    cols = tuple(o[0] for o in outs)
    return pl.pallas_call(
        functools.partial(_proj_kernel, cols=cols, precise=precise),
        grid=(n // tm,),
        in_specs=[pl.BlockSpec((tm, k), lambda i: (i + blk0, 0)),
                  pl.BlockSpec(w.shape, lambda i: (0, 0))],
        out_specs=[pl.BlockSpec((tm, o[1]), lambda i: (i, 0)) for o in outs],
        out_shape=[jax.ShapeDtypeStruct((n, o[1]), o[2]) for o in outs],
        compiler_params=_params("parallel"),
        name="in_proj",
    )(x, w)


def _s5_kernel(u_ref, x0r_ref, x0i_ref, lr_ref, li_ref, bre_ref, bim_ref, cre_ref, cim_ref, d_ref,
               wglu_ref, y_ref, fr_ref, fi_ref, xr_s, xi_s, sr_s, si_s, *, tc, precise):
    c = pl.program_id(1)

    @pl.when(c == 0)
    def _():
        xr_s[...] = x0r_ref[0]
        xi_s[...] = x0i_ref[0]

    def mm(a, b):
        if precise:
            return jnp.dot(a, b, precision=HIGHEST, preferred_element_type=F32)
        return jnp.dot(a.astype(BF16), b, preferred_element_type=F32)

    u = u_ref[...]
    cw, sw = W_HALF // S5_BLOCKS, S5_WIDTH // S5_BLOCKS
    for b in range(S5_BLOCKS):
        ub = u[:, b * cw:(b + 1) * cw]
        sr_s[:, b * sw:(b + 1) * sw] = mm(ub, bre_ref[b])
        si_s[:, b * sw:(b + 1) * sw] = mm(ub, bim_ref[b])
    lr = lr_ref[...]
    li = li_ref[...]

    def step(t, carry):
        xr, xi = carry
        nr = lr * xr - li * xi + sr_s[pl.ds(t, 1), :]
        ni = lr * xi + li * xr + si_s[pl.ds(t, 1), :]
        sr_s[pl.ds(t, 1), :] = nr
        si_s[pl.ds(t, 1), :] = ni
        return nr, ni

    xr, xi = lax.fori_loop(0, tc, step, (xr_s[...], xi_s[...]), unroll=8)
    xr_s[...] = xr
    xi_s[...] = xi
    y = jnp.concatenate(
        [mm(sr_s[:, b * sw:(b + 1) * sw], cre_ref[b]) - mm(si_s[:, b * sw:(b + 1) * sw], cim_ref[b])
         for b in range(S5_BLOCKS)], axis=1) + d_ref[...] * u
    y = jax.nn.gelu(y)
    y = y * jax.nn.sigmoid(mm(y, wglu_ref[...]))
    y_ref[...] = y.astype(y_ref.dtype)

    @pl.when(c == pl.num_programs(1) - 1)
    def _():
        fr_ref[0] = xr
        fi_ref[0] = xi


def _s5_mixer(u, x0r, x0i, p, tc, precise):
    b, l, _ = u.shape
    nc = l // tc
    wdt = F32 if precise else BF16
    full = lambda a: pl.BlockSpec(a.shape, lambda i, j: (0,) * a.ndim)
    st_spec = pl.BlockSpec((1, 1, S5_WIDTH), lambda i, j: (i, 0, 0))
    mats = [p["bre"].astype(wdt), p["bim"].astype(wdt), p["cre"].astype(wdt), p["cim"].astype(wdt)]
    wglu = p["wglu"].astype(wdt)
    return pl.pallas_call(
        functools.partial(_s5_kernel, tc=tc, precise=precise),
        grid=(b, nc),
        in_specs=[pl.BlockSpec((None, tc, W_HALF), lambda i, j: (i, j, 0)), st_spec, st_spec,
                  full(p["lr"]), full(p["li"])] + [full(m) for m in mats] + [full(p["d"]), full(wglu)],
        out_specs=[pl.BlockSpec((None, tc, W_HALF), lambda i, j: (i, j, 0)), st_spec, st_spec],
        out_shape=[jax.ShapeDtypeStruct((b, l, W_HALF), BF16),
                   jax.ShapeDtypeStruct((b, 1, S5_WIDTH), F32),
                   jax.ShapeDtypeStruct((b, 1, S5_WIDTH), F32)],
        scratch_shapes=[pltpu.VMEM((1, S5_WIDTH), F32), pltpu.VMEM((1, S5_WIDTH), F32),
                        pltpu.VMEM((tc, S5_WIDTH), F32), pltpu.VMEM((tc, S5_WIDTH), F32)],
        compiler_params=_params("parallel", "arbitrary"),
        name="s5_mixer",
    )(u, x0r, x0i, p["lr"], p["li"], *mats, p["d"], wglu)


def _s5_prepare(lam_re, lam_im, log_dt, b_re, b_im, c_re, c_im, d, w_glu):
    dt = jnp.exp(log_dt)[:, None]
    mag = jnp.exp(lam_re * dt)
    lb_re = mag * jnp.cos(lam_im * dt)
    lb_im = mag * jnp.sin(lam_im * dt)
    nr, ni = lb_re - 1.0, lb_im
    den = jnp.square(lam_re) + jnp.square(lam_im)
    cr = (nr * lam_re + ni * lam_im) / den
    ci = (ni * lam_re - nr * lam_im) / den
    bb_re = cr[..., None] * b_re - ci[..., None] * b_im
    bb_im = cr[..., None] * b_im + ci[..., None] * b_re
    gb = S5_GROUPS // S5_BLOCKS
    eye = jnp.eye(gb, dtype=F32)
    per_blk = lambda m: m.reshape((S5_BLOCKS, gb) + m.shape[1:])
    blk_in = lambda m: jnp.einsum("bgpi,gh->bgihp", per_blk(m), eye).reshape(
        S5_BLOCKS, W_HALF // S5_BLOCKS, S5_WIDTH // S5_BLOCKS)
    blk_out = lambda m: jnp.einsum("bgop,gh->bgpho", per_blk(m), eye).reshape(
        S5_BLOCKS, S5_WIDTH // S5_BLOCKS, W_HALF // S5_BLOCKS)
    return dict(lr=lb_re.reshape(1, S5_WIDTH), li=lb_im.reshape(1, S5_WIDTH),
                bre=blk_in(bb_re), bim=blk_in(bb_im), cre=blk_out(c_re), cim=blk_out(c_im),
                d=d.reshape(1, W_HALF), wglu=w_glu)


def _ret_kernel(q_ref, k_ref, v_ref, g_ref, cos_ref, sin_ref, dmat_ref, inter_ref, zeta_ref, gc_ref,
                r0_ref, y_ref, rf_ref, r_s, *, tc):
    c = pl.program_id(1)

    @pl.when(c == 0)
    def _():
        r_s[...] = r0_ref[0]

    lane = lax.broadcasted_iota(I32, (tc, W_HALF), 1)
    first_half = (lane & (RET_DK - 1)) < (RET_DK // 2)
    cos = cos_ref[...]
    sin = sin_ref[...]

    def rot(x):
        ahead = pltpu.roll(x, W_HALF - RET_DK // 2, 1)
        behind = pltpu.roll(x, RET_DK // 2, 1)
        return x * cos + jnp.where(first_half, ahead, behind) * sin

    q = rot(q_ref[...])
    k = rot(k_ref[...]) * (RET_DK ** -0.5)
    v = v_ref[...]
    vz = v * zeta_ref[...]
    g = g_ref[...]
    gate = g * jax.nn.sigmoid(g)
    heads = [slice(h * RET_DK, (h + 1) * RET_DK) for h in range(RET_HEADS)]
    qb = q.astype(BF16)
    kb = k.astype(BF16)
    vb = v.astype(BF16)
    vzb = vz.astype(BF16)
    kt = k.T.astype(BF16)
    scores = [lax.dot_general(qb[:, sl], kb[:, sl], NT_DIMS, preferred_element_type=F32) for sl in heads]
    cross = [jnp.dot(qb[:, sl], r_s[h].astype(BF16), preferred_element_type=F32)
             for h, sl in enumerate(heads)]
    new_r = [gc_ref[h] * r_s[h] + jnp.dot(kt[sl, :], vzb[:, sl], preferred_element_type=F32)
             for h, sl in enumerate(heads)]
    decayed = [(scores[h] * dmat_ref[h]).astype(BF16) for h in range(RET_HEADS)]
    outs = [jnp.dot(decayed[h], vb[:, sl], preferred_element_type=F32) + cross[h] * inter_ref[:, sl]
            for h, sl in enumerate(heads)]
    for h, sl in enumerate(heads):
        r_s[h] = new_r[h]
        o = outs[h]
        mu = jnp.mean(o, -1, keepdims=True)
        dlt = o - mu
        var = jnp.mean(dlt * dlt, -1, keepdims=True)
        y_ref[:, sl] = (dlt * lax.rsqrt(var + EPS) * gate[:, sl]).astype(y_ref.dtype)

    @pl.when(c == pl.num_programs(1) - 1)
    def _():
        rf_ref[0] = r_s[...]


def _ret_mixer(q, k, v, g, r0, pos, tc):
    b, l, _ = q.shape
    nc = l // tc
    inv = 1.0 / (10000.0 ** jnp.linspace(0.0, 1.0, RET_DK // 2, dtype=F32))
    ang = pos.astype(F32)[:, None] * inv[None]
    cos = jnp.tile(jnp.concatenate([jnp.cos(ang), jnp.cos(ang)], 1), (1, RET_HEADS))
    sin = jnp.tile(jnp.concatenate([-jnp.sin(ang), jnp.sin(ang)], 1), (1, RET_HEADS))
    log_gamma = jnp.log(1.0 - 2.0 ** (-5.0 - jnp.arange(RET_HEADS, dtype=F32)))
    n = jnp.arange(tc, dtype=F32)
    diff = n[:, None] - n[None, :]
    causal = diff >= 0
    dmat = jnp.where(causal[None], jnp.exp(jnp.where(causal, diff, 0.0)[None] * log_gamma[:, None, None]), 0.0)
    per_lane = lambda m: jnp.repeat(m, RET_DK, axis=1)
    inter = per_lane(jnp.exp((n[:, None] + 1.0) * log_gamma[None]))
    zeta = per_lane(jnp.exp((tc - 1.0 - n)[:, None] * log_gamma[None]))
    gc = jnp.broadcast_to(jnp.exp(tc * log_gamma)[:, None, None], (RET_HEADS, RET_DK, RET_DK))
    seq = pl.BlockSpec((None, tc, W_HALF), lambda i, j: (i, j, 0))
    tab = pl.BlockSpec((tc, W_HALF), lambda i, j: (j, 0))
    full = lambda a: pl.BlockSpec(a.shape, lambda i, j: (0,) * a.ndim)
    st = pl.BlockSpec((1, RET_HEADS, RET_DK, RET_DK), lambda i, j: (i, 0, 0, 0))
    return pl.pallas_call(
        functools.partial(_ret_kernel, tc=tc),
        grid=(b, nc),
        in_specs=[seq, seq, seq, seq, tab, tab, full(dmat), full(inter), full(zeta), full(gc), st],
        out_specs=[seq, st],
        out_shape=[jax.ShapeDtypeStruct((b, l, W_HALF), BF16),
                   jax.ShapeDtypeStruct((b, RET_HEADS, RET_DK, RET_DK), F32)],
        scratch_shapes=[pltpu.VMEM((RET_HEADS, RET_DK, RET_DK), F32)],
        compiler_params=_params("parallel", "arbitrary"),
        name="ret_mixer",
    )(q, k, v, g, cos, sin, dmat, inter, zeta, gc, r0)


def _attn_kernel(lq1_ref, lk1_ref, lq2_ref, lk2_ref, ng_ref, qs_ref, qd_ref, ks_ref, vst_ref, kd_ref,
                 vdt_ref, osb_ref, od_ref, *, bq, q_base, kv_lo, n_key_blocks, lam_init):
    qi = pl.program_id(1)
    q0 = q_base + qi * bq
    nkb = jnp.minimum(jnp.right_shift(q0 + bq + KEY_BLOCK - 1, KEY_SHIFT), n_key_blocks)
    qpos = q0 + lax.broadcasted_iota(I32, (KEY_BLOCK, bq), 1)
    koff = lax.broadcasted_iota(I32, (KEY_BLOCK, bq), 0)
    qchunk = jnp.right_shift(qpos, CHUNK_SHIFT)
    row = lax.broadcasted_iota(I32, (KEY_BLOCK, KEY_BLOCK), 0)
    col = lax.broadcasted_iota(I32, (KEY_BLOCK, KEY_BLOCK), 1)
    newer_keys = (col > row).astype(BF16)

    qs = (qs_ref[...].astype(F32) * (SB_DH ** -0.5)).astype(BF16)
    qd = (qd_ref[...].astype(F32) * (DIFF_DH ** -0.5)).astype(BF16)
    sb_q = [qs[:, h * SB_DH:(h + 1) * SB_DH] for h in range(SB_HEADS)]
    d_q = [qd[:, c * DIFF_DH:(c + 1) * DIFF_DH] for c in range(2 * DIFF_HEADS)]

    def body(j, carry, edge):
        sb_acc, sb_tail, d_m, d_l, d_a = carry
        kb = nkb - 1 - j
        kblk = ks_ref[kb]
        dblk = kd_ref[j]
        if edge:
            kpos = kb * KEY_BLOCK + koff
            mask = (kpos >= kv_lo) & (kpos < qpos)
            dmask_pos = j * KEY_BLOCK + koff
            dmask = (dmask_pos >= kv_lo) & (jnp.right_shift(dmask_pos, CHUNK_SHIFT) <= qchunk)
            keep = lambda m, x, other: jnp.where(m, x, other)
        else:
            mask = dmask = None
            keep = lambda m, x, other: x
        zs = [lax.dot_general(kblk[:, h * SB_DH:(h + 1) * SB_DH], sb_q[h], NT_DIMS,
                              preferred_element_type=F32) for h in range(SB_HEADS)]
        ss = [lax.dot_general(dblk[:, c * DIFF_DH:(c + 1) * DIFF_DH], d_q[c], NT_DIMS,
                              preferred_element_type=F32) for c in range(2 * DIFF_HEADS)]
        log_betas, splits, new_tail = [], [], []
        for h in range(SB_HEADS):
            z = zs[h]
            sp = jnp.log(1.0 + jnp.exp(-jnp.abs(z)))
            log_1mb = keep(mask, -(jnp.maximum(z, 0.0) + sp), 0.0)
            log_betas.append(jnp.minimum(z, 0.0) - sp)
            hi = log_1mb.astype(BF16)
            lo = (log_1mb - hi.astype(F32)).astype(BF16)
            splits.append(jnp.concatenate([hi, lo], axis=1))
            new_tail.append(sb_tail[h] + jnp.sum(log_1mb, axis=0, keepdims=True))
        new_m, new_l, alphas, probs = [], [], [], []
        for c in range(2 * DIFF_HEADS):
            s = keep(dmask, ss[c], NEG_BIG)
            mn = jnp.maximum(d_m[c], jnp.max(s, axis=0, keepdims=True))
            p = jnp.exp(s - mn)
            al = jnp.exp(d_m[c] - mn)
            new_m.append(mn)
            new_l.append(al * d_l[c] + jnp.sum(p, axis=0, keepdims=True))
            alphas.append(al)
            probs.append(p.astype(BF16))
        newer = [jnp.dot(newer_keys, splits[h], preferred_element_type=F32) for h in range(SB_HEADS)]
        new_a = [alphas[c] * d_a[c]
                 + jnp.dot(vdt_ref[j, (c // 2) * 2 * DIFF_DH:(c // 2 + 1) * 2 * DIFF_DH, :], probs[c],
                           preferred_element_type=F32) for c in range(2 * DIFF_HEADS)]
        ws = [keep(mask, jnp.exp(log_betas[h] + newer[h][:, :bq] + newer[h][:, bq:] + sb_tail[h]),
                   0.0).astype(BF16) for h in range(SB_HEADS)]
        new_acc = [sb_acc[h] + jnp.dot(vst_ref[kb, h * SB_DH:(h + 1) * SB_DH, :], ws[h],
                                       preferred_element_type=F32) for h in range(SB_HEADS)]
        return tuple(new_acc), tuple(new_tail), tuple(new_m), tuple(new_l), tuple(new_a)

    zrow = jnp.zeros((1, bq), F32)
    init = (tuple(jnp.zeros((SB_DH, bq), F32) for _ in range(SB_HEADS)),
            tuple(zrow for _ in range(SB_HEADS)),
            tuple(jnp.full((1, bq), NEG_BIG, F32) for _ in range(2 * DIFF_HEADS)),
            tuple(zrow for _ in range(2 * DIFF_HEADS)),
            tuple(jnp.zeros((2 * DIFF_DH, bq), F32) for _ in range(2 * DIFF_HEADS)))
    edge_body = functools.partial(body, edge=True)
    carry = lax.fori_loop(0, jnp.minimum(nkb, 1), edge_body, init)
    carry = lax.fori_loop(1, nkb - 1, functools.partial(body, edge=False), carry)
    sb_acc, _, _, d_l, d_a = lax.fori_loop(jnp.maximum(nkb - 1, 1), nkb, edge_body, carry)

    osb_ref[...] = jnp.concatenate(sb_acc, axis=0).T.astype(osb_ref.dtype)
    lam = (jnp.exp(jnp.sum(lq1_ref[...] * lk1_ref[...], axis=1, keepdims=True))
           - jnp.exp(jnp.sum(lq2_ref[...] * lk2_ref[...], axis=1, keepdims=True)) + lam_init)
    outs = []
    for h in range(DIFF_HEADS):
        o = d_a[2 * h] / d_l[2 * h] - lam * (d_a[2 * h + 1] / d_l[2 * h + 1])
        o = o * lax.rsqrt(jnp.mean(o * o, axis=0, keepdims=True) + EPS) * ng_ref[...]
        outs.append(o * (1.0 - lam_init))
    od_ref[...] = jnp.concatenate(outs, axis=0).T.astype(od_ref.dtype)


def _attention(qs, qd, ks, vs, kd, vd, lam_vecs, norm_g, *, bq, q_base, kv_lo, lam_init):
    b, lq, _ = qs.shape
    lk = ks.shape[1]
    nkb = lk // KEY_BLOCK
    assert bq == KEY_BLOCK and q_base % KEY_BLOCK == 0 and 0 <= kv_lo < KEY_BLOCK
    kblocks = lambda a: a.reshape(b, nkb, KEY_BLOCK, W_HALF)
    vblocks_t = lambda a: a.reshape(b, nkb, KEY_BLOCK, W_HALF).transpose(0, 1, 3, 2)
    qspec = pl.BlockSpec((None, bq, W_HALF), lambda i, j: (i, j, 0))
    kspec = pl.BlockSpec((None, nkb, KEY_BLOCK, W_HALF), lambda i, j: (i, 0, 0, 0))
    vspec = pl.BlockSpec((None, nkb, W_HALF, KEY_BLOCK), lambda i, j: (i, 0, 0, 0))
    vec = pl.BlockSpec((1, DIFF_DH), lambda i, j: (0, 0))
    return pl.pallas_call(
        functools.partial(_attn_kernel, bq=bq, q_base=q_base, kv_lo=kv_lo,
                          n_key_blocks=nkb, lam_init=lam_init),
        grid=(b, lq // bq),
        in_specs=[vec, vec, vec, vec, pl.BlockSpec((2 * DIFF_DH, 1), lambda i, j: (0, 0)),
                  qspec, qspec, kspec, vspec, kspec, vspec],
        out_specs=[qspec, qspec],
        out_shape=[jax.ShapeDtypeStruct((b, lq, W_HALF), BF16), jax.ShapeDtypeStruct((b, lq, W_HALF), BF16)],
        compiler_params=_params("parallel", "arbitrary"),
        name="sb_diff_attention",
    )(*lam_vecs, norm_g, qs, qd, kblocks(ks), vblocks_t(vs), kblocks(kd), vblocks_t(vd))


def _outproj_ln_kernel(ya_ref, yb_ref, h_ref, w_ref, g_ref, b_ref, o_ref, *, alpha):
    m = (jnp.dot(ya_ref[...].astype(BF16), w_ref[:W_HALF, :], preferred_element_type=F32)
         + jnp.dot(yb_ref[...].astype(BF16), w_ref[W_HALF:, :], preferred_element_type=F32))
    o_ref[...] = _layer_norm(alpha * h_ref[...] + m, g_ref[...], b_ref[...])


def _outproj_ln(ya, yb, h, w, g, b, alpha):
    n = h.shape[0]
    tm = _tile(n)
    row = lambda wd: pl.BlockSpec((tm, wd), lambda i: (i, 0))
    full = lambda a: pl.BlockSpec(a.shape, lambda i: (0, 0))
    return pl.pallas_call(
        functools.partial(_outproj_ln_kernel, alpha=alpha),
        grid=(n // tm,),
        in_specs=[row(W_HALF), row(W_HALF), row(D_MODEL), full(w), full(g), full(b)],
        out_specs=row(D_MODEL),
        out_shape=jax.ShapeDtypeStruct((n, D_MODEL), F32),
        compiler_params=_params("parallel"),
        name="out_proj_ln",
    )(ya, yb, h, w, g, b)


def _router_kernel(x_ref, wt_ref, bias_ref, eidx_ref, gate_ref, rank_ref, cnt_ref, run_s, *, tm):
    i = pl.program_id(0)

    @pl.when(i == 0)
    def _():
        run_s[...] = jnp.zeros_like(run_s)

    logits = lax.dot_general(wt_ref[...], x_ref[...], NT_DIMS, precision=HIGHEST,
                             preferred_element_type=F32)
    scores = jax.nn.sigmoid(logits)
    sel = scores + bias_ref[...]
    ninf = -jnp.inf

    sel3 = sel.reshape(N_GROUPS, GROUP_SIZE, tm)
    within = lax.broadcasted_iota(I32, (N_GROUPS, GROUP_SIZE, tm), 1)
    m1 = jnp.max(sel3, axis=1, keepdims=True)
    first = jnp.min(jnp.where(sel3 == m1, within, GROUP_SIZE), axis=1, keepdims=True)
    m2 = jnp.max(jnp.where(within == first, ninf, sel3), axis=1, keepdims=True)
    gscore = (m1 + m2).reshape(N_GROUPS, tm)

    gid = lax.broadcasted_iota(I32, (N_GROUPS, tm), 0)
    gsel = jnp.zeros((N_GROUPS, tm), jnp.bool_)
    cur = gscore
    for _ in range(TOPK_GROUPS):
        m = jnp.max(cur, axis=0, keepdims=True)
        f = jnp.min(jnp.where(cur == m, gid, N_GROUPS), axis=0, keepdims=True)
        pick = gid == f
        gsel = gsel | pick
        cur = jnp.where(pick, ninf, cur)
    emask = jnp.broadcast_to(gsel.reshape(N_GROUPS, 1, tm), (N_GROUPS, GROUP_SIZE, tm)).reshape(N_EXPERTS, tm)

    eid = lax.broadcasted_iota(I32, (N_EXPERTS, tm), 0)
    cur = jnp.where(emask, sel, ninf)
    picks, idxs, gates = [], [], []
    for _ in range(TOP_K):
        m = jnp.max(cur, axis=0, keepdims=True)
        f = jnp.min(jnp.where(cur == m, eid, N_EXPERTS), axis=0, keepdims=True)
        pick = eid == f
        picks.append(pick)
        idxs.append(f)
        gates.append(jnp.sum(jnp.where(pick, scores, 0.0), axis=0, keepdims=True))
        cur = jnp.where(pick, ninf, cur)
    gsum = gates[0]
    for gk in gates[1:]:
        gsum = gsum + gk

    chosen = picks[0]
    for pk in picks[1:]:
        chosen = chosen | pk
    chosen_f = chosen.astype(F32)
    r_ = lax.broadcasted_iota(I32, (tm, tm), 0)
    c_ = lax.broadcasted_iota(I32, (tm, tm), 1)
    earlier = (r_ < c_).astype(BF16)
    before = run_s[...] + jnp.dot(chosen_f.astype(BF16), earlier, preferred_element_type=F32)
    zrow_i = jnp.zeros((1, tm), I32)
    zrow_f = jnp.zeros((1, tm), F32)
    ranks = [jnp.sum(jnp.where(pk, before, 0.0), axis=0, keepdims=True).astype(I32) for pk in picks]
    eidx_ref[...] = jnp.concatenate(idxs + [zrow_i, zrow_i], axis=0)
    gate_ref[...] = jnp.concatenate([gk / gsum * ROUTED_SCALE for gk in gates] + [zrow_f, zrow_f], axis=0)
    rank_ref[...] = jnp.concatenate(ranks + [zrow_i, zrow_i], axis=0)
    run_s[...] = run_s[...] + jnp.sum(chosen_f, axis=1, keepdims=True)
    cnt_ref[...] = jnp.broadcast_to(run_s[...], cnt_ref.shape)


def _router(x, wt, bias):
    n = x.shape[0]
    tm = _tile(n, (512, 256, 128))
    lane_blk = pl.BlockSpec((8, tm), lambda i: (0, i))
    return pl.pallas_call(
        functools.partial(_router_kernel, tm=tm),
        grid=(n // tm,),
        in_specs=[pl.BlockSpec((tm, D_MODEL), lambda i: (i, 0)),
                  pl.BlockSpec(wt.shape, lambda i: (0, 0)),
                  pl.BlockSpec(bias.shape, lambda i: (0, 0))],
        out_specs=[lane_blk, lane_blk, lane_blk, pl.BlockSpec((N_EXPERTS, 128), lambda i: (0, 0))],
        out_shape=[jax.ShapeDtypeStruct((8, n), I32), jax.ShapeDtypeStruct((8, n), F32),
                   jax.ShapeDtypeStruct((8, n), I32), jax.ShapeDtypeStruct((N_EXPERTS, 128), F32)],
        scratch_shapes=[pltpu.VMEM((N_EXPERTS, 1), F32)],
        compiler_params=_params("arbitrary"),
        name="moe_router",
    )(x, wt, bias)


ROW_SUB = D_MODEL // 2 // LANES
HIGH_HALF = -65536


def _pack_rows(dst_ref, x, r0=0):
    rows = x.shape[0]
    bits = lambda v: lax.bitcast_convert_type(v.astype(BF16).astype(F32), I32)
    w = (lax.shift_right_logical(bits(x[:, :D_MODEL // 2]), jnp.full((), 16, I32))
         | (bits(x[:, D_MODEL // 2:]) & HIGH_HALF))
    for c in range(ROW_SUB):
        dst_ref[pl.ds(r0 * ROW_SUB + c, rows, stride=ROW_SUB), :] = w[:, c * LANES:(c + 1) * LANES]


def _unpack_rows(src_ref, rows, r0=0):
    ws = [src_ref[pl.ds(r0 * ROW_SUB + c, rows, stride=ROW_SUB), :] for c in range(ROW_SUB)]
    lo = [lax.bitcast_convert_type(lax.shift_left(w, jnp.full((), 16, I32)), F32) for w in ws]
    hi = [lax.bitcast_convert_type(w & HIGH_HALF, F32) for w in ws]
    return jnp.concatenate(lo + hi, axis=1)


def _dispatch_kernel(*refs, tm):
    n_idx = TOP_K * tm // LANES
    dest_refs, (x_ref, o_hbm, buf, sem) = refs[:n_idx], refs[n_idx:]
    i = pl.program_id(0)
    slot = i % 2
    stage = buf.at[slot]
    _pack_rows(stage, x_ref[...])

    def wait_all():
        for _ in range(TOP_K):
            pltpu.make_async_copy(stage, o_hbm.at[pl.ds(0, tm * ROW_SUB)], sem).wait()

    @pl.when(i > 0)
    def _():
        wait_all()

    for k in range(TOP_K):
        for blk in range(tm // LANES):
            def start(jj, carry, k=k, blk=blk):
                d = pl.multiple_of(dest_refs[k * (tm // LANES) + blk][0, 0, jj] * ROW_SUB, ROW_SUB)
                s = pl.multiple_of((blk * LANES + jj) * ROW_SUB, ROW_SUB)
                pltpu.make_async_copy(stage.at[pl.ds(s, ROW_SUB)], o_hbm.at[pl.ds(d, ROW_SUB)], sem).start()
                return carry
            lax.fori_loop(0, LANES, start, 0, unroll=8)

    @pl.when(i == pl.num_programs(0) - 1)
    def _():
        wait_all()


def _slot_spec():
    return pl.BlockSpec((1, 1, LANES), lambda i: (i, 0, 0), memory_space=pltpu.SMEM)


def _dispatch(x, dests, n_rows, tm):
    n = x.shape[0]
    return pl.pallas_call(
        functools.partial(_dispatch_kernel, tm=tm),
        grid=(n // tm,),
        in_specs=[_slot_spec() for _ in dests] + [pl.BlockSpec((tm, D_MODEL), lambda i: (i, 0))],
        out_specs=pl.BlockSpec(memory_space=pl.ANY),
        out_shape=jax.ShapeDtypeStruct((n_rows * ROW_SUB, LANES), I32),
        scratch_shapes=[pltpu.VMEM((2, tm * ROW_SUB, LANES), I32), pltpu.SemaphoreType.DMA(())],
        compiler_params=_params("arbitrary"),
        name="moe_dispatch",
    )(*dests, x)


def _ffn_kernel(be_ref, nv_ref, xs_ref, w1_ref, w3_ref, w2_ref, o_ref):
    i = pl.program_id(0)
    x = _unpack_rows(xs_ref, EXPERT_BLOCK)
    live = lax.broadcasted_iota(I32, (EXPERT_BLOCK, 1), 0) < nv_ref[i]
    x = jnp.where(live, x, 0.0).astype(BF16)
    a = jnp.dot(x, w1_ref[0], preferred_element_type=F32)
    b = jnp.dot(x, w3_ref[0], preferred_element_type=F32)
    hdn = (a * jax.nn.sigmoid(a) * b).astype(BF16)
    _pack_rows(o_ref, jnp.dot(hdn, w2_ref[0], preferred_element_type=F32))


def _expert_ffn(xs, blk_e, blk_valid, w1, w3, w2):
    nb = xs.shape[0] // (EXPERT_BLOCK * ROW_SUB)
    de = w1.shape[-1]
    rows = pl.BlockSpec((EXPERT_BLOCK * ROW_SUB, LANES), lambda i, be, nv: (i, 0))
    grid_spec = pltpu.PrefetchScalarGridSpec(
        num_scalar_prefetch=2,
        grid=(nb,),
        in_specs=[rows,
                  pl.BlockSpec((1, D_MODEL, de), lambda i, be, nv: (be[i], 0, 0)),
                  pl.BlockSpec((1, D_MODEL, de), lambda i, be, nv: (be[i], 0, 0)),
                  pl.BlockSpec((1, de, D_MODEL), lambda i, be, nv: (be[i], 0, 0))],
        out_specs=rows,
    )
    return pl.pallas_call(
        _ffn_kernel,
        grid_spec=grid_spec,
        out_shape=jax.ShapeDtypeStruct(xs.shape, I32),
        compiler_params=_params("arbitrary"),
        name="moe_expert_ffn",
    )(blk_e, blk_valid, xs, w1, w3, w2)


def _combine_kernel(*refs, tm, alpha):
    n_idx = TOP_K * tm // LANES
    dest_refs = refs[:n_idx]
    gate_ref, x_ref, ys_hbm, sw1_ref, sw3_ref, sw2_ref, g_ref, b_ref, o_ref, buf, sem = refs[n_idx:]
    for k in range(TOP_K):
        for blk in range(tm // LANES):
            def start(jj, carry, k=k, blk=blk):
                s = pl.multiple_of(dest_refs[k * (tm // LANES) + blk][0, 0, jj] * ROW_SUB, ROW_SUB)
                d = pl.multiple_of((k * tm + blk * LANES + jj) * ROW_SUB, ROW_SUB)
                pltpu.make_async_copy(ys_hbm.at[pl.ds(s, ROW_SUB)], buf.at[pl.ds(d, ROW_SUB)], sem).start()
                return carry
            lax.fori_loop(0, LANES, start, 0, unroll=8)

    x = x_ref[...]
    xb = x.astype(BF16)
    a = jnp.dot(xb, sw1_ref[...], preferred_element_type=F32)
    b = jnp.dot(xb, sw3_ref[...], preferred_element_type=F32)
    shared = jnp.dot((a * jax.nn.sigmoid(a) * b).astype(BF16), sw2_ref[...], preferred_element_type=F32)

    for k in range(TOP_K):
        grp = pl.ds(k * tm * ROW_SUB, tm * ROW_SUB)
        pltpu.make_async_copy(ys_hbm.at[pl.ds(0, tm * ROW_SUB)], buf.at[grp], sem).wait()

    gate = gate_ref[...]
    routed = gate[:, 0:1] * _unpack_rows(buf, tm, 0)
    for k in range(1, TOP_K):
        routed = routed + gate[:, k:k + 1] * _unpack_rows(buf, tm, k * tm)
    o_ref[...] = _layer_norm(alpha * x + routed + shared, g_ref[...], b_ref[...])


def _combine(dests, gate_t, x, ys, sw1, sw3, sw2, g, b, alpha, tm):
    n = x.shape[0]
    full = lambda a: pl.BlockSpec(a.shape, lambda i: (0,) * a.ndim)
    return pl.pallas_call(
        functools.partial(_combine_kernel, tm=tm, alpha=alpha),
        grid=(n // tm,),
        in_specs=[_slot_spec() for _ in dests] + [
                  pl.BlockSpec((tm, 8), lambda i: (i, 0)),
                  pl.BlockSpec((tm, D_MODEL), lambda i: (i, 0)),
                  pl.BlockSpec(memory_space=pl.ANY),
                  full(sw1), full(sw3), full(sw2), full(g), full(b)],
        out_specs=pl.BlockSpec((tm, D_MODEL), lambda i: (i, 0)),
        out_shape=jax.ShapeDtypeStruct((n, D_MODEL), F32),
        scratch_shapes=[pltpu.VMEM((TOP_K * tm * ROW_SUB, LANES), I32), pltpu.SemaphoreType.DMA(())],
        compiler_params=_params("arbitrary"),
        name="moe_combine_ln",
    )(*dests, gate_t, x, ys, sw1, sw3, sw2, g, b)


def _moe(x, rw_t, r_bias, w1, w3, w2, sw1, sw3, sw2, g, b, alpha):
    n = x.shape[0]
    eidx, gate, rank, cnt = _router(x, rw_t, r_bias)
    counts = cnt[:, 0].astype(I32)
    padded = (counts + EXPERT_BLOCK - 1) // EXPERT_BLOCK * EXPERT_BLOCK
    pad_end = jnp.cumsum(padded)
    pad_start = pad_end - padded
    onehot = eidx[:, :, None] == jnp.arange(N_EXPERTS, dtype=I32)[None, None, :]
    dest = jnp.sum(jnp.where(onehot, pad_start[None, None, :], 0), axis=-1) + rank
    n_rows = -(-(n * TOP_K + N_EXPERTS * (EXPERT_BLOCK - 1)) // EXPERT_BLOCK) * EXPERT_BLOCK
    nb = n_rows // EXPERT_BLOCK
    blk_lo = jnp.arange(nb, dtype=I32) * EXPERT_BLOCK
    blk_e = jnp.minimum(jnp.sum((pad_end[None, :] <= blk_lo[:, None]).astype(I32), axis=1), N_EXPERTS - 1)
    blk_valid = jnp.clip(jnp.take(pad_start + counts, blk_e) - blk_lo, 0, EXPERT_BLOCK).astype(I32)
    tm = _tile(n, (256, 128))
    runs = dest.reshape(8, n // tm, tm // LANES, 1, LANES)
    dests = [runs[k, :, r] for k in range(TOP_K) for r in range(tm // LANES)]
    xs = _dispatch(x, dests, n_rows, tm)
    ys = _expert_ffn(xs, blk_e, blk_valid, w1, w3, w2)
    return _combine(dests, gate.T, x, ys, sw1, sw3, sw2, g, b, alpha, tm)


def kernel(x_prompt, x_sample, cache_sb_k, cache_sb_v, cache_diff_k, cache_diff_v, state_s5_re, state_s5_im, state_ret, meta_tokens, w_in_e, s5_lam_re, s5_lam_im, s5_log_dt, s5_b_re, s5_b_im, s5_c_re, s5_c_im, s5_d, s5_w_glu, w_out_e, w_in_o, diff_lq1, diff_lk1, diff_lq2, diff_lk2, diff_norm_g, w_out_o, ln1_g, ln1_b, ln2_g, ln2_b, router_w, router_bias, exp_w1, exp_w3, exp_w2, sh_w1, sh_w3, sh_w2):
    bsz, seq, _ = x_prompt.shape
    bsz_s, t_s, _ = x_sample.shape
    past = cache_sb_k.shape[2]
    depth = ln1_g.shape[0]
    alpha = (2 * depth) ** 0.25
    lp = PAD + N_META + seq
    n_p = bsz * lp
    n_s = bsz_s * t_s
    lt = N_META + seq

    meta = jnp.broadcast_to(meta_tokens[None].astype(F32), (bsz, N_META, D_MODEL))
    hp = jnp.concatenate([jnp.zeros((bsz, PAD, D_MODEL), F32), meta, x_prompt], 1)
    h = jnp.concatenate([hp.reshape(n_p, D_MODEL), x_sample.reshape(n_s, D_MODEL)], 0)

    outs_p = {k: [] for k in ("sbk", "sbv", "dk", "dv", "s5r", "s5i", "ret")}
    outs_s = {k: [] for k in ("sbk", "sbv", "dk", "dv", "s5r", "s5i", "ret")}
    half = lambda c0, dt=F32: (c0 * W_HALF, W_HALF, dt)

    for layer in range(depth):
        i = layer // 2
        if layer % 2 == 0:
            w_in = w_in_e[i]
            splits = [half(c) for c in range(5)]
            u_p, q_p, k_p, v_p, g_p = _project(h, w_in.astype(BF16), splits, 0, n_p)
            u_s, q_s, k_s, v_s, g_s = _project(h, w_in, splits, n_p, n_s, precise=True)
            s5p = _s5_prepare(s5_lam_re[i], s5_lam_im[i], s5_log_dt[i], s5_b_re[i], s5_b_im[i],
                              s5_c_re[i], s5_c_im[i], s5_d[i], s5_w_glu[i])
            seq_p = lambda a: a.reshape(bsz, lp, W_HALF)
            seq_s = lambda a: a.reshape(bsz_s, t_s, W_HALF)
            z5 = jnp.zeros((bsz, 1, S5_WIDTH), F32)
            y5_p, fr_p, fi_p = _s5_mixer(seq_p(u_p), z5, z5, s5p, Q_BLOCK, False)
            y5_s, fr_s, fi_s = _s5_mixer(seq_s(u_s), state_s5_re[i].reshape(bsz_s, 1, S5_WIDTH),
                                         state_s5_im[i].reshape(bsz_s, 1, S5_WIDTH), s5p, t_s, True)
            yr_p, rt_p = _ret_mixer(seq_p(q_p), seq_p(k_p), seq_p(v_p), seq_p(g_p),
                                    jnp.zeros((bsz, RET_HEADS, RET_DK, RET_DK), F32),
                                    jnp.arange(lp) - PAD, Q_BLOCK)
            yr_s, rt_s = _ret_mixer(seq_s(q_s), seq_s(k_s), seq_s(v_s), seq_s(g_s),
                                    state_ret[i].astype(F32), past + jnp.arange(t_s), t_s)
            ya = jnp.concatenate([y5_p.reshape(n_p, W_HALF), y5_s.reshape(n_s, W_HALF)], 0)
            yb = jnp.concatenate([yr_p.reshape(n_p, W_HALF), yr_s.reshape(n_s, W_HALF)], 0)
            w_out = w_out_e[i]
            st = lambda a, nb_: a.reshape(nb_, S5_GROUPS, S5_STATE)
            outs_p["s5r"].append(st(fr_p, bsz)); outs_p["s5i"].append(st(fi_p, bsz)); outs_p["ret"].append(rt_p)
            outs_s["s5r"].append(st(fr_s, bsz_s)); outs_s["s5i"].append(st(fi_s, bsz_s)); outs_s["ret"].append(rt_s)
        else:
            splits = [half(0, BF16), half(1), half(2), half(3, BF16), half(4), half(5),
                      half(1, BF16), half(2, BF16), half(4, BF16), half(5, BF16)]
            w_in = w_in_o[i].astype(BF16)
            (qs, ksf, vsf, qd, kdf, vdf, ksb, vsb, kdb, vdb) = _project(h, w_in, splits, 0, n_p)
            (qs_s, ksf_s, vsf_s, qd_s, kdf_s, vdf_s, ksb_s, vsb_s, kdb_s, vdb_s) = _project(h, w_in, splits, n_p, n_s)
            lam_init = 0.8 - 0.6 * math.exp(-0.3 * layer)
            lam_vecs = [v[i].reshape(1, DIFF_DH).astype(F32) for v in (diff_lq1, diff_lk1, diff_lq2, diff_lk2)]
            norm_g = diff_norm_g[i].reshape(2 * DIFF_DH, 1).astype(F32)
            seq_p = lambda a: a.reshape(bsz, lp, W_HALF)
            seq_s = lambda a: a.reshape(bsz_s, t_s, W_HALF)
            osb_p, od_p = _attention(seq_p(qs), seq_p(qd), seq_p(ksb), seq_p(vsb), seq_p(kdb), seq_p(vdb),
                                     lam_vecs, norm_g, bq=Q_BLOCK, q_base=0, kv_lo=PAD, lam_init=lam_init)
            bq_s = -(-t_s // Q_BLOCK) * Q_BLOCK
            lk_s = -(-(past + bq_s) // KEY_BLOCK) * KEY_BLOCK
            qpad = lambda a: jnp.pad(seq_s(a), ((0, 0), (0, bq_s - t_s), (0, 0)))

            def keys(cache, new):
                c = cache.reshape(bsz_s, past, W_HALF).astype(BF16)
                z = jnp.zeros((bsz_s, lk_s - past - t_s, W_HALF), BF16)
                return jnp.concatenate([c, seq_s(new), z], 1)
            osb_s, od_s = _attention(qpad(qs_s), qpad(qd_s), keys(cache_sb_k[i], ksb_s), keys(cache_sb_v[i], vsb_s),
                                     keys(cache_diff_k[i], kdb_s), keys(cache_diff_v[i], vdb_s),
                                     lam_vecs, norm_g, bq=bq_s, q_base=past, kv_lo=0, lam_init=lam_init)
            ya = jnp.concatenate([osb_p.reshape(n_p, W_HALF), osb_s[:, :t_s].reshape(n_s, W_HALF)], 0)
            yb = jnp.concatenate([od_p.reshape(n_p, W_HALF), od_s[:, :t_s].reshape(n_s, W_HALF)], 0)
            w_out = w_out_o[i]
            kv_p = lambda a, hh, dd: a.reshape(bsz, lp, W_HALF)[:, PAD:].reshape(bsz, lt, hh, dd)
            kv_s = lambda a, hh, dd: a.reshape(bsz_s, t_s, hh, dd)
            outs_p["sbk"].append(kv_p(ksf, SB_HEADS, SB_DH)); outs_p["sbv"].append(kv_p(vsf, SB_HEADS, SB_DH))
            outs_p["dk"].append(kv_p(kdf, DIFF_HEADS, 2 * DIFF_DH)); outs_p["dv"].append(kv_p(vdf, DIFF_HEADS, 2 * DIFF_DH))
            outs_s["sbk"].append(kv_s(ksf_s, SB_HEADS, SB_DH)); outs_s["sbv"].append(kv_s(vsf_s, SB_HEADS, SB_DH))
            outs_s["dk"].append(kv_s(kdf_s, DIFF_HEADS, 2 * DIFF_DH)); outs_s["dv"].append(kv_s(vdf_s, DIFF_HEADS, 2 * DIFF_DH))

        row = lambda a: a[layer].reshape(1, D_MODEL).astype(F32)
        h = _outproj_ln(ya, yb, h, w_out.astype(BF16), row(ln1_g), row(ln1_b), alpha)
        h = _moe(h, router_w[layer].T.astype(F32), router_bias[layer].reshape(N_EXPERTS, 1).astype(F32),
                 exp_w1[layer].astype(BF16), exp_w3[layer].astype(BF16), exp_w2[layer].astype(BF16),
                 sh_w1[layer].astype(BF16), sh_w3[layer].astype(BF16), sh_w2[layer].astype(BF16),
                 row(ln2_g), row(ln2_b), alpha)

    y_prompt = h[:n_p].reshape(bsz, lp, D_MODEL)[:, PAD + N_META:]
    y_sample = h[n_p:].reshape(bsz_s, t_s, D_MODEL)
    order = ("sbk", "sbv", "dk", "dv", "s5r", "s5i", "ret")
    return ((y_prompt, y_sample) + tuple(jnp.stack(outs_p[k]) for k in order)
            + tuple(jnp.stack(outs_s[k]) for k in order))
```

```python
import functools
import math

import jax
import jax.numpy as jnp
from jax import lax
from jax.experimental import pallas as pl
from jax.experimental.pallas import tpu as pltpu

F32 = jnp.float32
BF16 = jnp.bfloat16
I32 = jnp.int32
HIGHEST = lax.Precision.HIGHEST

D_MODEL = 1024
W_HALF = 512
N_META = 16
Q_BLOCK = 128
PAD = Q_BLOCK - N_META
CHUNK = 64
CHUNK_SHIFT = 6
S5_GROUPS = 32
S5_GROUP_CH = 16
S5_STATE = 64
S5_WIDTH = S5_GROUPS * S5_STATE
S5_BLOCKS = 4
RET_HEADS = 8
RET_DK = 64
SB_HEADS = 8
SB_DH = 64
DIFF_HEADS = 4
DIFF_DH = 64
N_EXPERTS = 64
N_GROUPS = 8
GROUP_SIZE = N_EXPERTS // N_GROUPS
TOPK_GROUPS = 4
TOP_K = 6
ROUTED_SCALE = 2.5
EXPERT_BLOCK = 512
EPS = 1e-5
NEG_BIG = -1e30
KEY_BLOCK = 128
KEY_SHIFT = 7
SUBLANES = 8
LANES = 128
VMEM_LIMIT = 56 * 1024 * 1024

NT_DIMS = (((1,), (1,)), ((), ()))


def _params(*sem):
    return pltpu.CompilerParams(dimension_semantics=sem, vmem_limit_bytes=VMEM_LIMIT)


def _tile(n, cands=(512, 256, 128, 64)):
    for c in cands:
        if n % c == 0:
            return c
    raise ValueError(f"no tile for {n}")


def _layer_norm(r, g, b):
    mu = jnp.mean(r, -1, keepdims=True)
    d = r - mu
    var = jnp.mean(d * d, -1, keepdims=True)
    return d * lax.rsqrt(var + EPS) * g + b


def _proj_kernel(x_ref, w_ref, *o_refs, cols, precise):
    x = x_ref[...]
    xb = x.astype(F32) if precise else x.astype(BF16)
    for o_ref, c0 in zip(o_refs, cols):
        wd = o_ref.shape[-1]
        w = w_ref[:, c0:c0 + wd]
        if precise:
            y = jnp.dot(xb, w, precision=HIGHEST, preferred_element_type=F32)
        else:
            y = jnp.dot(xb, w, preferred_element_type=F32)
        o_ref[...] = y.astype(o_ref.dtype)


def _project(x, w, outs, row0, n, precise=False):
    k = x.shape[1]
    tm = _tile(math.gcd(n, row0) if row0 else n)
    blk0 = row0 // tm
    cols = tuple(o[0] for o in outs)
    return pl.pallas_call(
        functools.partial(_proj_kernel, cols=cols, precise=precise),
        grid=(n // tm,),
        in_specs=[pl.BlockSpec((tm, k), lambda i: (i + blk0, 0)),
                  pl.BlockSpec(w.shape, lambda i: (0, 0))],
        out_specs=[pl.BlockSpec((tm, o[1]), lambda i: (i, 0)) for o in outs],
        out_shape=[jax.ShapeDtypeStruct((n, o[1]), o[2]) for o in outs],
        compiler_params=_params("parallel"),
        name="in_proj",
    )(x, w)


def _s5_kernel(u_ref, x0r_ref, x0i_ref, lr_ref, li_ref, bre_ref, bim_ref, cre_ref, cim_ref, d_ref,
               wglu_ref, y_ref, fr_ref, fi_ref, xr_s, xi_s, sr_s, si_s, *, tc, precise):
    c = pl.program_id(1)

    @pl.when(c == 0)
    def _():
        xr_s[...] = x0r_ref[0]
        xi_s[...] = x0i_ref[0]

    def mm(a, b):
        if precise:
            return jnp.dot(a, b, precision=HIGHEST, preferred_element_type=F32)
        return jnp.dot(a.astype(BF16), b, preferred_element_type=F32)

    u = u_ref[...]
    cw, sw = W_HALF // S5_BLOCKS, S5_WIDTH // S5_BLOCKS
    for b in range(S5_BLOCKS):
        ub = u[:, b * cw:(b + 1) * cw]
        sr_s[:, b * sw:(b + 1) * sw] = mm(ub, bre_ref[b])
        si_s[:, b * sw:(b + 1) * sw] = mm(ub, bim_ref[b])
    lr = lr_ref[...]
    li = li_ref[...]

    def step(t, carry):
        xr, xi = carry
        nr = lr * xr - li * xi + sr_s[pl.ds(t, 1), :]
        ni = lr * xi + li * xr + si_s[pl.ds(t, 1), :]
        sr_s[pl.ds(t, 1), :] = nr
        si_s[pl.ds(t, 1), :] = ni
        return nr, ni

    xr, xi = lax.fori_loop(0, tc, step, (xr_s[...], xi_s[...]), unroll=8)
    xr_s[...] = xr
    xi_s[...] = xi
    y = jnp.concatenate(
        [mm(sr_s[:, b * sw:(b + 1) * sw], cre_ref[b]) - mm(si_s[:, b * sw:(b + 1) * sw], cim_ref[b])
         for b in range(S5_BLOCKS)], axis=1) + d_ref[...] * u
    y = jax.nn.gelu(y)
    y = y * jax.nn.sigmoid(mm(y, wglu_ref[...]))
    y_ref[...] = y.astype(y_ref.dtype)

    @pl.when(c == pl.num_programs(1) - 1)
    def _():
        fr_ref[0] = xr
        fi_ref[0] = xi


def _s5_mixer(u, x0r, x0i, p, tc, precise):
    b, l, _ = u.shape
    nc = l // tc
    wdt = F32 if precise else BF16
    full = lambda a: pl.BlockSpec(a.shape, lambda i, j: (0,) * a.ndim)
    st_spec = pl.BlockSpec((1, 1, S5_WIDTH), lambda i, j: (i, 0, 0))
    mats = [p["bre"].astype(wdt), p["bim"].astype(wdt), p["cre"].astype(wdt), p["cim"].astype(wdt)]
    wglu = p["wglu"].astype(wdt)
    return pl.pallas_call(
        functools.partial(_s5_kernel, tc=tc, precise=precise),
        grid=(b, nc),
        in_specs=[pl.BlockSpec((None, tc, W_HALF), lambda i, j: (i, j, 0)), st_spec, st_spec,
                  full(p["lr"]), full(p["li"])] + [full(m) for m in mats] + [full(p["d"]), full(wglu)],
        out_specs=[pl.BlockSpec((None, tc, W_HALF), lambda i, j: (i, j, 0)), st_spec, st_spec],
        out_shape=[jax.ShapeDtypeStruct((b, l, W_HALF), BF16),
                   jax.ShapeDtypeStruct((b, 1, S5_WIDTH), F32),
                   jax.ShapeDtypeStruct((b, 1, S5_WIDTH), F32)],
        scratch_shapes=[pltpu.VMEM((1, S5_WIDTH), F32), pltpu.VMEM((1, S5_WIDTH), F32),
                        pltpu.VMEM((tc, S5_WIDTH), F32), pltpu.VMEM((tc, S5_WIDTH), F32)],
        compiler_params=_params("parallel", "arbitrary"),
        name="s5_mixer",
    )(u, x0r, x0i, p["lr"], p["li"], *mats, p["d"], wglu)


def _s5_prepare(lam_re, lam_im, log_dt, b_re, b_im, c_re, c_im, d, w_glu):
    dt = jnp.exp(log_dt)[:, None]
    mag = jnp.exp(lam_re * dt)
    lb_re = mag * jnp.cos(lam_im * dt)
    lb_im = mag * jnp.sin(lam_im * dt)
    nr, ni = lb_re - 1.0, lb_im
    den = jnp.square(lam_re) + jnp.square(lam_im)
    cr = (nr * lam_re + ni * lam_im) / den
    ci = (ni * lam_re - nr * lam_im) / den
    bb_re = cr[..., None] * b_re - ci[..., None] * b_im
    bb_im = cr[..., None] * b_im + ci[..., None] * b_re
    gb = S5_GROUPS // S5_BLOCKS
    eye = jnp.eye(gb, dtype=F32)
    per_blk = lambda m: m.reshape((S5_BLOCKS, gb) + m.shape[1:])
    blk_in = lambda m: jnp.einsum("bgpi,gh->bgihp", per_blk(m), eye).reshape(
        S5_BLOCKS, W_HALF // S5_BLOCKS, S5_WIDTH // S5_BLOCKS)
    blk_out = lambda m: jnp.einsum("bgop,gh->bgpho", per_blk(m), eye).reshape(
        S5_BLOCKS, S5_WIDTH // S5_BLOCKS, W_HALF // S5_BLOCKS)
    return dict(lr=lb_re.reshape(1, S5_WIDTH), li=lb_im.reshape(1, S5_WIDTH),
                bre=blk_in(bb_re), bim=blk_in(bb_im), cre=blk_out(c_re), cim=blk_out(c_im),
                d=d.reshape(1, W_HALF), wglu=w_glu)


def _ret_kernel(q_ref, k_ref, v_ref, g_ref, cos_ref, sin_ref, dmat_ref, inter_ref, zeta_ref, gc_ref,
                r0_ref, y_ref, rf_ref, r_s, *, tc):
    c = pl.program_id(1)

    @pl.when(c == 0)
    def _():
        r_s[...] = r0_ref[0]

    lane = lax.broadcasted_iota(I32, (tc, W_HALF), 1)
    first_half = (lane & (RET_DK - 1)) < (RET_DK // 2)
    cos = cos_ref[...]
    sin = sin_ref[...]

    def rot(x):
        ahead = pltpu.roll(x, W_HALF - RET_DK // 2, 1)
        behind = pltpu.roll(x, RET_DK // 2, 1)
        return x * cos + jnp.where(first_half, ahead, behind) * sin

    q = rot(q_ref[...])
    k = rot(k_ref[...]) * (RET_DK ** -0.5)
    v = v_ref[...]
    vz = v * zeta_ref[...]
    g = g_ref[...]
    gate = g * jax.nn.sigmoid(g)
    heads = [slice(h * RET_DK, (h + 1) * RET_DK) for h in range(RET_HEADS)]
    qb = q.astype(BF16)
    kb = k.astype(BF16)
    vb = v.astype(BF16)
    vzb = vz.astype(BF16)
    kt = k.T.astype(BF16)
    scores = [lax.dot_general(qb[:, sl], kb[:, sl], NT_DIMS, preferred_element_type=F32) for sl in heads]
    cross = [jnp.dot(qb[:, sl], r_s[h].astype(BF16), preferred_element_type=F32)
             for h, sl in enumerate(heads)]
    new_r = [gc_ref[h] * r_s[h] + jnp.dot(kt[sl, :], vzb[:, sl], preferred_element_type=F32)
             for h, sl in enumerate(heads)]
    decayed = [(scores[h] * dmat_ref[h]).astype(BF16) for h in range(RET_HEADS)]
    outs = [jnp.dot(decayed[h], vb[:, sl], preferred_element_type=F32) + cross[h] * inter_ref[:, sl]
            for h, sl in enumerate(heads)]
    for h, sl in enumerate(heads):
        r_s[h] = new_r[h]
        o = outs[h]
        mu = jnp.mean(o, -1, keepdims=True)
        dlt = o - mu
        var = jnp.mean(dlt * dlt, -1, keepdims=True)
        y_ref[:, sl] = (dlt * lax.rsqrt(var + EPS) * gate[:, sl]).astype(y_ref.dtype)

    @pl.when(c == pl.num_programs(1) - 1)
    def _():
        rf_ref[0] = r_s[...]


def _ret_mixer(q, k, v, g, r0, pos, tc):
    b, l, _ = q.shape
    nc = l // tc
    inv = 1.0 / (10000.0 ** jnp.linspace(0.0, 1.0, RET_DK // 2, dtype=F32))
    ang = pos.astype(F32)[:, None] * inv[None]
    cos = jnp.tile(jnp.concatenate([jnp.cos(ang), jnp.cos(ang)], 1), (1, RET_HEADS))
    sin = jnp.tile(jnp.concatenate([-jnp.sin(ang), jnp.sin(ang)], 1), (1, RET_HEADS))
    log_gamma = jnp.log(1.0 - 2.0 ** (-5.0 - jnp.arange(RET_HEADS, dtype=F32)))
    n = jnp.arange(tc, dtype=F32)
    diff = n[:, None] - n[None, :]
    causal = diff >= 0
    dmat = jnp.where(causal[None], jnp.exp(jnp.where(causal, diff, 0.0)[None] * log_gamma[:, None, None]), 0.0)
    per_lane = lambda m: jnp.repeat(m, RET_DK, axis=1)
    inter = per_lane(jnp.exp((n[:, None] + 1.0) * log_gamma[None]))
    zeta = per_lane(jnp.exp((tc - 1.0 - n)[:, None] * log_gamma[None]))
    gc = jnp.broadcast_to(jnp.exp(tc * log_gamma)[:, None, None], (RET_HEADS, RET_DK, RET_DK))
    seq = pl.BlockSpec((None, tc, W_HALF), lambda i, j: (i, j, 0))
    tab = pl.BlockSpec((tc, W_HALF), lambda i, j: (j, 0))
    full = lambda a: pl.BlockSpec(a.shape, lambda i, j: (0,) * a.ndim)
    st = pl.BlockSpec((1, RET_HEADS, RET_DK, RET_DK), lambda i, j: (i, 0, 0, 0))
    return pl.pallas_call(
        functools.partial(_ret_kernel, tc=tc),
        grid=(b, nc),
        in_specs=[seq, seq, seq, seq, tab, tab, full(dmat), full(inter), full(zeta), full(gc), st],
        out_specs=[seq, st],
        out_shape=[jax.ShapeDtypeStruct((b, l, W_HALF), BF16),
                   jax.ShapeDtypeStruct((b, RET_HEADS, RET_DK, RET_DK), F32)],
        scratch_shapes=[pltpu.VMEM((RET_HEADS, RET_DK, RET_DK), F32)],
        compiler_params=_params("parallel", "arbitrary"),
        name="ret_mixer",
    )(q, k, v, g, cos, sin, dmat, inter, zeta, gc, r0)


def _attn_kernel(lq1_ref, lk1_ref, lq2_ref, lk2_ref, ng_ref, qs_ref, qd_ref, ks_ref, vst_ref, kd_ref,
                 vdt_ref, osb_ref, od_ref, *, bq, q_base, kv_lo, n_key_blocks, lam_init):
    qi = pl.program_id(1)
    q0 = q_base + qi * bq
    nkb = jnp.minimum(jnp.right_shift(q0 + bq + KEY_BLOCK - 1, KEY_SHIFT), n_key_blocks)
    qpos = q0 + lax.broadcasted_iota(I32, (KEY_BLOCK, bq), 1)
    koff = lax.broadcasted_iota(I32, (KEY_BLOCK, bq), 0)
    qchunk = jnp.right_shift(qpos, CHUNK_SHIFT)
    row = lax.broadcasted_iota(I32, (KEY_BLOCK, KEY_BLOCK), 0)
    col = lax.broadcasted_iota(I32, (KEY_BLOCK, KEY_BLOCK), 1)
    newer_keys = (col > row).astype(BF16)

    qs = (qs_ref[...].astype(F32) * (SB_DH ** -0.5)).astype(BF16)
    qd = (qd_ref[...].astype(F32) * (DIFF_DH ** -0.5)).astype(BF16)
    sb_q = [qs[:, h * SB_DH:(h + 1) * SB_DH] for h in range(SB_HEADS)]
    d_q = [qd[:, c * DIFF_DH:(c + 1) * DIFF_DH] for c in range(2 * DIFF_HEADS)]

    def body(t, carry, edge, nblk, j0):
        sb_acc, sb_tail, d_m, d_l, d_a = carry
        blocks = range(nblk)
        js = [j0 + nblk * t + i for i in blocks]
        kbs = [nkb - 1 - j for j in js]
        kblks = [ks_ref[kb] for kb in kbs]
        dblks = [kd_ref[j] for j in js]
        if edge:
            masks, dmasks = [], []
            for i in blocks:
                kpos = kbs[i] * KEY_BLOCK + koff
                masks.append((kpos >= kv_lo) & (kpos < qpos))
                dpos = js[i] * KEY_BLOCK + koff
                dmasks.append((dpos >= kv_lo) & (jnp.right_shift(dpos, CHUNK_SHIFT) <= qchunk))
            keep = lambda m, x, other: jnp.where(m, x, other)
        else:
            masks = dmasks = [None] * nblk
            keep = lambda m, x, other: x
        zs = [[lax.dot_general(kblks[i][:, h * SB_DH:(h + 1) * SB_DH], sb_q[h], NT_DIMS,
                               preferred_element_type=F32) for h in range(SB_HEADS)] for i in blocks]
        ss = [[lax.dot_general(dblks[i][:, c * DIFF_DH:(c + 1) * DIFF_DH], d_q[c], NT_DIMS,
                               preferred_element_type=F32) for c in range(2 * DIFF_HEADS)] for i in blocks]
        log_betas = [[None] * SB_HEADS for _ in blocks]
        splits = [[None] * SB_HEADS for _ in blocks]
        tails = [[None] * SB_HEADS for _ in blocks]
        new_tail = []
        for h in range(SB_HEADS):
            tail = sb_tail[h]
            for i in blocks:
                z = zs[i][h]
                sp = jnp.log(1.0 + jnp.exp(-jnp.abs(z)))
                log_1mb = keep(masks[i], -(jnp.maximum(z, 0.0) + sp), 0.0)
                log_betas[i][h] = jnp.minimum(z, 0.0) - sp
                hi = log_1mb.astype(BF16)
                lo = (log_1mb - hi.astype(F32)).astype(BF16)
                splits[i][h] = jnp.concatenate([hi, lo], axis=1)
                tails[i][h] = tail
                tail = tail + jnp.sum(log_1mb, axis=0, keepdims=True)
            new_tail.append(tail)
        new_m, new_l, alphas = [], [], []
        probs = [[None] * (2 * DIFF_HEADS) for _ in blocks]
        for c in range(2 * DIFF_HEADS):
            s = [keep(dmasks[i], ss[i][c], NEG_BIG) for i in blocks]
            mn = d_m[c]
            for i in blocks:
                mn = jnp.maximum(mn, jnp.max(s[i], axis=0, keepdims=True))
            al = jnp.exp(d_m[c] - mn)
            l = al * d_l[c]
            for i in blocks:
                p = jnp.exp(s[i] - mn)
                l = l + jnp.sum(p, axis=0, keepdims=True)
                probs[i][c] = p.astype(BF16)
            new_m.append(mn)
            new_l.append(l)
            alphas.append(al)
        newer = [[jnp.dot(newer_keys, splits[i][h], preferred_element_type=F32)
                  for h in range(SB_HEADS)] for i in blocks]
        new_a = []
        for c in range(2 * DIFF_HEADS):
            a = alphas[c] * d_a[c]
            vsl = slice((c // 2) * 2 * DIFF_DH, (c // 2 + 1) * 2 * DIFF_DH)
            for i in blocks:
                a = a + jnp.dot(vdt_ref[js[i], vsl, :], probs[i][c], preferred_element_type=F32)
            new_a.append(a)
        ws = [[keep(masks[i], jnp.exp(log_betas[i][h] + newer[i][h][:, :bq] + newer[i][h][:, bq:]
                                      + tails[i][h]), 0.0).astype(BF16)
               for h in range(SB_HEADS)] for i in blocks]
        new_acc = []
        for h in range(SB_HEADS):
            acc = sb_acc[h]
            for i in blocks:
                acc = acc + jnp.dot(vst_ref[kbs[i], h * SB_DH:(h + 1) * SB_DH, :], ws[i][h],
                                    preferred_element_type=F32)
            new_acc.append(acc)
        return tuple(new_acc), tuple(new_tail), tuple(new_m), tuple(new_l), tuple(new_a)

    zrow = jnp.zeros((1, bq), F32)
    init = (tuple(jnp.zeros((SB_DH, bq), F32) for _ in range(SB_HEADS)),
            tuple(zrow for _ in range(SB_HEADS)),
            tuple(jnp.full((1, bq), NEG_BIG, F32) for _ in range(2 * DIFF_HEADS)),
            tuple(zrow for _ in range(2 * DIFF_HEADS)),
            tuple(jnp.zeros((2 * DIFF_DH, bq), F32) for _ in range(2 * DIFF_HEADS)))
    edge_body = functools.partial(body, edge=True, nblk=1, j0=0)
    n_pairs = jnp.right_shift(jnp.maximum(nkb - 2, 0), 1)
    carry = lax.fori_loop(0, jnp.minimum(nkb, 1), edge_body, init)
    carry = lax.fori_loop(0, n_pairs, functools.partial(body, edge=False, nblk=2, j0=1), carry)
    carry = lax.fori_loop(1 + 2 * n_pairs, nkb - 1, functools.partial(body, edge=False, nblk=1, j0=0), carry)
    sb_acc, _, _, d_l, d_a = lax.fori_loop(jnp.maximum(nkb - 1, 1), nkb, edge_body, carry)

    osb_ref[...] = jnp.concatenate(sb_acc, axis=0).T.astype(osb_ref.dtype)
    lam = (jnp.exp(jnp.sum(lq1_ref[...] * lk1_ref[...], axis=1, keepdims=True))
           - jnp.exp(jnp.sum(lq2_ref[...] * lk2_ref[...], axis=1, keepdims=True)) + lam_init)
    outs = []
    for h in range(DIFF_HEADS):
        o = d_a[2 * h] / d_l[2 * h] - lam * (d_a[2 * h + 1] / d_l[2 * h + 1])
        o = o * lax.rsqrt(jnp.mean(o * o, axis=0, keepdims=True) + EPS) * ng_ref[...]
        outs.append(o * (1.0 - lam_init))
    od_ref[...] = jnp.concatenate(outs, axis=0).T.astype(od_ref.dtype)


def _attention(qs, qd, ks, vs, kd, vd, lam_vecs, norm_g, *, bq, q_base, kv_lo, lam_init):
    b, lq, _ = qs.shape
    lk = ks.shape[1]
    nkb = lk // KEY_BLOCK
    assert bq == KEY_BLOCK and q_base % KEY_BLOCK == 0 and 0 <= kv_lo < KEY_BLOCK
    kblocks = lambda a: a.reshape(b, nkb, KEY_BLOCK, W_HALF)
    vblocks_t = lambda a: a.reshape(b, nkb, KEY_BLOCK, W_HALF).transpose(0, 1, 3, 2)
    qspec = pl.BlockSpec((None, bq, W_HALF), lambda i, j: (i, j, 0))
    kspec = pl.BlockSpec((None, nkb, KEY_BLOCK, W_HALF), lambda i, j: (i, 0, 0, 0))
    vspec = pl.BlockSpec((None, nkb, W_HALF, KEY_BLOCK), lambda i, j: (i, 0, 0, 0))
    vec = pl.BlockSpec((1, DIFF_DH), lambda i, j: (0, 0))
    return pl.pallas_call(
        functools.partial(_attn_kernel, bq=bq, q_base=q_base, kv_lo=kv_lo,
                          n_key_blocks=nkb, lam_init=lam_init),
        grid=(b, lq // bq),
        in_specs=[vec, vec, vec, vec, pl.BlockSpec((2 * DIFF_DH, 1), lambda i, j: (0, 0)),
                  qspec, qspec, kspec, vspec, kspec, vspec],
        out_specs=[qspec, qspec],
        out_shape=[jax.ShapeDtypeStruct((b, lq, W_HALF), BF16), jax.ShapeDtypeStruct((b, lq, W_HALF), BF16)],
        compiler_params=_params("parallel", "arbitrary"),
        name="sb_diff_attention",
    )(*lam_vecs, norm_g, qs, qd, kblocks(ks), vblocks_t(vs), kblocks(kd), vblocks_t(vd))


def _outproj_ln_kernel(ya_ref, yb_ref, h_ref, w_ref, g_ref, b_ref, o_ref, *, alpha):
    m = (jnp.dot(ya_ref[...].astype(BF16), w_ref[:W_HALF, :], preferred_element_type=F32)
         + jnp.dot(yb_ref[...].astype(BF16), w_ref[W_HALF:, :], preferred_element_type=F32))
    o_ref[...] = _layer_norm(alpha * h_ref[...] + m, g_ref[...], b_ref[...])


def _outproj_ln(ya, yb, h, w, g, b, alpha):
    n = h.shape[0]
    tm = _tile(n)
    row = lambda wd: pl.BlockSpec((tm, wd), lambda i: (i, 0))
    full = lambda a: pl.BlockSpec(a.shape, lambda i: (0, 0))
    return pl.pallas_call(
        functools.partial(_outproj_ln_kernel, alpha=alpha),
        grid=(n // tm,),
        in_specs=[row(W_HALF), row(W_HALF), row(D_MODEL), full(w), full(g), full(b)],
        out_specs=row(D_MODEL),
        out_shape=jax.ShapeDtypeStruct((n, D_MODEL), F32),
        compiler_params=_params("parallel"),
        name="out_proj_ln",
    )(ya, yb, h, w, g, b)


def _router_kernel(x_ref, wt_ref, bias_ref, eidx_ref, gate_ref, rank_ref, cnt_ref, run_s, *, tm):
    i = pl.program_id(0)

    @pl.when(i == 0)
    def _():
        run_s[...] = jnp.zeros_like(run_s)

    logits = lax.dot_general(wt_ref[...], x_ref[...], NT_DIMS, precision=HIGHEST,
                             preferred_element_type=F32)
    scores = jax.nn.sigmoid(logits)
    sel = scores + bias_ref[...]
    ninf = -jnp.inf

    sel3 = sel.reshape(N_GROUPS, GROUP_SIZE, tm)
    within = lax.broadcasted_iota(I32, (N_GROUPS, GROUP_SIZE, tm), 1)
    m1 = jnp.max(sel3, axis=1, keepdims=True)
    first = jnp.min(jnp.where(sel3 == m1, within, GROUP_SIZE), axis=1, keepdims=True)
    m2 = jnp.max(jnp.where(within == first, ninf, sel3), axis=1, keepdims=True)
    gscore = (m1 + m2).reshape(N_GROUPS, tm)

    gid = lax.broadcasted_iota(I32, (N_GROUPS, tm), 0)
    gsel = jnp.zeros((N_GROUPS, tm), jnp.bool_)
    cur = gscore
    for _ in range(TOPK_GROUPS):
        m = jnp.max(cur, axis=0, keepdims=True)
        f = jnp.min(jnp.where(cur == m, gid, N_GROUPS), axis=0, keepdims=True)
        pick = gid == f
        gsel = gsel | pick
        cur = jnp.where(pick, ninf, cur)
    emask = jnp.broadcast_to(gsel.reshape(N_GROUPS, 1, tm), (N_GROUPS, GROUP_SIZE, tm)).reshape(N_EXPERTS, tm)

    eid = lax.broadcasted_iota(I32, (N_EXPERTS, tm), 0)
    cur = jnp.where(emask, sel, ninf)
    picks, idxs, gates = [], [], []
    for _ in range(TOP_K):
        m = jnp.max(cur, axis=0, keepdims=True)
        f = jnp.min(jnp.where(cur == m, eid, N_EXPERTS), axis=0, keepdims=True)
        pick = eid == f
        picks.append(pick)
        idxs.append(f)
        gates.append(jnp.sum(jnp.where(pick, scores, 0.0), axis=0, keepdims=True))
        cur = jnp.where(pick, ninf, cur)
    gsum = gates[0]
    for gk in gates[1:]:
        gsum = gsum + gk

    chosen = picks[0]
    for pk in picks[1:]:
        chosen = chosen | pk
    chosen_f = chosen.astype(F32)
    r_ = lax.broadcasted_iota(I32, (tm, tm), 0)
    c_ = lax.broadcasted_iota(I32, (tm, tm), 1)
    earlier = (r_ < c_).astype(BF16)
    before = run_s[...] + jnp.dot(chosen_f.astype(BF16), earlier, preferred_element_type=F32)
    zrow_i = jnp.zeros((1, tm), I32)
    zrow_f = jnp.zeros((1, tm), F32)
    ranks = [jnp.sum(jnp.where(pk, before, 0.0), axis=0, keepdims=True).astype(I32) for pk in picks]
    eidx_ref[...] = jnp.concatenate(idxs + [zrow_i, zrow_i], axis=0)
    gate_ref[...] = jnp.concatenate([gk / gsum * ROUTED_SCALE for gk in gates] + [zrow_f, zrow_f], axis=0)
    rank_ref[...] = jnp.concatenate(ranks + [zrow_i, zrow_i], axis=0)
    run_s[...] = run_s[...] + jnp.sum(chosen_f, axis=1, keepdims=True)
    cnt_ref[...] = jnp.broadcast_to(run_s[...], cnt_ref.shape)


def _router(x, wt, bias):
    n = x.shape[0]
    tm = _tile(n, (512, 256, 128))
    lane_blk = pl.BlockSpec((8, tm), lambda i: (0, i))
    return pl.pallas_call(
        functools.partial(_router_kernel, tm=tm),
        grid=(n // tm,),
        in_specs=[pl.BlockSpec((tm, D_MODEL), lambda i: (i, 0)),
                  pl.BlockSpec(wt.shape, lambda i: (0, 0)),
                  pl.BlockSpec(bias.shape, lambda i: (0, 0))],
        out_specs=[lane_blk, lane_blk, lane_blk, pl.BlockSpec((N_EXPERTS, 128), lambda i: (0, 0))],
        out_shape=[jax.ShapeDtypeStruct((8, n), I32), jax.ShapeDtypeStruct((8, n), F32),
                   jax.ShapeDtypeStruct((8, n), I32), jax.ShapeDtypeStruct((N_EXPERTS, 128), F32)],
        scratch_shapes=[pltpu.VMEM((N_EXPERTS, 1), F32)],
        compiler_params=_params("arbitrary"),
        name="moe_router",
    )(x, wt, bias)


ROW_SUB = D_MODEL // 2 // LANES
HIGH_HALF = -65536
DMA_PRIORITIES = 2


def _pack_rows(dst_ref, x, r0=0):
    rows = x.shape[0]
    bits = lambda v: lax.bitcast_convert_type(v.astype(BF16).astype(F32), I32)
    w = (lax.shift_right_logical(bits(x[:, :D_MODEL // 2]), jnp.full((), 16, I32))
         | (bits(x[:, D_MODEL // 2:]) & HIGH_HALF))
    for c in range(ROW_SUB):
        dst_ref[pl.ds(r0 * ROW_SUB + c, rows, stride=ROW_SUB), :] = w[:, c * LANES:(c + 1) * LANES]


def _unpack_rows(src_ref, rows, r0=0):
    ws = [src_ref[pl.ds(r0 * ROW_SUB + c, rows, stride=ROW_SUB), :] for c in range(ROW_SUB)]
    lo = [lax.bitcast_convert_type(lax.shift_left(w, jnp.full((), 16, I32)), F32) for w in ws]
    hi = [lax.bitcast_convert_type(w & HIGH_HALF, F32) for w in ws]
    return jnp.concatenate(lo + hi, axis=1)


def _dispatch_kernel(*refs, tm):
    n_idx = TOP_K * tm // LANES
    dest_refs, (x_ref, o_hbm, buf, sem) = refs[:n_idx], refs[n_idx:]
    i = pl.program_id(0)
    slot = i % 2
    stage = buf.at[slot]
    _pack_rows(stage, x_ref[...])

    def wait_all():
        for _ in range(TOP_K):
            pltpu.make_async_copy(stage, o_hbm.at[pl.ds(0, tm * ROW_SUB)], sem).wait()

    @pl.when(i > 0)
    def _():
        wait_all()

    for k in range(TOP_K):
        for blk in range(tm // LANES):
            def start(t, carry, k=k, blk=blk):
                for prio in range(DMA_PRIORITIES):
                    jj = t * DMA_PRIORITIES + prio
                    d = pl.multiple_of(dest_refs[k * (tm // LANES) + blk][0, 0, jj] * ROW_SUB, ROW_SUB)
                    s = pl.multiple_of((blk * LANES + jj) * ROW_SUB, ROW_SUB)
                    pltpu.make_async_copy(stage.at[pl.ds(s, ROW_SUB)], o_hbm.at[pl.ds(d, ROW_SUB)],
                                          sem).start(priority=prio)
                return carry
            lax.fori_loop(0, LANES // DMA_PRIORITIES, start, 0, unroll=4)

    @pl.when(i == pl.num_programs(0) - 1)
    def _():
        wait_all()


def _slot_spec():
    return pl.BlockSpec((1, 1, LANES), lambda i: (i, 0, 0), memory_space=pltpu.SMEM)


def _dispatch(x, dests, n_rows, tm):
    n = x.shape[0]
    return pl.pallas_call(
        functools.partial(_dispatch_kernel, tm=tm),
        grid=(n // tm,),
        in_specs=[_slot_spec() for _ in dests] + [pl.BlockSpec((tm, D_MODEL), lambda i: (i, 0))],
        out_specs=pl.BlockSpec(memory_space=pl.ANY),
        out_shape=jax.ShapeDtypeStruct((n_rows * ROW_SUB, LANES), I32),
        scratch_shapes=[pltpu.VMEM((2, tm * ROW_SUB, LANES), I32), pltpu.SemaphoreType.DMA(())],
        compiler_params=_params("arbitrary"),
        name="moe_dispatch",
    )(*dests, x)


def _ffn_kernel(be_ref, nv_ref, xs_ref, w1_ref, w3_ref, w2_ref, o_ref):
    i = pl.program_id(0)
    x = _unpack_rows(xs_ref, EXPERT_BLOCK)
    live = lax.broadcasted_iota(I32, (EXPERT_BLOCK, 1), 0) < nv_ref[i]
    x = jnp.where(live, x, 0.0).astype(BF16)
    a = jnp.dot(x, w1_ref[0], preferred_element_type=F32)
    b = jnp.dot(x, w3_ref[0], preferred_element_type=F32)
    hdn = (a * jax.nn.sigmoid(a) * b).astype(BF16)
    _pack_rows(o_ref, jnp.dot(hdn, w2_ref[0], preferred_element_type=F32))


def _expert_ffn(xs, blk_e, blk_valid, w1, w3, w2):
    nb = xs.shape[0] // (EXPERT_BLOCK * ROW_SUB)
    de = w1.shape[-1]
    rows = pl.BlockSpec((EXPERT_BLOCK * ROW_SUB, LANES), lambda i, be, nv: (i, 0))
    grid_spec = pltpu.PrefetchScalarGridSpec(
        num_scalar_prefetch=2,
        grid=(nb,),
        in_specs=[rows,
                  pl.BlockSpec((1, D_MODEL, de), lambda i, be, nv: (be[i], 0, 0)),
                  pl.BlockSpec((1, D_MODEL, de), lambda i, be, nv: (be[i], 0, 0)),
                  pl.BlockSpec((1, de, D_MODEL), lambda i, be, nv: (be[i], 0, 0))],
        out_specs=rows,
    )
    return pl.pallas_call(
        _ffn_kernel,
        grid_spec=grid_spec,
        out_shape=jax.ShapeDtypeStruct(xs.shape, I32),
        compiler_params=_params("arbitrary"),
        name="moe_expert_ffn",
    )(blk_e, blk_valid, xs, w1, w3, w2)


def _combine_kernel(*refs, tm, alpha):
    n_idx = TOP_K * tm // LANES
    dest_refs = refs[:n_idx]
    gate_ref, x_ref, ys_hbm, sw1_ref, sw3_ref, sw2_ref, g_ref, b_ref, o_ref, buf, sem = refs[n_idx:]
    for k in range(TOP_K):
        for blk in range(tm // LANES):
            def start(t, carry, k=k, blk=blk):
                for prio in range(DMA_PRIORITIES):
                    jj = t * DMA_PRIORITIES + prio
                    s = pl.multiple_of(dest_refs[k * (tm // LANES) + blk][0, 0, jj] * ROW_SUB, ROW_SUB)
                    d = pl.multiple_of((k * tm + blk * LANES + jj) * ROW_SUB, ROW_SUB)
                    pltpu.make_async_copy(ys_hbm.at[pl.ds(s, ROW_SUB)], buf.at[pl.ds(d, ROW_SUB)],
                                          sem).start(priority=prio)
                return carry
            lax.fori_loop(0, LANES // DMA_PRIORITIES, start, 0, unroll=4)

    x = x_ref[...]
    xb = x.astype(BF16)
    a = jnp.dot(xb, sw1_ref[...], preferred_element_type=F32)
    b = jnp.dot(xb, sw3_ref[...], preferred_element_type=F32)
    shared = jnp.dot((a * jax.nn.sigmoid(a) * b).astype(BF16), sw2_ref[...], preferred_element_type=F32)

    for k in range(TOP_K):
        grp = pl.ds(k * tm * ROW_SUB, tm * ROW_SUB)
        pltpu.make_async_copy(ys_hbm.at[pl.ds(0, tm * ROW_SUB)], buf.at[grp], sem).wait()

    gate = gate_ref[...]
    routed = gate[:, 0:1] * _unpack_rows(buf, tm, 0)
    for k in range(1, TOP_K):
        routed = routed + gate[:, k:k + 1] * _unpack_rows(buf, tm, k * tm)
    o_ref[...] = _layer_norm(alpha * x + routed + shared, g_ref[...], b_ref[...])


def _combine(dests, gate_t, x, ys, sw1, sw3, sw2, g, b, alpha, tm):
    n = x.shape[0]
    full = lambda a: pl.BlockSpec(a.shape, lambda i: (0,) * a.ndim)
    return pl.pallas_call(
        functools.partial(_combine_kernel, tm=tm, alpha=alpha),
        grid=(n // tm,),
        in_specs=[_slot_spec() for _ in dests] + [
                  pl.BlockSpec((tm, 8), lambda i: (i, 0)),
                  pl.BlockSpec((tm, D_MODEL), lambda i: (i, 0)),
                  pl.BlockSpec(memory_space=pl.ANY),
                  full(sw1), full(sw3), full(sw2), full(g), full(b)],
        out_specs=pl.BlockSpec((tm, D_MODEL), lambda i: (i, 0)),
        out_shape=jax.ShapeDtypeStruct((n, D_MODEL), F32),
        scratch_shapes=[pltpu.VMEM((TOP_K * tm * ROW_SUB, LANES), I32), pltpu.SemaphoreType.DMA(())],
        compiler_params=_params("arbitrary"),
        name="moe_combine_ln",
    )(*dests, gate_t, x, ys, sw1, sw3, sw2, g, b)


def _moe(x, rw_t, r_bias, w1, w3, w2, sw1, sw3, sw2, g, b, alpha):
    n = x.shape[0]
    eidx, gate, rank, cnt = _router(x, rw_t, r_bias)
    counts = cnt[:, 0].astype(I32)
    padded = (counts + EXPERT_BLOCK - 1) // EXPERT_BLOCK * EXPERT_BLOCK
    pad_end = jnp.cumsum(padded)
    pad_start = pad_end - padded
    onehot = eidx[:, :, None] == jnp.arange(N_EXPERTS, dtype=I32)[None, None, :]
    dest = jnp.sum(jnp.where(onehot, pad_start[None, None, :], 0), axis=-1) + rank
    n_rows = -(-(n * TOP_K + N_EXPERTS * (EXPERT_BLOCK - 1)) // EXPERT_BLOCK) * EXPERT_BLOCK
    nb = n_rows // EXPERT_BLOCK
    blk_lo = jnp.arange(nb, dtype=I32) * EXPERT_BLOCK
    blk_e = jnp.minimum(jnp.sum((pad_end[None, :] <= blk_lo[:, None]).astype(I32), axis=1), N_EXPERTS - 1)
    blk_valid = jnp.clip(jnp.take(pad_start + counts, blk_e) - blk_lo, 0, EXPERT_BLOCK).astype(I32)
    tm = _tile(n, (256, 128))
    runs = dest.reshape(8, n // tm, tm // LANES, 1, LANES)
    dests = [runs[k, :, r] for k in range(TOP_K) for r in range(tm // LANES)]
    xs = _dispatch(x, dests, n_rows, tm)
    ys = _expert_ffn(xs, blk_e, blk_valid, w1, w3, w2)
    return _combine(dests, gate.T, x, ys, sw1, sw3, sw2, g, b, alpha, tm)


def kernel(x_prompt, x_sample, cache_sb_k, cache_sb_v, cache_diff_k, cache_diff_v, state_s5_re, state_s5_im, state_ret, meta_tokens, w_in_e, s5_lam_re, s5_lam_im, s5_log_dt, s5_b_re, s5_b_im, s5_c_re, s5_c_im, s5_d, s5_w_glu, w_out_e, w_in_o, diff_lq1, diff_lk1, diff_lq2, diff_lk2, diff_norm_g, w_out_o, ln1_g, ln1_b, ln2_g, ln2_b, router_w, router_bias, exp_w1, exp_w3, exp_w2, sh_w1, sh_w3, sh_w2):
    bsz, seq, _ = x_prompt.shape
    bsz_s, t_s, _ = x_sample.shape
    past = cache_sb_k.shape[2]
    depth = ln1_g.shape[0]
    alpha = (2 * depth) ** 0.25
    lp = PAD + N_META + seq
    n_p = bsz * lp
    n_s = bsz_s * t_s
    lt = N_META + seq

    meta = jnp.broadcast_to(meta_tokens[None].astype(F32), (bsz, N_META, D_MODEL))
    hp = jnp.concatenate([jnp.zeros((bsz, PAD, D_MODEL), F32), meta, x_prompt], 1)
    h = jnp.concatenate([hp.reshape(n_p, D_MODEL), x_sample.reshape(n_s, D_MODEL)], 0)

    outs_p = {k: [] for k in ("sbk", "sbv", "dk", "dv", "s5r", "s5i", "ret")}
    outs_s = {k: [] for k in ("sbk", "sbv", "dk", "dv", "s5r", "s5i", "ret")}
    half = lambda c0, dt=F32: (c0 * W_HALF, W_HALF, dt)

    for layer in range(depth):
        i = layer // 2
        if layer % 2 == 0:
            w_in = w_in_e[i]
            splits = [half(c) for c in range(5)]
            u_p, q_p, k_p, v_p, g_p = _project(h, w_in.astype(BF16), splits, 0, n_p)
            u_s, q_s, k_s, v_s, g_s = _project(h, w_in, splits, n_p, n_s, precise=True)
            s5p = _s5_prepare(s5_lam_re[i], s5_lam_im[i], s5_log_dt[i], s5_b_re[i], s5_b_im[i],
                              s5_c_re[i], s5_c_im[i], s5_d[i], s5_w_glu[i])
            seq_p = lambda a: a.reshape(bsz, lp, W_HALF)
            seq_s = lambda a: a.reshape(bsz_s, t_s, W_HALF)
            z5 = jnp.zeros((bsz, 1, S5_WIDTH), F32)
            y5_p, fr_p, fi_p = _s5_mixer(seq_p(u_p), z5, z5, s5p, Q_BLOCK, False)
            y5_s, fr_s, fi_s = _s5_mixer(seq_s(u_s), state_s5_re[i].reshape(bsz_s, 1, S5_WIDTH),
                                         state_s5_im[i].reshape(bsz_s, 1, S5_WIDTH), s5p, t_s, True)
            yr_p, rt_p = _ret_mixer(seq_p(q_p), seq_p(k_p), seq_p(v_p), seq_p(g_p),
                                    jnp.zeros((bsz, RET_HEADS, RET_DK, RET_DK), F32),
                                    jnp.arange(lp) - PAD, Q_BLOCK)
            yr_s, rt_s = _ret_mixer(seq_s(q_s), seq_s(k_s), seq_s(v_s), seq_s(g_s),
                                    state_ret[i].astype(F32), past + jnp.arange(t_s), t_s)
            ya = jnp.concatenate([y5_p.reshape(n_p, W_HALF), y5_s.reshape(n_s, W_HALF)], 0)
            yb = jnp.concatenate([yr_p.reshape(n_p, W_HALF), yr_s.reshape(n_s, W_HALF)], 0)
            w_out = w_out_e[i]
            st = lambda a, nb_: a.reshape(nb_, S5_GROUPS, S5_STATE)
            outs_p["s5r"].append(st(fr_p, bsz)); outs_p["s5i"].append(st(fi_p, bsz)); outs_p["ret"].append(rt_p)
            outs_s["s5r"].append(st(fr_s, bsz_s)); outs_s["s5i"].append(st(fi_s, bsz_s)); outs_s["ret"].append(rt_s)
        else:
            splits = [half(0, BF16), half(1), half(2), half(3, BF16), half(4), half(5),
                      half(1, BF16), half(2, BF16), half(4, BF16), half(5, BF16)]
            w_in = w_in_o[i].astype(BF16)
            (qs, ksf, vsf, qd, kdf, vdf, ksb, vsb, kdb, vdb) = _project(h, w_in, splits, 0, n_p)
            (qs_s, ksf_s, vsf_s, qd_s, kdf_s, vdf_s, ksb_s, vsb_s, kdb_s, vdb_s) = _project(h, w_in, splits, n_p, n_s)
            lam_init = 0.8 - 0.6 * math.exp(-0.3 * layer)
            lam_vecs = [v[i].reshape(1, DIFF_DH).astype(F32) for v in (diff_lq1, diff_lk1, diff_lq2, diff_lk2)]
            norm_g = diff_norm_g[i].reshape(2 * DIFF_DH, 1).astype(F32)
            seq_p = lambda a: a.reshape(bsz, lp, W_HALF)
            seq_s = lambda a: a.reshape(bsz_s, t_s, W_HALF)
            osb_p, od_p = _attention(seq_p(qs), seq_p(qd), seq_p(ksb), seq_p(vsb), seq_p(kdb), seq_p(vdb),
                                     lam_vecs, norm_g, bq=Q_BLOCK, q_base=0, kv_lo=PAD, lam_init=lam_init)
            bq_s = -(-t_s // Q_BLOCK) * Q_BLOCK
            lk_s = -(-(past + bq_s) // KEY_BLOCK) * KEY_BLOCK
            qpad = lambda a: jnp.pad(seq_s(a), ((0, 0), (0, bq_s - t_s), (0, 0)))

            def keys(cache, new):
                c = cache.reshape(bsz_s, past, W_HALF).astype(BF16)
                z = jnp.zeros((bsz_s, lk_s - past - t_s, W_HALF), BF16)
                return jnp.concatenate([c, seq_s(new), z], 1)
            osb_s, od_s = _attention(qpad(qs_s), qpad(qd_s), keys(cache_sb_k[i], ksb_s), keys(cache_sb_v[i], vsb_s),
                                     keys(cache_diff_k[i], kdb_s), keys(cache_diff_v[i], vdb_s),
                                     lam_vecs, norm_g, bq=bq_s, q_base=past, kv_lo=0, lam_init=lam_init)
            ya = jnp.concatenate([osb_p.reshape(n_p, W_HALF), osb_s[:, :t_s].reshape(n_s, W_HALF)], 0)
            yb = jnp.concatenate([od_p.reshape(n_p, W_HALF), od_s[:, :t_s].reshape(n_s, W_HALF)], 0)
            w_out = w_out_o[i]
            kv_p = lambda a, hh, dd: a.reshape(bsz, lp, W_HALF)[:, PAD:].reshape(bsz, lt, hh, dd)
            kv_s = lambda a, hh, dd: a.reshape(bsz_s, t_s, hh, dd)
            outs_p["sbk"].append(kv_p(ksf, SB_HEADS, SB_DH)); outs_p["sbv"].append(kv_p(vsf, SB_HEADS, SB_DH))
            outs_p["dk"].append(kv_p(kdf, DIFF_HEADS, 2 * DIFF_DH)); outs_p["dv"].append(kv_p(vdf, DIFF_HEADS, 2 * DIFF_DH))
            outs_s["sbk"].append(kv_s(ksf_s, SB_HEADS, SB_DH)); outs_s["sbv"].append(kv_s(vsf_s, SB_HEADS, SB_DH))
            outs_s["dk"].append(kv_s(kdf_s, DIFF_HEADS, 2 * DIFF_DH)); outs_s["dv"].append(kv_s(vdf_s, DIFF_HEADS, 2 * DIFF_DH))

        row = lambda a: a[layer].reshape(1, D_MODEL).astype(F32)
        h = _outproj_ln(ya, yb, h, w_out.astype(BF16), row(ln1_g), row(ln1_b), alpha)
        h = _moe(h, router_w[layer].T.astype(F32), router_bias[layer].reshape(N_EXPERTS, 1).astype(F32),
                 exp_w1[layer].astype(BF16), exp_w3[layer].astype(BF16), exp_w2[layer].astype(BF16),
                 sh_w1[layer].astype(BF16), sh_w3[layer].astype(BF16), sh_w2[layer].astype(BF16),
                 row(ln2_g), row(ln2_b), alpha)

    y_prompt = h[:n_p].reshape(bsz, lp, D_MODEL)[:, PAD + N_META:]
    y_sample = h[n_p:].reshape(bsz_s, t_s, D_MODEL)
    order = ("sbk", "sbv", "dk", "dv", "s5r", "s5i", "ret")
    return ((y_prompt, y_sample) + tuple(jnp.stack(outs_p[k]) for k in order)
            + tuple(jnp.stack(outs_s[k]) for k in order))
```

```python
import functools
import math

import jax
import jax.numpy as jnp
from jax import lax
from jax.experimental import pallas as pl
from jax.experimental.pallas import tpu as pltpu

F32 = jnp.float32
BF16 = jnp.bfloat16
I32 = jnp.int32
HIGHEST = lax.Precision.HIGHEST

D_MODEL = 1024
W_HALF = 512
N_META = 16
Q_BLOCK = 128
PAD = Q_BLOCK - N_META
CHUNK = 64
CHUNK_SHIFT = 6
S5_GROUPS = 32
S5_GROUP_CH = 16
S5_STATE = 64
S5_WIDTH = S5_GROUPS * S5_STATE
S5_BLOCKS = 4
RET_HEADS = 8
RET_DK = 64
SB_HEADS = 8
SB_DH = 64
DIFF_HEADS = 4
DIFF_DH = 64
N_EXPERTS = 64
N_GROUPS = 8
GROUP_SIZE = N_EXPERTS // N_GROUPS
TOPK_GROUPS = 4
TOP_K = 6
ROUTED_SCALE = 2.5
EXPERT_BLOCK = 512
EPS = 1e-5
NEG_BIG = -1e30
KEY_BLOCK = 128
KEY_SHIFT = 7
SUBLANES = 8
LANES = 128
VMEM_LIMIT = 56 * 1024 * 1024

NT_DIMS = (((1,), (1,)), ((), ()))


def _params(*sem):
    return pltpu.CompilerParams(dimension_semantics=sem, vmem_limit_bytes=VMEM_LIMIT)


def _tile(n, cands=(512, 256, 128, 64)):
    for c in cands:
        if n % c == 0:
            return c
    raise ValueError(f"no tile for {n}")


def _layer_norm(r, g, b):
    mu = jnp.mean(r, -1, keepdims=True)
    d = r - mu
    var = jnp.mean(d * d, -1, keepdims=True)
    return d * lax.rsqrt(var + EPS) * g + b


def _proj_kernel(x_ref, w_ref, *o_refs, cols, precise):
    x = x_ref[...]
    xb = x.astype(F32) if precise else x.astype(BF16)
    for o_ref, c0 in zip(o_refs, cols):
        wd = o_ref.shape[-1]
        w = w_ref[:, c0:c0 + wd]
        if precise:
            y = jnp.dot(xb, w, precision=HIGHEST, preferred_element_type=F32)
        else:
            y = jnp.dot(xb, w, preferred_element_type=F32)
        o_ref[...] = y.astype(o_ref.dtype)


def _project(x, w, outs, row0, n, precise=False):
    k = x.shape[1]
    tm = _tile(math.gcd(n, row0) if row0 else n)
    blk0 = row0 // tm
    cols = tuple(o[0] for o in outs)
    return pl.pallas_call(
        functools.partial(_proj_kernel, cols=cols, precise=precise),
        grid=(n // tm,),
        in_specs=[pl.BlockSpec((tm, k), lambda i: (i + blk0, 0)),
                  pl.BlockSpec(w.shape, lambda i: (0, 0))],
        out_specs=[pl.BlockSpec((tm, o[1]), lambda i: (i, 0)) for o in outs],
        out_shape=[jax.ShapeDtypeStruct((n, o[1]), o[2]) for o in outs],
        compiler_params=_params("parallel"),
        name="in_proj",
    )(x, w)


def _proj_kv_kernel(x_ref, w_ref, qs_ref, qd_ref, ks_ref, vs_ref, kd_ref, vd_ref,
                    ksf_hbm, vsf_hbm, kdf_hbm, vdf_hbm, stage, sem, *, rows, nq, lt):
    b, q = pl.program_id(0), pl.program_id(1)
    step = b * nq + q
    slot = step % 2
    caches = (ksf_hbm, vsf_hbm, kdf_hbm, vdf_hbm)
    xb = x_ref[...].astype(BF16)
    col = lambda c: jnp.dot(xb, w_ref[:, c * W_HALF:(c + 1) * W_HALF], preferred_element_type=F32)
    qs_ref[...] = col(0).astype(BF16)
    qd_ref[...] = col(3).astype(BF16)
    for a, (c, o_ref) in enumerate(zip((1, 2, 4, 5), (ks_ref, vs_ref, kd_ref, vd_ref))):
        y = col(c)
        o_ref[...] = y.astype(BF16)
        stage[slot, a] = y

    def copy(slot_, a, b_, q_, first):
        if first:
            return pltpu.make_async_copy(stage.at[slot_, a, pl.ds(PAD, rows - PAD)],
                                         caches[a].at[pl.ds(pl.multiple_of(b_ * lt, SUBLANES), rows - PAD)], sem)
        dst = pl.multiple_of(b_ * lt + q_ * rows - PAD, SUBLANES)
        return pltpu.make_async_copy(stage.at[slot_, a], caches[a].at[pl.ds(dst, rows)], sem)

    def start_all(first):
        for a in range(len(caches)):
            copy(slot, a, b, q, first).start()

    def wait_all(first):
        for a in range(len(caches)):
            copy(0, a, 0, 1, first).wait()

    @pl.when((step > 0) & (q == 1))
    def _():
        wait_all(True)

    @pl.when((step > 0) & (q != 1))
    def _():
        wait_all(False)

    @pl.when(q == 0)
    def _():
        start_all(True)

    @pl.when(q != 0)
    def _():
        start_all(False)

    @pl.when(step == pl.num_programs(0) * nq - 1)
    def _():
        wait_all(False)


def _project_prompt_kv(x, w, bsz, lp):
    nq = next(c for c in (4, 3, 2) if lp % c == 0 and (lp // c) % SUBLANES == 0 and lp // c > PAD)
    rows = lp // nq
    lt = lp - PAD
    blk = lambda wd: pl.BlockSpec((rows, wd), lambda i, j: (i * nq + j, 0))
    pad_out = jax.ShapeDtypeStruct((bsz * lp, W_HALF), BF16)
    cache_out = jax.ShapeDtypeStruct((bsz * lt, W_HALF), F32)
    return pl.pallas_call(
        functools.partial(_proj_kv_kernel, rows=rows, nq=nq, lt=lt),
        grid=(bsz, nq),
        in_specs=[blk(x.shape[1]), pl.BlockSpec(w.shape, lambda i, j: (0, 0))],
        out_specs=[blk(W_HALF)] * 6 + [pl.BlockSpec(memory_space=pl.ANY)] * 4,
        out_shape=[pad_out] * 6 + [cache_out] * 4,
        scratch_shapes=[pltpu.VMEM((2, 4, rows, W_HALF), F32), pltpu.SemaphoreType.DMA(())],
        compiler_params=_params("arbitrary", "arbitrary"),
        name="in_proj_kv",
    )(x, w)


def _s5_kernel(u_ref, x0r_ref, x0i_ref, lr_ref, li_ref, bre_ref, bim_ref, cre_ref, cim_ref, d_ref,
               wglu_ref, y_ref, fr_ref, fi_ref, xr_s, xi_s, sr_s, si_s, *, tc, sg, precise):
    c = pl.program_id(1)

    @pl.when(c == 0)
    def _():
        xr_s[...] = x0r_ref[...]
        xi_s[...] = x0i_ref[...]

    def mm(a, b):
        if precise:
            return jnp.dot(a, b, precision=HIGHEST, preferred_element_type=F32)
        return jnp.dot(a.astype(BF16), b, preferred_element_type=F32)

    cw, sw = W_HALF // S5_BLOCKS, S5_WIDTH // S5_BLOCKS
    for s in range(sg):
        u = u_ref[s]
        sr_s[s] = jnp.concatenate(
            [mm(u[:, b * cw:(b + 1) * cw], bre_ref[b]) for b in range(S5_BLOCKS)], axis=1)
        si_s[s] = jnp.concatenate(
            [mm(u[:, b * cw:(b + 1) * cw], bim_ref[b]) for b in range(S5_BLOCKS)], axis=1)
    lr = lr_ref[...]
    li = li_ref[...]

    def step(t, carry):
        out = []
        for s, (xr, xi) in enumerate(carry):
            nr = lr * xr - li * xi + sr_s[s, pl.ds(t, 1), :]
            ni = lr * xi + li * xr + si_s[s, pl.ds(t, 1), :]
            sr_s[s, pl.ds(t, 1), :] = nr
            si_s[s, pl.ds(t, 1), :] = ni
            out.append((nr, ni))
        return tuple(out)

    init = tuple((xr_s[pl.ds(s, 1), :], xi_s[pl.ds(s, 1), :]) for s in range(sg))
    last = lax.fori_loop(0, tc, step, init, unroll=2)
    xr = jnp.concatenate([x[0] for x in last], axis=0)
    xi = jnp.concatenate([x[1] for x in last], axis=0)
    xr_s[...] = xr
    xi_s[...] = xi
    for s in range(sg):
        y = jnp.concatenate(
            [mm(sr_s[s, :, b * sw:(b + 1) * sw], cre_ref[b]) - mm(si_s[s, :, b * sw:(b + 1) * sw], cim_ref[b])
             for b in range(S5_BLOCKS)], axis=1) + d_ref[...] * u_ref[s]
        y = jax.nn.gelu(y)
        y = y * jax.nn.sigmoid(mm(y, wglu_ref[...]))
        y_ref[s] = y.astype(y_ref.dtype)

    @pl.when(c == pl.num_programs(1) - 1)
    def _():
        fr_ref[...] = xr
        fi_ref[...] = xi


def _s5_mixer(u, x0r, x0i, p, tc, precise):
    b, l, _ = u.shape
    nc = l // tc
    sg = math.gcd(b, SUBLANES)
    wdt = F32 if precise else BF16
    full = lambda a: pl.BlockSpec(a.shape, lambda i, j: (0,) * a.ndim)
    st_spec = pl.BlockSpec((sg, S5_WIDTH), lambda i, j: (i, 0))
    seq_spec = pl.BlockSpec((sg, tc, W_HALF), lambda i, j: (i, j, 0))
    mats = [p["bre"].astype(wdt), p["bim"].astype(wdt), p["cre"].astype(wdt), p["cim"].astype(wdt)]
    wglu = p["wglu"].astype(wdt)
    return pl.pallas_call(
        functools.partial(_s5_kernel, tc=tc, sg=sg, precise=precise),
        grid=(b // sg, nc),
        in_specs=[seq_spec, st_spec, st_spec,
                  full(p["lr"]), full(p["li"])] + [full(m) for m in mats] + [full(p["d"]), full(wglu)],
        out_specs=[seq_spec, st_spec, st_spec],
        out_shape=[jax.ShapeDtypeStruct((b, l, W_HALF), BF16),
                   jax.ShapeDtypeStruct((b, S5_WIDTH), F32),
                   jax.ShapeDtypeStruct((b, S5_WIDTH), F32)],
        scratch_shapes=[pltpu.VMEM((sg, S5_WIDTH), F32), pltpu.VMEM((sg, S5_WIDTH), F32),
                        pltpu.VMEM((sg, tc, S5_WIDTH), F32), pltpu.VMEM((sg, tc, S5_WIDTH), F32)],
        compiler_params=_params("parallel", "arbitrary"),
        name="s5_mixer",
    )(u, x0r, x0i, p["lr"], p["li"], *mats, p["d"], wglu)


def _s5_prepare(lam_re, lam_im, log_dt, b_re, b_im, c_re, c_im, d, w_glu):
    dt = jnp.exp(log_dt)[:, None]
    mag = jnp.exp(lam_re * dt)
    lb_re = mag * jnp.cos(lam_im * dt)
    lb_im = mag * jnp.sin(lam_im * dt)
    nr, ni = lb_re - 1.0, lb_im
    den = jnp.square(lam_re) + jnp.square(lam_im)
    cr = (nr * lam_re + ni * lam_im) / den
    ci = (ni * lam_re - nr * lam_im) / den
    bb_re = cr[..., None] * b_re - ci[..., None] * b_im
    bb_im = cr[..., None] * b_im + ci[..., None] * b_re
    gb = S5_GROUPS // S5_BLOCKS
    eye = jnp.eye(gb, dtype=F32)
    per_blk = lambda m: m.reshape((S5_BLOCKS, gb) + m.shape[1:])
    blk_in = lambda m: jnp.einsum("bgpi,gh->bgihp", per_blk(m), eye).reshape(
        S5_BLOCKS, W_HALF // S5_BLOCKS, S5_WIDTH // S5_BLOCKS)
    blk_out = lambda m: jnp.einsum("bgop,gh->bgpho", per_blk(m), eye).reshape(
        S5_BLOCKS, S5_WIDTH // S5_BLOCKS, W_HALF // S5_BLOCKS)
    return dict(lr=lb_re.reshape(1, S5_WIDTH), li=lb_im.reshape(1, S5_WIDTH),
                bre=blk_in(bb_re), bim=blk_in(bb_im), cre=blk_out(c_re), cim=blk_out(c_im),
                d=d.reshape(1, W_HALF), wglu=w_glu)


def _ret_kernel(q_ref, k_ref, v_ref, g_ref, cos_ref, sin_ref, dmat_ref, inter_ref, zeta_ref, gc_ref,
                r0_ref, y_ref, rf_ref, r_s, *, tc):
    c = pl.program_id(1)

    @pl.when(c == 0)
    def _():
        r_s[...] = r0_ref[0]

    lane = lax.broadcasted_iota(I32, (tc, W_HALF), 1)
    first_half = (lane & (RET_DK - 1)) < (RET_DK // 2)
    cos = cos_ref[...]
    sin = sin_ref[...]

    def rot(x):
        ahead = pltpu.roll(x, W_HALF - RET_DK // 2, 1)
        behind = pltpu.roll(x, RET_DK // 2, 1)
        return x * cos + jnp.where(first_half, ahead, behind) * sin

    q = rot(q_ref[...])
    k = rot(k_ref[...]) * (RET_DK ** -0.5)
    v = v_ref[...]
    vz = v * zeta_ref[...]
    g = g_ref[...]
    gate = g * jax.nn.sigmoid(g)
    heads = [slice(h * RET_DK, (h + 1) * RET_DK) for h in range(RET_HEADS)]
    qb = q.astype(BF16)
    kb = k.astype(BF16)
    vb = v.astype(BF16)
    vzb = vz.astype(BF16)
    kt = k.T.astype(BF16)
    scores = [lax.dot_general(qb[:, sl], kb[:, sl], NT_DIMS, preferred_element_type=F32) for sl in heads]
    cross = [jnp.dot(qb[:, sl], r_s[h].astype(BF16), preferred_element_type=F32)
             for h, sl in enumerate(heads)]
    new_r = [gc_ref[h] * r_s[h] + jnp.dot(kt[sl, :], vzb[:, sl], preferred_element_type=F32)
             for h, sl in enumerate(heads)]
    decayed = [(scores[h] * dmat_ref[h]).astype(BF16) for h in range(RET_HEADS)]
    outs = [jnp.dot(decayed[h], vb[:, sl], preferred_element_type=F32) + cross[h] * inter_ref[:, sl]
            for h, sl in enumerate(heads)]
    for h, sl in enumerate(heads):
        r_s[h] = new_r[h]
        o = outs[h]
        mu = jnp.mean(o, -1, keepdims=True)
        dlt = o - mu
        var = jnp.mean(dlt * dlt, -1, keepdims=True)
        y_ref[:, sl] = (dlt * lax.rsqrt(var + EPS) * gate[:, sl]).astype(y_ref.dtype)

    @pl.when(c == pl.num_programs(1) - 1)
    def _():
        rf_ref[0] = r_s[...]


def _ret_mixer(q, k, v, g, r0, pos, tc):
    b, l, _ = q.shape
    nc = l // tc
    inv = 1.0 / (10000.0 ** jnp.linspace(0.0, 1.0, RET_DK // 2, dtype=F32))
    ang = pos.astype(F32)[:, None] * inv[None]
    cos = jnp.tile(jnp.concatenate([jnp.cos(ang), jnp.cos(ang)], 1), (1, RET_HEADS))
    sin = jnp.tile(jnp.concatenate([-jnp.sin(ang), jnp.sin(ang)], 1), (1, RET_HEADS))
    log_gamma = jnp.log(1.0 - 2.0 ** (-5.0 - jnp.arange(RET_HEADS, dtype=F32)))
    n = jnp.arange(tc, dtype=F32)
    diff = n[:, None] - n[None, :]
    causal = diff >= 0
    dmat = jnp.where(causal[None], jnp.exp(jnp.where(causal, diff, 0.0)[None] * log_gamma[:, None, None]), 0.0)
    per_lane = lambda m: jnp.repeat(m, RET_DK, axis=1)
    inter = per_lane(jnp.exp((n[:, None] + 1.0) * log_gamma[None]))
    zeta = per_lane(jnp.exp((tc - 1.0 - n)[:, None] * log_gamma[None]))
    gc = jnp.broadcast_to(jnp.exp(tc * log_gamma)[:, None, None], (RET_HEADS, RET_DK, RET_DK))
    seq = pl.BlockSpec((None, tc, W_HALF), lambda i, j: (i, j, 0))
    tab = pl.BlockSpec((tc, W_HALF), lambda i, j: (j, 0))
    full = lambda a: pl.BlockSpec(a.shape, lambda i, j: (0,) * a.ndim)
    st = pl.BlockSpec((1, RET_HEADS, RET_DK, RET_DK), lambda i, j: (i, 0, 0, 0))
    return pl.pallas_call(
        functools.partial(_ret_kernel, tc=tc),
        grid=(b, nc),
        in_specs=[seq, seq, seq, seq, tab, tab, full(dmat), full(inter), full(zeta), full(gc), st],
        out_specs=[seq, st],
        out_shape=[jax.ShapeDtypeStruct((b, l, W_HALF), BF16),
                   jax.ShapeDtypeStruct((b, RET_HEADS, RET_DK, RET_DK), F32)],
        scratch_shapes=[pltpu.VMEM((RET_HEADS, RET_DK, RET_DK), F32)],
        compiler_params=_params("parallel", "arbitrary"),
        name="ret_mixer",
    )(q, k, v, g, cos, sin, dmat, inter, zeta, gc, r0)


def _attn_kernel(lq1_ref, lk1_ref, lq2_ref, lk2_ref, ng_ref, qs_ref, qd_ref, ks_ref, vst_ref, kd_ref,
                 vdt_ref, osb_ref, od_ref, *, bq, q_base, kv_lo, n_key_blocks, lam_init):
    qi = pl.program_id(1)
    q0 = q_base + qi * bq
    nkb = jnp.minimum(jnp.right_shift(q0 + bq + KEY_BLOCK - 1, KEY_SHIFT), n_key_blocks)
    qpos = q0 + lax.broadcasted_iota(I32, (KEY_BLOCK, bq), 1)
    koff = lax.broadcasted_iota(I32, (KEY_BLOCK, bq), 0)
    qchunk = jnp.right_shift(qpos, CHUNK_SHIFT)
    row = lax.broadcasted_iota(I32, (KEY_BLOCK, KEY_BLOCK), 0)
    col = lax.broadcasted_iota(I32, (KEY_BLOCK, KEY_BLOCK), 1)
    newer_keys = (col > row).astype(BF16)

    qs = (qs_ref[...].astype(F32) * (SB_DH ** -0.5)).astype(BF16)
    qd = (qd_ref[...].astype(F32) * (DIFF_DH ** -0.5)).astype(BF16)
    sb_q = [qs[:, h * SB_DH:(h + 1) * SB_DH] for h in range(SB_HEADS)]
    d_q = [qd[:, c * DIFF_DH:(c + 1) * DIFF_DH] for c in range(2 * DIFF_HEADS)]

    def body(t, carry, edge, nblk, j0):
        sb_acc, sb_tail, d_m, d_l, d_a = carry
        blocks = range(nblk)
        js = [j0 + nblk * t + i for i in blocks]
        kbs = [nkb - 1 - j for j in js]
        kblks = [ks_ref[kb] for kb in kbs]
        dblks = [kd_ref[j] for j in js]
        if edge:
            masks, dmasks = [], []
            for i in blocks:
                kpos = kbs[i] * KEY_BLOCK + koff
                masks.append((kpos >= kv_lo) & (kpos < qpos))
                dpos = js[i] * KEY_BLOCK + koff
                dmasks.append((dpos >= kv_lo) & (jnp.right_shift(dpos, CHUNK_SHIFT) <= qchunk))
            keep = lambda m, x, other: jnp.where(m, x, other)
        else:
            masks = dmasks = [None] * nblk
            keep = lambda m, x, other: x
        zs = [[lax.dot_general(kblks[i][:, h * SB_DH:(h + 1) * SB_DH], sb_q[h], NT_DIMS,
                               preferred_element_type=F32) for h in range(SB_HEADS)] for i in blocks]
        ss = [[lax.dot_general(dblks[i][:, c * DIFF_DH:(c + 1) * DIFF_DH], d_q[c], NT_DIMS,
                               preferred_element_type=F32) for c in range(2 * DIFF_HEADS)] for i in blocks]
        log_betas = [[None] * SB_HEADS for _ in blocks]
        splits = [[None] * SB_HEADS for _ in blocks]
        tails = [[None] * SB_HEADS for _ in blocks]
        new_tail = []
        for h in range(SB_HEADS):
            tail = sb_tail[h]
            for i in blocks:
                z = zs[i][h]
                sp = jnp.log(1.0 + jnp.exp(-jnp.abs(z)))
                log_1mb = keep(masks[i], -(jnp.maximum(z, 0.0) + sp), 0.0)
                log_betas[i][h] = jnp.minimum(z, 0.0) - sp
                hi = log_1mb.astype(BF16)
                lo = (log_1mb - hi.astype(F32)).astype(BF16)
                splits[i][h] = jnp.concatenate([hi, lo], axis=1)
                tails[i][h] = tail
                tail = tail + jnp.sum(log_1mb, axis=0, keepdims=True)
            new_tail.append(tail)
        new_m, new_l, alphas = [], [], []
        probs = [[None] * (2 * DIFF_HEADS) for _ in blocks]
        for c in range(2 * DIFF_HEADS):
            s = [keep(dmasks[i], ss[i][c], NEG_BIG) for i in blocks]
            mn = d_m[c]
            for i in blocks:
                mn = jnp.maximum(mn, jnp.max(s[i], axis=0, keepdims=True))
            al = jnp.exp(d_m[c] - mn)
            l = al * d_l[c]
            for i in blocks:
                p = jnp.exp(s[i] - mn)
                l = l + jnp.sum(p, axis=0, keepdims=True)
                probs[i][c] = p.astype(BF16)
            new_m.append(mn)
            new_l.append(l)
            alphas.append(al)
        newer = [[jnp.dot(newer_keys, splits[i][h], preferred_element_type=F32)
                  for h in range(SB_HEADS)] for i in blocks]
        new_a = []
        for c in range(2 * DIFF_HEADS):
            a = alphas[c] * d_a[c]
            vsl = slice((c // 2) * 2 * DIFF_DH, (c // 2 + 1) * 2 * DIFF_DH)
            for i in blocks:
                a = a + jnp.dot(vdt_ref[js[i], vsl, :], probs[i][c], preferred_element_type=F32)
            new_a.append(a)
        ws = [[keep(masks[i], jnp.exp(log_betas[i][h] + newer[i][h][:, :bq] + newer[i][h][:, bq:]
                                      + tails[i][h]), 0.0).astype(BF16)
               for h in range(SB_HEADS)] for i in blocks]
        new_acc = []
        for h in range(SB_HEADS):
            acc = sb_acc[h]
            for i in blocks:
                acc = acc + jnp.dot(vst_ref[kbs[i], h * SB_DH:(h + 1) * SB_DH, :], ws[i][h],
                                    preferred_element_type=F32)
            new_acc.append(acc)
        return tuple(new_acc), tuple(new_tail), tuple(new_m), tuple(new_l), tuple(new_a)

    zrow = jnp.zeros((1, bq), F32)
    init = (tuple(jnp.zeros((SB_DH, bq), F32) for _ in range(SB_HEADS)),
            tuple(zrow for _ in range(SB_HEADS)),
            tuple(jnp.full((1, bq), NEG_BIG, F32) for _ in range(2 * DIFF_HEADS)),
            tuple(zrow for _ in range(2 * DIFF_HEADS)),
            tuple(jnp.zeros((2 * DIFF_DH, bq), F32) for _ in range(2 * DIFF_HEADS)))
    edge_body = functools.partial(body, edge=True, nblk=1, j0=0)
    n_pairs = jnp.right_shift(jnp.maximum(nkb - 2, 0), 1)
    carry = lax.fori_loop(0, jnp.minimum(nkb, 1), edge_body, init)
    carry = lax.fori_loop(0, n_pairs, functools.partial(body, edge=False, nblk=2, j0=1), carry)
    carry = lax.fori_loop(1 + 2 * n_pairs, nkb - 1, functools.partial(body, edge=False, nblk=1, j0=0), carry)
    sb_acc, _, _, d_l, d_a = lax.fori_loop(jnp.maximum(nkb - 1, 1), nkb, edge_body, carry)

    osb_ref[...] = jnp.concatenate(sb_acc, axis=0).T.astype(osb_ref.dtype)
    lam = (jnp.exp(jnp.sum(lq1_ref[...] * lk1_ref[...], axis=1, keepdims=True))
           - jnp.exp(jnp.sum(lq2_ref[...] * lk2_ref[...], axis=1, keepdims=True)) + lam_init)
    outs = []
    for h in range(DIFF_HEADS):
        o = d_a[2 * h] / d_l[2 * h] - lam * (d_a[2 * h + 1] / d_l[2 * h + 1])
        o = o * lax.rsqrt(jnp.mean(o * o, axis=0, keepdims=True) + EPS) * ng_ref[...]
        outs.append(o * (1.0 - lam_init))
    od_ref[...] = jnp.concatenate(outs, axis=0).T.astype(od_ref.dtype)


def _attention(qs, qd, ks, vs, kd, vd, lam_vecs, norm_g, *, bq, q_base, kv_lo, lam_init):
    b, lq, _ = qs.shape
    lk = ks.shape[1]
    nkb = lk // KEY_BLOCK
    assert bq == KEY_BLOCK and q_base % KEY_BLOCK == 0 and 0 <= kv_lo < KEY_BLOCK
    kblocks = lambda a: a.reshape(b, nkb, KEY_BLOCK, W_HALF)
    vblocks_t = lambda a: a.reshape(b, nkb, KEY_BLOCK, W_HALF).transpose(0, 1, 3, 2)
    qspec = pl.BlockSpec((None, bq, W_HALF), lambda i, j: (i, j, 0))
    kspec = pl.BlockSpec((None, nkb, KEY_BLOCK, W_HALF), lambda i, j: (i, 0, 0, 0))
    vspec = pl.BlockSpec((None, nkb, W_HALF, KEY_BLOCK), lambda i, j: (i, 0, 0, 0))
    vec = pl.BlockSpec((1, DIFF_DH), lambda i, j: (0, 0))
    return pl.pallas_call(
        functools.partial(_attn_kernel, bq=bq, q_base=q_base, kv_lo=kv_lo,
                          n_key_blocks=nkb, lam_init=lam_init),
        grid=(b, lq // bq),
        in_specs=[vec, vec, vec, vec, pl.BlockSpec((2 * DIFF_DH, 1), lambda i, j: (0, 0)),
                  qspec, qspec, kspec, vspec, kspec, vspec],
        out_specs=[qspec, qspec],
        out_shape=[jax.ShapeDtypeStruct((b, lq, W_HALF), BF16), jax.ShapeDtypeStruct((b, lq, W_HALF), BF16)],
        compiler_params=_params("parallel", "arbitrary"),
        name="sb_diff_attention",
    )(*lam_vecs, norm_g, qs, qd, kblocks(ks), vblocks_t(vs), kblocks(kd), vblocks_t(vd))


def _outproj_ln_kernel(yap_ref, ybp_ref, yas_ref, ybs_ref, h_ref, w_ref, g_ref, b_ref, o_ref, *,
                       alpha, prompt_blocks):
    is_prompt = pl.program_id(0) < prompt_blocks
    ya = jnp.where(is_prompt, yap_ref[...], yas_ref[...])
    yb = jnp.where(is_prompt, ybp_ref[...], ybs_ref[...])
    m = (jnp.dot(ya, w_ref[:W_HALF, :], preferred_element_type=F32)
         + jnp.dot(yb, w_ref[W_HALF:, :], preferred_element_type=F32))
    o_ref[...] = _layer_norm(alpha * h_ref[...] + m, g_ref[...], b_ref[...])


def _outproj_ln(ya_p, yb_p, ya_s, yb_s, h, w, g, b, alpha):
    n = h.shape[0]
    n_p, n_s = ya_p.shape[0], ya_s.shape[0]
    tm = _tile(math.gcd(n_p, n_s))
    pb = n_p // tm
    row = pl.BlockSpec((tm, D_MODEL), lambda i: (i, 0))
    prompt = pl.BlockSpec((tm, W_HALF), lambda i: (jnp.minimum(i, pb - 1), 0))
    sample = pl.BlockSpec((tm, W_HALF), lambda i: (jnp.maximum(i - pb, 0), 0))
    full = lambda a: pl.BlockSpec(a.shape, lambda i: (0, 0))
    return pl.pallas_call(
        functools.partial(_outproj_ln_kernel, alpha=alpha, prompt_blocks=pb),
        grid=(n // tm,),
        in_specs=[prompt, prompt, sample, sample, row, full(w), full(g), full(b)],
        out_specs=row,
        out_shape=jax.ShapeDtypeStruct((n, D_MODEL), F32),
        compiler_params=_params("parallel"),
        name="out_proj_ln",
    )(ya_p, yb_p, ya_s, yb_s, h, w, g, b)


def _router_kernel(x_ref, wt_ref, bias_ref, eidx_ref, gate_ref, rank_ref, cnt_ref, run_s, *, tm):
    i = pl.program_id(0)

    @pl.when(i == 0)
    def _():
        run_s[...] = jnp.zeros_like(run_s)

    logits = lax.dot_general(wt_ref[...], x_ref[...], NT_DIMS, precision=HIGHEST,
                             preferred_element_type=F32)
    scores = jax.nn.sigmoid(logits)
    sel = scores + bias_ref[...]
    ninf = -jnp.inf

    sel3 = sel.reshape(N_GROUPS, GROUP_SIZE, tm)
    within = lax.broadcasted_iota(I32, (N_GROUPS, GROUP_SIZE, tm), 1)
    m1 = jnp.max(sel3, axis=1, keepdims=True)
    first = jnp.min(jnp.where(sel3 == m1, within, GROUP_SIZE), axis=1, keepdims=True)
    m2 = jnp.max(jnp.where(within == first, ninf, sel3), axis=1, keepdims=True)
    gscore = (m1 + m2).reshape(N_GROUPS, tm)

    gid = lax.broadcasted_iota(I32, (N_GROUPS, tm), 0)
    gsel = jnp.zeros((N_GROUPS, tm), jnp.bool_)
    cur = gscore
    for _ in range(TOPK_GROUPS):
        m = jnp.max(cur, axis=0, keepdims=True)
        f = jnp.min(jnp.where(cur == m, gid, N_GROUPS), axis=0, keepdims=True)
        pick = gid == f
        gsel = gsel | pick
        cur = jnp.where(pick, ninf, cur)
    emask = jnp.broadcast_to(gsel.reshape(N_GROUPS, 1, tm), (N_GROUPS, GROUP_SIZE, tm)).reshape(N_EXPERTS, tm)

    eid = lax.broadcasted_iota(I32, (N_EXPERTS, tm), 0)
    cur = jnp.where(emask, sel, ninf)
    picks, idxs, gates = [], [], []
    for _ in range(TOP_K):
        m = jnp.max(cur, axis=0, keepdims=True)
        f = jnp.min(jnp.where(cur == m, eid, N_EXPERTS), axis=0, keepdims=True)
        pick = eid == f
        picks.append(pick)
        idxs.append(f)
        gates.append(jnp.sum(jnp.where(pick, scores, 0.0), axis=0, keepdims=True))
        cur = jnp.where(pick, ninf, cur)
    gsum = gates[0]
    for gk in gates[1:]:
        gsum = gsum + gk

    chosen = picks[0]
    for pk in picks[1:]:
        chosen = chosen | pk
    chosen_f = chosen.astype(F32)
    r_ = lax.broadcasted_iota(I32, (tm, tm), 0)
    c_ = lax.broadcasted_iota(I32, (tm, tm), 1)
    earlier = (r_ < c_).astype(BF16)
    before = run_s[...] + jnp.dot(chosen_f.astype(BF16), earlier, preferred_element_type=F32)
    zrow_i = jnp.zeros((1, tm), I32)
    zrow_f = jnp.zeros((1, tm), F32)
    ranks = [jnp.sum(jnp.where(pk, before, 0.0), axis=0, keepdims=True).astype(I32) for pk in picks]
    eidx_ref[...] = jnp.concatenate(idxs + [zrow_i, zrow_i], axis=0)
    gate_ref[...] = jnp.concatenate([gk / gsum * ROUTED_SCALE for gk in gates] + [zrow_f, zrow_f], axis=0)
    rank_ref[...] = jnp.concatenate(ranks + [zrow_i, zrow_i], axis=0)
    run_s[...] = run_s[...] + jnp.sum(chosen_f, axis=1, keepdims=True)
    cnt_ref[...] = jnp.broadcast_to(run_s[...], cnt_ref.shape)


def _router(x, wt, bias):
    n = x.shape[0]
    tm = _tile(n, (512, 256, 128))
    lane_blk = pl.BlockSpec((8, tm), lambda i: (0, i))
    return pl.pallas_call(
        functools.partial(_router_kernel, tm=tm),
        grid=(n // tm,),
        in_specs=[pl.BlockSpec((tm, D_MODEL), lambda i: (i, 0)),
                  pl.BlockSpec(wt.shape, lambda i: (0, 0)),
                  pl.BlockSpec(bias.shape, lambda i: (0, 0))],
        out_specs=[lane_blk, lane_blk, lane_blk, pl.BlockSpec((N_EXPERTS, 128), lambda i: (0, 0))],
        out_shape=[jax.ShapeDtypeStruct((8, n), I32), jax.ShapeDtypeStruct((8, n), F32),
                   jax.ShapeDtypeStruct((8, n), I32), jax.ShapeDtypeStruct((N_EXPERTS, 128), F32)],
        scratch_shapes=[pltpu.VMEM((N_EXPERTS, 1), F32)],
        compiler_params=_params("arbitrary"),
        name="moe_router",
    )(x, wt, bias)


ROW_SUB = D_MODEL // 2 // LANES
HIGH_HALF = -65536
DMA_PRIORITIES = 2


def _pack_rows(dst_ref, x, r0=0):
    rows = x.shape[0]
    bits = lambda v: lax.bitcast_convert_type(v.astype(BF16).astype(F32), I32)
    w = (lax.shift_right_logical(bits(x[:, :D_MODEL // 2]), jnp.full((), 16, I32))
         | (bits(x[:, D_MODEL // 2:]) & HIGH_HALF))
    for c in range(ROW_SUB):
        dst_ref[pl.ds(r0 * ROW_SUB + c, rows, stride=ROW_SUB), :] = w[:, c * LANES:(c + 1) * LANES]


def _unpack_rows(src_ref, rows, r0=0):
    ws = [src_ref[pl.ds(r0 * ROW_SUB + c, rows, stride=ROW_SUB), :] for c in range(ROW_SUB)]
    lo = [lax.bitcast_convert_type(lax.shift_left(w, jnp.full((), 16, I32)), F32) for w in ws]
    hi = [lax.bitcast_convert_type(w & HIGH_HALF, F32) for w in ws]
    return jnp.concatenate(lo + hi, axis=1)


def _dispatch_kernel(*refs, tm):
    n_idx = TOP_K * tm // LANES
    dest_refs, (x_ref, o_hbm, buf, sem) = refs[:n_idx], refs[n_idx:]
    i = pl.program_id(0)
    slot = i % 2
    stage = buf.at[slot]
    _pack_rows(stage, x_ref[...])

    def wait_all():
        for _ in range(TOP_K):
            pltpu.make_async_copy(stage, o_hbm.at[pl.ds(0, tm * ROW_SUB)], sem).wait()

    @pl.when(i > 0)
    def _():
        wait_all()

    for k in range(TOP_K):
        for blk in range(tm // LANES):
            def start(t, carry, k=k, blk=blk):
                for prio in range(DMA_PRIORITIES):
                    jj = t * DMA_PRIORITIES + prio
                    d = pl.multiple_of(dest_refs[k * (tm // LANES) + blk][0, 0, jj] * ROW_SUB, ROW_SUB)
                    s = pl.multiple_of((blk * LANES + jj) * ROW_SUB, ROW_SUB)
                    pltpu.make_async_copy(stage.at[pl.ds(s, ROW_SUB)], o_hbm.at[pl.ds(d, ROW_SUB)],
                                          sem).start(priority=prio)
                return carry
            lax.fori_loop(0, LANES // DMA_PRIORITIES, start, 0, unroll=4)

    @pl.when(i == pl.num_programs(0) - 1)
    def _():
        wait_all()


def _slot_spec():
    return pl.BlockSpec((1, 1, LANES), lambda i: (i, 0, 0), memory_space=pltpu.SMEM)


def _dispatch(x, dests, n_rows, tm):
    n = x.shape[0]
    return pl.pallas_call(
        functools.partial(_dispatch_kernel, tm=tm),
        grid=(n // tm,),
        in_specs=[_slot_spec() for _ in dests] + [pl.BlockSpec((tm, D_MODEL), lambda i: (i, 0))],
        out_specs=pl.BlockSpec(memory_space=pl.ANY),
        out_shape=jax.ShapeDtypeStruct((n_rows * ROW_SUB, LANES), I32),
        scratch_shapes=[pltpu.VMEM((2, tm * ROW_SUB, LANES), I32), pltpu.SemaphoreType.DMA(())],
        compiler_params=_params("arbitrary"),
        name="moe_dispatch",
    )(*dests, x)


def _ffn_kernel(be_ref, nv_ref, xs_ref, w1_ref, w3_ref, w2_ref, o_ref):
    i = pl.program_id(0)
    x = _unpack_rows(xs_ref, EXPERT_BLOCK)
    live = lax.broadcasted_iota(I32, (EXPERT_BLOCK, 1), 0) < nv_ref[i]
    x = jnp.where(live, x, 0.0).astype(BF16)
    a = jnp.dot(x, w1_ref[0], preferred_element_type=F32)
    b = jnp.dot(x, w3_ref[0], preferred_element_type=F32)
    hdn = (a * jax.nn.sigmoid(a) * b).astype(BF16)
    _pack_rows(o_ref, jnp.dot(hdn, w2_ref[0], preferred_element_type=F32))


def _expert_ffn(xs, blk_e, blk_valid, w1, w3, w2):
    nb = xs.shape[0] // (EXPERT_BLOCK * ROW_SUB)
    de = w1.shape[-1]
    rows = pl.BlockSpec((EXPERT_BLOCK * ROW_SUB, LANES), lambda i, be, nv: (i, 0))
    grid_spec = pltpu.PrefetchScalarGridSpec(
        num_scalar_prefetch=2,
        grid=(nb,),
        in_specs=[rows,
                  pl.BlockSpec((1, D_MODEL, de), lambda i, be, nv: (be[i], 0, 0)),
                  pl.BlockSpec((1, D_MODEL, de), lambda i, be, nv: (be[i], 0, 0)),
                  pl.BlockSpec((1, de, D_MODEL), lambda i, be, nv: (be[i], 0, 0))],
        out_specs=rows,
    )
    return pl.pallas_call(
        _ffn_kernel,
        grid_spec=grid_spec,
        out_shape=jax.ShapeDtypeStruct(xs.shape, I32),
        compiler_params=_params("arbitrary"),
        name="moe_expert_ffn",
    )(blk_e, blk_valid, xs, w1, w3, w2)


def _combine_kernel(*refs, tm, alpha):
    n_idx = TOP_K * tm // LANES
    dest_refs = refs[:n_idx]
    gate_ref, x_ref, ys_hbm, sw1_ref, sw3_ref, sw2_ref, g_ref, b_ref, o_ref, buf, sem = refs[n_idx:]
    for k in range(TOP_K):
        for blk in range(tm // LANES):
            def start(t, carry, k=k, blk=blk):
                for prio in range(DMA_PRIORITIES):
                    jj = t * DMA_PRIORITIES + prio
                    s = pl.multiple_of(dest_refs[k * (tm // LANES) + blk][0, 0, jj] * ROW_SUB, ROW_SUB)
                    d = pl.multiple_of((k * tm + blk * LANES + jj) * ROW_SUB, ROW_SUB)
                    pltpu.make_async_copy(ys_hbm.at[pl.ds(s, ROW_SUB)], buf.at[pl.ds(d, ROW_SUB)],
                                          sem).start(priority=prio)
                return carry
            lax.fori_loop(0, LANES // DMA_PRIORITIES, start, 0, unroll=4)

    x = x_ref[...]
    xb = x.astype(BF16)
    a = jnp.dot(xb, sw1_ref[...], preferred_element_type=F32)
    b = jnp.dot(xb, sw3_ref[...], preferred_element_type=F32)
    shared = jnp.dot((a * jax.nn.sigmoid(a) * b).astype(BF16), sw2_ref[...], preferred_element_type=F32)

    for k in range(TOP_K):
        grp = pl.ds(k * tm * ROW_SUB, tm * ROW_SUB)
        pltpu.make_async_copy(ys_hbm.at[pl.ds(0, tm * ROW_SUB)], buf.at[grp], sem).wait()

    gate = gate_ref[...]
    routed = gate[:, 0:1] * _unpack_rows(buf, tm, 0)
    for k in range(1, TOP_K):
        routed = routed + gate[:, k:k + 1] * _unpack_rows(buf, tm, k * tm)
    o_ref[...] = _layer_norm(alpha * x + routed + shared, g_ref[...], b_ref[...])


def _combine(dests, gate_t, x, ys, sw1, sw3, sw2, g, b, alpha, tm):
    n = x.shape[0]
    full = lambda a: pl.BlockSpec(a.shape, lambda i: (0,) * a.ndim)
    return pl.pallas_call(
        functools.partial(_combine_kernel, tm=tm, alpha=alpha),
        grid=(n // tm,),
        in_specs=[_slot_spec() for _ in dests] + [
                  pl.BlockSpec((tm, 8), lambda i: (i, 0)),
                  pl.BlockSpec((tm, D_MODEL), lambda i: (i, 0)),
                  pl.BlockSpec(memory_space=pl.ANY),
                  full(sw1), full(sw3), full(sw2), full(g), full(b)],
        out_specs=pl.BlockSpec((tm, D_MODEL), lambda i: (i, 0)),
        out_shape=jax.ShapeDtypeStruct((n, D_MODEL), F32),
        scratch_shapes=[pltpu.VMEM((TOP_K * tm * ROW_SUB, LANES), I32), pltpu.SemaphoreType.DMA(())],
        compiler_params=_params("arbitrary"),
        name="moe_combine_ln",
    )(*dests, gate_t, x, ys, sw1, sw3, sw2, g, b)


def _moe(x, rw_t, r_bias, w1, w3, w2, sw1, sw3, sw2, g, b, alpha):
    n = x.shape[0]
    eidx, gate, rank, cnt = _router(x, rw_t, r_bias)
    counts = cnt[:, 0].astype(I32)
    padded = (counts + EXPERT_BLOCK - 1) // EXPERT_BLOCK * EXPERT_BLOCK
    pad_end = jnp.cumsum(padded)
    pad_start = pad_end - padded
    onehot = eidx[:, :, None] == jnp.arange(N_EXPERTS, dtype=I32)[None, None, :]
    dest = jnp.sum(jnp.where(onehot, pad_start[None, None, :], 0), axis=-1) + rank
    n_rows = -(-(n * TOP_K + N_EXPERTS * (EXPERT_BLOCK - 1)) // EXPERT_BLOCK) * EXPERT_BLOCK
    nb = n_rows // EXPERT_BLOCK
    blk_lo = jnp.arange(nb, dtype=I32) * EXPERT_BLOCK
    blk_e = jnp.minimum(jnp.sum((pad_end[None, :] <= blk_lo[:, None]).astype(I32), axis=1), N_EXPERTS - 1)
    blk_valid = jnp.clip(jnp.take(pad_start + counts, blk_e) - blk_lo, 0, EXPERT_BLOCK).astype(I32)
    tm = _tile(n, (256, 128))
    runs = dest.reshape(8, n // tm, tm // LANES, 1, LANES)
    dests = [runs[k, :, r] for k in range(TOP_K) for r in range(tm // LANES)]
    xs = _dispatch(x, dests, n_rows, tm)
    ys = _expert_ffn(xs, blk_e, blk_valid, w1, w3, w2)
    return _combine(dests, gate.T, x, ys, sw1, sw3, sw2, g, b, alpha, tm)


def kernel(x_prompt, x_sample, cache_sb_k, cache_sb_v, cache_diff_k, cache_diff_v, state_s5_re, state_s5_im, state_ret, meta_tokens, w_in_e, s5_lam_re, s5_lam_im, s5_log_dt, s5_b_re, s5_b_im, s5_c_re, s5_c_im, s5_d, s5_w_glu, w_out_e, w_in_o, diff_lq1, diff_lk1, diff_lq2, diff_lk2, diff_norm_g, w_out_o, ln1_g, ln1_b, ln2_g, ln2_b, router_w, router_bias, exp_w1, exp_w3, exp_w2, sh_w1, sh_w3, sh_w2):
    bsz, seq, _ = x_prompt.shape
    bsz_s, t_s, _ = x_sample.shape
    past = cache_sb_k.shape[2]
    depth = ln1_g.shape[0]
    alpha = (2 * depth) ** 0.25
    lp = PAD + N_META + seq
    n_p = bsz * lp
    n_s = bsz_s * t_s
    lt = N_META + seq

    meta = jnp.broadcast_to(meta_tokens[None].astype(F32), (bsz, N_META, D_MODEL))
    hp = jnp.concatenate([jnp.zeros((bsz, PAD, D_MODEL), F32), meta, x_prompt], 1)
    h = jnp.concatenate([hp.reshape(n_p, D_MODEL), x_sample.reshape(n_s, D_MODEL)], 0)

    outs_p = {k: [] for k in ("sbk", "sbv", "dk", "dv", "s5r", "s5i", "ret")}
    outs_s = {k: [] for k in ("sbk", "sbv", "dk", "dv", "s5r", "s5i", "ret")}
    half = lambda c0, dt=F32: (c0 * W_HALF, W_HALF, dt)

    for layer in range(depth):
        i = layer // 2
        if layer % 2 == 0:
            w_in = w_in_e[i]
            splits = [half(c) for c in range(5)]
            u_p, q_p, k_p, v_p, g_p = _project(h, w_in.astype(BF16), splits, 0, n_p)
            u_s, q_s, k_s, v_s, g_s = _project(h, w_in, splits, n_p, n_s, precise=True)
            s5p = _s5_prepare(s5_lam_re[i], s5_lam_im[i], s5_log_dt[i], s5_b_re[i], s5_b_im[i],
                              s5_c_re[i], s5_c_im[i], s5_d[i], s5_w_glu[i])
            seq_p = lambda a: a.reshape(bsz, lp, W_HALF)
            seq_s = lambda a: a.reshape(bsz_s, t_s, W_HALF)
            z5 = jnp.zeros((bsz, S5_WIDTH), F32)
            y5_p, fr_p, fi_p = _s5_mixer(seq_p(u_p), z5, z5, s5p, Q_BLOCK, False)
            y5_s, fr_s, fi_s = _s5_mixer(seq_s(u_s), state_s5_re[i].reshape(bsz_s, S5_WIDTH).astype(F32),
                                         state_s5_im[i].reshape(bsz_s, S5_WIDTH).astype(F32), s5p, t_s, True)
            yr_p, rt_p = _ret_mixer(seq_p(q_p), seq_p(k_p), seq_p(v_p), seq_p(g_p),
                                    jnp.zeros((bsz, RET_HEADS, RET_DK, RET_DK), F32),
                                    jnp.arange(lp) - PAD, Q_BLOCK)
            yr_s, rt_s = _ret_mixer(seq_s(q_s), seq_s(k_s), seq_s(v_s), seq_s(g_s),
                                    state_ret[i].astype(F32), past + jnp.arange(t_s), t_s)
            mixed = (y5_p.reshape(n_p, W_HALF), yr_p.reshape(n_p, W_HALF),
                     y5_s.reshape(n_s, W_HALF), yr_s.reshape(n_s, W_HALF))
            w_out = w_out_e[i]
            st = lambda a, nb_: a.reshape(nb_, S5_GROUPS, S5_STATE)
            outs_p["s5r"].append(st(fr_p, bsz)); outs_p["s5i"].append(st(fi_p, bsz)); outs_p["ret"].append(rt_p)
            outs_s["s5r"].append(st(fr_s, bsz_s)); outs_s["s5i"].append(st(fi_s, bsz_s)); outs_s["ret"].append(rt_s)
        else:
            splits = [half(0, BF16), half(1), half(2), half(3, BF16), half(4), half(5),
                      half(1, BF16), half(2, BF16), half(4, BF16), half(5, BF16)]
            w_in = w_in_o[i].astype(BF16)
            (qs, qd, ksb, vsb, kdb, vdb, ksf, vsf, kdf, vdf) = _project_prompt_kv(h, w_in, bsz, lp)
            (qs_s, ksf_s, vsf_s, qd_s, kdf_s, vdf_s, ksb_s, vsb_s, kdb_s, vdb_s) = _project(h, w_in, splits, n_p, n_s)
            lam_init = 0.8 - 0.6 * math.exp(-0.3 * layer)
            lam_vecs = [v[i].reshape(1, DIFF_DH).astype(F32) for v in (diff_lq1, diff_lk1, diff_lq2, diff_lk2)]
            norm_g = diff_norm_g[i].reshape(2 * DIFF_DH, 1).astype(F32)
            seq_p = lambda a: a.reshape(bsz, lp, W_HALF)
            seq_s = lambda a: a.reshape(bsz_s, t_s, W_HALF)
            osb_p, od_p = _attention(seq_p(qs), seq_p(qd), seq_p(ksb), seq_p(vsb), seq_p(kdb), seq_p(vdb),
                                     lam_vecs, norm_g, bq=Q_BLOCK, q_base=0, kv_lo=PAD, lam_init=lam_init)
            bq_s = -(-t_s // Q_BLOCK) * Q_BLOCK
            lk_s = -(-(past + bq_s) // KEY_BLOCK) * KEY_BLOCK
            qpad = lambda a: jnp.pad(seq_s(a), ((0, 0), (0, bq_s - t_s), (0, 0)))

            def keys(cache, new):
                c = cache.reshape(bsz_s, past, W_HALF).astype(BF16)
                z = jnp.zeros((bsz_s, lk_s - past - t_s, W_HALF), BF16)
                return jnp.concatenate([c, seq_s(new), z], 1)
            osb_s, od_s = _attention(qpad(qs_s), qpad(qd_s), keys(cache_sb_k[i], ksb_s), keys(cache_sb_v[i], vsb_s),
                                     keys(cache_diff_k[i], kdb_s), keys(cache_diff_v[i], vdb_s),
                                     lam_vecs, norm_g, bq=bq_s, q_base=past, kv_lo=0, lam_init=lam_init)
            mixed = (osb_p.reshape(n_p, W_HALF), od_p.reshape(n_p, W_HALF),
                     osb_s[:, :t_s].reshape(n_s, W_HALF), od_s[:, :t_s].reshape(n_s, W_HALF))
            w_out = w_out_o[i]
            kv_p = lambda a, hh, dd: a.reshape(bsz, lt, hh, dd)
            kv_s = lambda a, hh, dd: a.reshape(bsz_s, t_s, hh, dd)
            outs_p["sbk"].append(kv_p(ksf, SB_HEADS, SB_DH)); outs_p["sbv"].append(kv_p(vsf, SB_HEADS, SB_DH))
            outs_p["dk"].append(kv_p(kdf, DIFF_HEADS, 2 * DIFF_DH)); outs_p["dv"].append(kv_p(vdf, DIFF_HEADS, 2 * DIFF_DH))
            outs_s["sbk"].append(kv_s(ksf_s, SB_HEADS, SB_DH)); outs_s["sbv"].append(kv_s(vsf_s, SB_HEADS, SB_DH))
            outs_s["dk"].append(kv_s(kdf_s, DIFF_HEADS, 2 * DIFF_DH)); outs_s["dv"].append(kv_s(vdf_s, DIFF_HEADS, 2 * DIFF_DH))

        row = lambda a: a[layer].reshape(1, D_MODEL).astype(F32)
        h = _outproj_ln(*mixed, h, w_out.astype(BF16), row(ln1_g), row(ln1_b), alpha)
        h = _moe(h, router_w[layer].T.astype(F32), router_bias[layer].reshape(N_EXPERTS, 1).astype(F32),
                 exp_w1[layer].astype(BF16), exp_w3[layer].astype(BF16), exp_w2[layer].astype(BF16),
                 sh_w1[layer].astype(BF16), sh_w3[layer].astype(BF16), sh_w2[layer].astype(BF16),
                 row(ln2_g), row(ln2_b), alpha)

    y_prompt = h[:n_p].reshape(bsz, lp, D_MODEL)[:, PAD + N_META:]
    y_sample = h[n_p:].reshape(bsz_s, t_s, D_MODEL)
    order = ("sbk", "sbv", "dk", "dv", "s5r", "s5i", "ret")
    return ((y_prompt, y_sample) + tuple(jnp.stack(outs_p[k]) for k in order)
            + tuple(jnp.stack(outs_s[k]) for k in order))
```

```python
import functools
import math

import jax
import jax.numpy as jnp
from jax import lax
from jax.experimental import pallas as pl
from jax.experimental.pallas import tpu as pltpu

F32 = jnp.float32
BF16 = jnp.bfloat16
I32 = jnp.int32
HIGHEST = lax.Precision.HIGHEST

D_MODEL = 1024
W_HALF = 512
N_META = 16
Q_BLOCK = 128
PAD = Q_BLOCK - N_META
CHUNK = 64
CHUNK_SHIFT = 6
S5_GROUPS = 32
S5_GROUP_CH = 16
S5_STATE = 64
S5_WIDTH = S5_GROUPS * S5_STATE
S5_BLOCKS = 4
RET_HEADS = 8
RET_DK = 64
SB_HEADS = 8
SB_DH = 64
DIFF_HEADS = 4
DIFF_DH = 64
N_EXPERTS = 64
N_GROUPS = 8
GROUP_SIZE = N_EXPERTS // N_GROUPS
TOPK_GROUPS = 4
TOP_K = 6
ROUTED_SCALE = 2.5
EXPERT_BLOCK = 1024
EPS = 1e-5
NEG_BIG = -1e30
KEY_BLOCK = 128
KEY_SHIFT = 7
SUBLANES = 8
LANES = 128
VMEM_LIMIT = 56 * 1024 * 1024

NT_DIMS = (((1,), (1,)), ((), ()))


def _params(*sem):
    return pltpu.CompilerParams(dimension_semantics=sem, vmem_limit_bytes=VMEM_LIMIT)


def _tile(n, cands=(512, 256, 128, 64)):
    for c in cands:
        if n % c == 0:
            return c
    raise ValueError(f"no tile for {n}")


def _layer_norm(r, g, b):
    mu = jnp.mean(r, -1, keepdims=True)
    d = r - mu
    var = jnp.mean(d * d, -1, keepdims=True)
    return d * lax.rsqrt(var + EPS) * g + b


def _proj_kernel(x_ref, w_ref, *o_refs, cols, precise):
    x = x_ref[...]
    xb = x.astype(F32) if precise else x.astype(BF16)
    for o_ref, c0 in zip(o_refs, cols):
        wd = o_ref.shape[-1]
        w = w_ref[:, c0:c0 + wd]
        if precise:
            y = jnp.dot(xb, w, precision=HIGHEST, preferred_element_type=F32)
        else:
            y = jnp.dot(xb, w, preferred_element_type=F32)
        o_ref[...] = y.astype(o_ref.dtype)


def _project(x, w, outs, row0, n, precise=False):
    k = x.shape[1]
    tm = _tile(math.gcd(n, row0) if row0 else n)
    blk0 = row0 // tm
    cols = tuple(o[0] for o in outs)
    return pl.pallas_call(
        functools.partial(_proj_kernel, cols=cols, precise=precise),
        grid=(n // tm,),
        in_specs=[pl.BlockSpec((tm, k), lambda i: (i + blk0, 0)),
                  pl.BlockSpec(w.shape, lambda i: (0, 0))],
        out_specs=[pl.BlockSpec((tm, o[1]), lambda i: (i, 0)) for o in outs],
        out_shape=[jax.ShapeDtypeStruct((n, o[1]), o[2]) for o in outs],
        compiler_params=_params("parallel"),
        name="in_proj",
    )(x, w)


def _proj_kv_kernel(x_ref, w_ref, qs_ref, qd_ref, ks_ref, vs_ref, kd_ref, vd_ref,
                    ksf_hbm, vsf_hbm, kdf_hbm, vdf_hbm, stage, sem, *, rows, nq, lt):
    b, q = pl.program_id(0), pl.program_id(1)
    step = b * nq + q
    slot = step % 2
    caches = (ksf_hbm, vsf_hbm, kdf_hbm, vdf_hbm)
    xb = x_ref[...].astype(BF16)
    col = lambda c: jnp.dot(xb, w_ref[:, c * W_HALF:(c + 1) * W_HALF], preferred_element_type=F32)
    qs_ref[...] = col(0).astype(BF16)
    qd_ref[...] = col(3).astype(BF16)
    for a, (c, o_ref) in enumerate(zip((1, 2, 4, 5), (ks_ref, vs_ref, kd_ref, vd_ref))):
        y = col(c)
        o_ref[...] = y.astype(BF16)
        stage[slot, a] = y

    def copy(slot_, a, b_, q_, first):
        if first:
            return pltpu.make_async_copy(stage.at[slot_, a, pl.ds(PAD, rows - PAD)],
                                         caches[a].at[pl.ds(pl.multiple_of(b_ * lt, SUBLANES), rows - PAD)], sem)
        dst = pl.multiple_of(b_ * lt + q_ * rows - PAD, SUBLANES)
        return pltpu.make_async_copy(stage.at[slot_, a], caches[a].at[pl.ds(dst, rows)], sem)

    def start_all(first):
        for a in range(len(caches)):
            copy(slot, a, b, q, first).start()

    def wait_all(first):
        for a in range(len(caches)):
            copy(0, a, 0, 1, first).wait()

    @pl.when((step > 0) & (q == 1))
    def _():
        wait_all(True)

    @pl.when((step > 0) & (q != 1))
    def _():
        wait_all(False)

    @pl.when(q == 0)
    def _():
        start_all(True)

    @pl.when(q != 0)
    def _():
        start_all(False)

    @pl.when(step == pl.num_programs(0) * nq - 1)
    def _():
        wait_all(False)


def _project_prompt_kv(x, w, bsz, lp):
    nq = next(c for c in (4, 3, 2) if lp % c == 0 and (lp // c) % SUBLANES == 0 and lp // c > PAD)
    rows = lp // nq
    lt = lp - PAD
    blk = lambda wd: pl.BlockSpec((rows, wd), lambda i, j: (i * nq + j, 0))
    pad_out = jax.ShapeDtypeStruct((bsz * lp, W_HALF), BF16)
    cache_out = jax.ShapeDtypeStruct((bsz * lt, W_HALF), F32)
    return pl.pallas_call(
        functools.partial(_proj_kv_kernel, rows=rows, nq=nq, lt=lt),
        grid=(bsz, nq),
        in_specs=[blk(x.shape[1]), pl.BlockSpec(w.shape, lambda i, j: (0, 0))],
        out_specs=[blk(W_HALF)] * 6 + [pl.BlockSpec(memory_space=pl.ANY)] * 4,
        out_shape=[pad_out] * 6 + [cache_out] * 4,
        scratch_shapes=[pltpu.VMEM((2, 4, rows, W_HALF), F32), pltpu.SemaphoreType.DMA(())],
        compiler_params=_params("arbitrary", "arbitrary"),
        name="in_proj_kv",
    )(x, w)


def _s5_kernel(u_ref, x0r_ref, x0i_ref, lr_ref, li_ref, bre_ref, bim_ref, cre_ref, cim_ref, d_ref,
               wglu_ref, y_ref, fr_ref, fi_ref, xr_s, xi_s, sr_s, si_s, *, tc, sg, precise):
    c = pl.program_id(1)

    @pl.when(c == 0)
    def _():
        xr_s[...] = x0r_ref[...]
        xi_s[...] = x0i_ref[...]

    def mm(a, b):
        if precise:
            return jnp.dot(a, b, precision=HIGHEST, preferred_element_type=F32)
        return jnp.dot(a.astype(BF16), b, preferred_element_type=F32)

    cw, sw = W_HALF // S5_BLOCKS, S5_WIDTH // S5_BLOCKS
    for s in range(sg):
        u = u_ref[s]
        sr_s[s] = jnp.concatenate(
            [mm(u[:, b * cw:(b + 1) * cw], bre_ref[b]) for b in range(S5_BLOCKS)], axis=1)
        si_s[s] = jnp.concatenate(
            [mm(u[:, b * cw:(b + 1) * cw], bim_ref[b]) for b in range(S5_BLOCKS)], axis=1)
    lr = lr_ref[...]
    li = li_ref[...]

    def step(t, carry):
        out = []
        for s, (xr, xi) in enumerate(carry):
            nr = lr * xr - li * xi + sr_s[s, pl.ds(t, 1), :]
            ni = lr * xi + li * xr + si_s[s, pl.ds(t, 1), :]
            sr_s[s, pl.ds(t, 1), :] = nr
            si_s[s, pl.ds(t, 1), :] = ni
            out.append((nr, ni))
        return tuple(out)

    init = tuple((xr_s[pl.ds(s, 1), :], xi_s[pl.ds(s, 1), :]) for s in range(sg))
    last = lax.fori_loop(0, tc, step, init, unroll=2)
    xr = jnp.concatenate([x[0] for x in last], axis=0)
    xi = jnp.concatenate([x[1] for x in last], axis=0)
    xr_s[...] = xr
    xi_s[...] = xi
    for s in range(sg):
        y = jnp.concatenate(
            [mm(sr_s[s, :, b * sw:(b + 1) * sw], cre_ref[b]) - mm(si_s[s, :, b * sw:(b + 1) * sw], cim_ref[b])
             for b in range(S5_BLOCKS)], axis=1) + d_ref[...] * u_ref[s]
        y = jax.nn.gelu(y)
        y = y * jax.nn.sigmoid(mm(y, wglu_ref[...]))
        y_ref[s] = y.astype(y_ref.dtype)

    @pl.when(c == pl.num_programs(1) - 1)
    def _():
        fr_ref[...] = xr
        fi_ref[...] = xi


def _s5_mixer(u, x0r, x0i, p, tc, precise):
    b, l, _ = u.shape
    nc = l // tc
    sg = math.gcd(b, SUBLANES)
    wdt = F32 if precise else BF16
    full = lambda a: pl.BlockSpec(a.shape, lambda i, j: (0,) * a.ndim)
    st_spec = pl.BlockSpec((sg, S5_WIDTH), lambda i, j: (i, 0))
    seq_spec = pl.BlockSpec((sg, tc, W_HALF), lambda i, j: (i, j, 0))
    mats = [p["bre"].astype(wdt), p["bim"].astype(wdt), p["cre"].astype(wdt), p["cim"].astype(wdt)]
    wglu = p["wglu"].astype(wdt)
    return pl.pallas_call(
        functools.partial(_s5_kernel, tc=tc, sg=sg, precise=precise),
        grid=(b // sg, nc),
        in_specs=[seq_spec, st_spec, st_spec,
                  full(p["lr"]), full(p["li"])] + [full(m) for m in mats] + [full(p["d"]), full(wglu)],
        out_specs=[seq_spec, st_spec, st_spec],
        out_shape=[jax.ShapeDtypeStruct((b, l, W_HALF), BF16),
                   jax.ShapeDtypeStruct((b, S5_WIDTH), F32),
                   jax.ShapeDtypeStruct((b, S5_WIDTH), F32)],
        scratch_shapes=[pltpu.VMEM((sg, S5_WIDTH), F32), pltpu.VMEM((sg, S5_WIDTH), F32),
                        pltpu.VMEM((sg, tc, S5_WIDTH), F32), pltpu.VMEM((sg, tc, S5_WIDTH), F32)],
        compiler_params=_params("parallel", "arbitrary"),
        name="s5_mixer",
    )(u, x0r, x0i, p["lr"], p["li"], *mats, p["d"], wglu)


def _s5_prepare(lam_re, lam_im, log_dt, b_re, b_im, c_re, c_im, d, w_glu):
    dt = jnp.exp(log_dt)[:, None]
    mag = jnp.exp(lam_re * dt)
    lb_re = mag * jnp.cos(lam_im * dt)
    lb_im = mag * jnp.sin(lam_im * dt)
    nr, ni = lb_re - 1.0, lb_im
    den = jnp.square(lam_re) + jnp.square(lam_im)
    cr = (nr * lam_re + ni * lam_im) / den
    ci = (ni * lam_re - nr * lam_im) / den
    bb_re = cr[..., None] * b_re - ci[..., None] * b_im
    bb_im = cr[..., None] * b_im + ci[..., None] * b_re
    gb = S5_GROUPS // S5_BLOCKS
    eye = jnp.eye(gb, dtype=F32)
    per_blk = lambda m: m.reshape((S5_BLOCKS, gb) + m.shape[1:])
    blk_in = lambda m: jnp.einsum("bgpi,gh->bgihp", per_blk(m), eye).reshape(
        S5_BLOCKS, W_HALF // S5_BLOCKS, S5_WIDTH // S5_BLOCKS)
    blk_out = lambda m: jnp.einsum("bgop,gh->bgpho", per_blk(m), eye).reshape(
        S5_BLOCKS, S5_WIDTH // S5_BLOCKS, W_HALF // S5_BLOCKS)
    return dict(lr=lb_re.reshape(1, S5_WIDTH), li=lb_im.reshape(1, S5_WIDTH),
                bre=blk_in(bb_re), bim=blk_in(bb_im), cre=blk_out(c_re), cim=blk_out(c_im),
                d=d.reshape(1, W_HALF), wglu=w_glu)


def _ret_kernel(q_ref, k_ref, v_ref, g_ref, cos_ref, sin_ref, dmat_ref, inter_ref, zeta_ref, gc_ref,
                r0_ref, y_ref, rf_ref, r_s, *, tc):
    c = pl.program_id(1)

    @pl.when(c == 0)
    def _():
        r_s[...] = r0_ref[0]

    lane = lax.broadcasted_iota(I32, (tc, W_HALF), 1)
    first_half = (lane & (RET_DK - 1)) < (RET_DK // 2)
    cos = cos_ref[...]
    sin = sin_ref[...]

    def rot(x):
        ahead = pltpu.roll(x, W_HALF - RET_DK // 2, 1)
        behind = pltpu.roll(x, RET_DK // 2, 1)
        return x * cos + jnp.where(first_half, ahead, behind) * sin

    q = rot(q_ref[...])
    k = rot(k_ref[...]) * (RET_DK ** -0.5)
    v = v_ref[...]
    vz = v * zeta_ref[...]
    g = g_ref[...]
    gate = g * jax.nn.sigmoid(g)
    heads = [slice(h * RET_DK, (h + 1) * RET_DK) for h in range(RET_HEADS)]
    qb = q.astype(BF16)
    kb = k.astype(BF16)
    vb = v.astype(BF16)
    vzb = vz.astype(BF16)
    kt = k.T.astype(BF16)
    scores = [lax.dot_general(qb[:, sl], kb[:, sl], NT_DIMS, preferred_element_type=F32) for sl in heads]
    cross = [jnp.dot(qb[:, sl], r_s[h].astype(BF16), preferred_element_type=F32)
             for h, sl in enumerate(heads)]
    new_r = [gc_ref[h] * r_s[h] + jnp.dot(kt[sl, :], vzb[:, sl], preferred_element_type=F32)
             for h, sl in enumerate(heads)]
    decayed = [(scores[h] * dmat_ref[h]).astype(BF16) for h in range(RET_HEADS)]
    outs = [jnp.dot(decayed[h], vb[:, sl], preferred_element_type=F32) + cross[h] * inter_ref[:, sl]
            for h, sl in enumerate(heads)]
    for h, sl in enumerate(heads):
        r_s[h] = new_r[h]
        o = outs[h]
        mu = jnp.mean(o, -1, keepdims=True)
        dlt = o - mu
        var = jnp.mean(dlt * dlt, -1, keepdims=True)
        y_ref[:, sl] = (dlt * lax.rsqrt(var + EPS) * gate[:, sl]).astype(y_ref.dtype)

    @pl.when(c == pl.num_programs(1) - 1)
    def _():
        rf_ref[0] = r_s[...]


def _ret_mixer(q, k, v, g, r0, pos, tc):
    b, l, _ = q.shape
    nc = l // tc
    inv = 1.0 / (10000.0 ** jnp.linspace(0.0, 1.0, RET_DK // 2, dtype=F32))
    ang = pos.astype(F32)[:, None] * inv[None]
    cos = jnp.tile(jnp.concatenate([jnp.cos(ang), jnp.cos(ang)], 1), (1, RET_HEADS))
    sin = jnp.tile(jnp.concatenate([-jnp.sin(ang), jnp.sin(ang)], 1), (1, RET_HEADS))
    log_gamma = jnp.log(1.0 - 2.0 ** (-5.0 - jnp.arange(RET_HEADS, dtype=F32)))
    n = jnp.arange(tc, dtype=F32)
    diff = n[:, None] - n[None, :]
    causal = diff >= 0
    dmat = jnp.where(causal[None], jnp.exp(jnp.where(causal, diff, 0.0)[None] * log_gamma[:, None, None]), 0.0)
    per_lane = lambda m: jnp.repeat(m, RET_DK, axis=1)
    inter = per_lane(jnp.exp((n[:, None] + 1.0) * log_gamma[None]))
    zeta = per_lane(jnp.exp((tc - 1.0 - n)[:, None] * log_gamma[None]))
    gc = jnp.broadcast_to(jnp.exp(tc * log_gamma)[:, None, None], (RET_HEADS, RET_DK, RET_DK))
    seq = pl.BlockSpec((None, tc, W_HALF), lambda i, j: (i, j, 0))
    tab = pl.BlockSpec((tc, W_HALF), lambda i, j: (j, 0))
    full = lambda a: pl.BlockSpec(a.shape, lambda i, j: (0,) * a.ndim)
    st = pl.BlockSpec((1, RET_HEADS, RET_DK, RET_DK), lambda i, j: (i, 0, 0, 0))
    return pl.pallas_call(
        functools.partial(_ret_kernel, tc=tc),
        grid=(b, nc),
        in_specs=[seq, seq, seq, seq, tab, tab, full(dmat), full(inter), full(zeta), full(gc), st],
        out_specs=[seq, st],
        out_shape=[jax.ShapeDtypeStruct((b, l, W_HALF), BF16),
                   jax.ShapeDtypeStruct((b, RET_HEADS, RET_DK, RET_DK), F32)],
        scratch_shapes=[pltpu.VMEM((RET_HEADS, RET_DK, RET_DK), F32)],
        compiler_params=_params("parallel", "arbitrary"),
        name="ret_mixer",
    )(q, k, v, g, cos, sin, dmat, inter, zeta, gc, r0)


def _attn_kernel(lq1_ref, lk1_ref, lq2_ref, lk2_ref, ng_ref, qs_ref, qd_ref, ks_ref, vst_ref, kd_ref,
                 vdt_ref, osb_ref, od_ref, *, bq, q_base, kv_lo, n_key_blocks, lam_init):
    qi = pl.program_id(1)
    q0 = q_base + qi * bq
    nkb = jnp.minimum(jnp.right_shift(q0 + bq + KEY_BLOCK - 1, KEY_SHIFT), n_key_blocks)
    qpos = q0 + lax.broadcasted_iota(I32, (KEY_BLOCK, bq), 1)
    koff = lax.broadcasted_iota(I32, (KEY_BLOCK, bq), 0)
    qchunk = jnp.right_shift(qpos, CHUNK_SHIFT)
    row = lax.broadcasted_iota(I32, (KEY_BLOCK, KEY_BLOCK), 0)
    col = lax.broadcasted_iota(I32, (KEY_BLOCK, KEY_BLOCK), 1)
    newer_keys = (col > row).astype(BF16)

    qs = (qs_ref[...].astype(F32) * (SB_DH ** -0.5)).astype(BF16)
    qd = (qd_ref[...].astype(F32) * (DIFF_DH ** -0.5)).astype(BF16)
    sb_q = [qs[:, h * SB_DH:(h + 1) * SB_DH] for h in range(SB_HEADS)]
    d_q = [qd[:, c * DIFF_DH:(c + 1) * DIFF_DH] for c in range(2 * DIFF_HEADS)]

    def body(t, carry, edge, nblk, j0):
        sb_acc, sb_tail, d_m, d_l, d_a = carry
        blocks = range(nblk)
        js = [j0 + nblk * t + i for i in blocks]
        kbs = [nkb - 1 - j for j in js]
        kblks = [ks_ref[kb] for kb in kbs]
        dblks = [kd_ref[j] for j in js]
        if edge:
            masks, dmasks = [], []
            for i in blocks:
                kpos = kbs[i] * KEY_BLOCK + koff
                masks.append((kpos >= kv_lo) & (kpos < qpos))
                dpos = js[i] * KEY_BLOCK + koff
                dmasks.append((dpos >= kv_lo) & (jnp.right_shift(dpos, CHUNK_SHIFT) <= qchunk))
            keep = lambda m, x, other: jnp.where(m, x, other)
        else:
            masks = dmasks = [None] * nblk
            keep = lambda m, x, other: x
        zs = [[lax.dot_general(kblks[i][:, h * SB_DH:(h + 1) * SB_DH], sb_q[h], NT_DIMS,
                               preferred_element_type=F32) for h in range(SB_HEADS)] for i in blocks]
        ss = [[lax.dot_general(dblks[i][:, c * DIFF_DH:(c + 1) * DIFF_DH], d_q[c], NT_DIMS,
                               preferred_element_type=F32) for c in range(2 * DIFF_HEADS)] for i in blocks]
        log_betas = [[None] * SB_HEADS for _ in blocks]
        splits = [[None] * SB_HEADS for _ in blocks]
        tails = [[None] * SB_HEADS for _ in blocks]
        new_tail = []
        for h in range(SB_HEADS):
            tail = sb_tail[h]
            for i in blocks:
                z = zs[i][h]
                sp = jnp.log(1.0 + jnp.exp(-jnp.abs(z)))
                cost = keep(masks[i], jnp.maximum(z, 0.0) + sp, 0.0)
                log_betas[i][h] = jnp.minimum(z, 0.0) - sp
                splits[i][h] = cost.astype(BF16)
                tails[i][h] = tail
                tail = tail + jnp.sum(cost, axis=0, keepdims=True)
            new_tail.append(tail)
        new_m, new_l, alphas = [], [], []
        probs = [[None] * (2 * DIFF_HEADS) for _ in blocks]
        for c in range(2 * DIFF_HEADS):
            s = [keep(dmasks[i], ss[i][c], NEG_BIG) for i in blocks]
            mn = d_m[c]
            for i in blocks:
                mn = jnp.maximum(mn, jnp.max(s[i], axis=0, keepdims=True))
            al = jnp.exp(d_m[c] - mn)
            l = al * d_l[c]
            for i in blocks:
                p = jnp.exp(s[i] - mn)
                l = l + jnp.sum(p, axis=0, keepdims=True)
                probs[i][c] = p.astype(BF16)
            new_m.append(mn)
            new_l.append(l)
            alphas.append(al)
        newer = [[jnp.dot(newer_keys, splits[i][h], preferred_element_type=F32)
                  for h in range(SB_HEADS)] for i in blocks]
        new_a = []
        for c in range(2 * DIFF_HEADS):
            a = alphas[c] * d_a[c]
            vsl = slice((c // 2) * 2 * DIFF_DH, (c // 2 + 1) * 2 * DIFF_DH)
            for i in blocks:
                a = a + jnp.dot(vdt_ref[js[i], vsl, :], probs[i][c], preferred_element_type=F32)
            new_a.append(a)
        ws = [[keep(masks[i], jnp.exp(log_betas[i][h] - newer[i][h] - tails[i][h]), 0.0).astype(BF16)
               for h in range(SB_HEADS)] for i in blocks]
        new_acc = []
        for h in range(SB_HEADS):
            acc = sb_acc[h]
            for i in blocks:
                acc = acc + jnp.dot(vst_ref[kbs[i], h * SB_DH:(h + 1) * SB_DH, :], ws[i][h],
                                    preferred_element_type=F32)
            new_acc.append(acc)
        return tuple(new_acc), tuple(new_tail), tuple(new_m), tuple(new_l), tuple(new_a)

    zrow = jnp.zeros((1, bq), F32)
    init = (tuple(jnp.zeros((SB_DH, bq), F32) for _ in range(SB_HEADS)),
            tuple(zrow for _ in range(SB_HEADS)),
            tuple(jnp.full((1, bq), NEG_BIG, F32) for _ in range(2 * DIFF_HEADS)),
            tuple(zrow for _ in range(2 * DIFF_HEADS)),
            tuple(jnp.zeros((2 * DIFF_DH, bq), F32) for _ in range(2 * DIFF_HEADS)))
    edge_body = functools.partial(body, edge=True, nblk=1, j0=0)
    n_pairs = jnp.right_shift(jnp.maximum(nkb - 2, 0), 1)
    carry = lax.fori_loop(0, jnp.minimum(nkb, 1), edge_body, init)
    carry = lax.fori_loop(0, n_pairs, functools.partial(body, edge=False, nblk=2, j0=1), carry)
    carry = lax.fori_loop(1 + 2 * n_pairs, nkb - 1, functools.partial(body, edge=False, nblk=1, j0=0), carry)
    sb_acc, _, _, d_l, d_a = lax.fori_loop(jnp.maximum(nkb - 1, 1), nkb, edge_body, carry)

    osb_ref[...] = jnp.concatenate(sb_acc, axis=0).T.astype(osb_ref.dtype)
    lam = (jnp.exp(jnp.sum(lq1_ref[...] * lk1_ref[...], axis=1, keepdims=True))
           - jnp.exp(jnp.sum(lq2_ref[...] * lk2_ref[...], axis=1, keepdims=True)) + lam_init)
    outs = []
    for h in range(DIFF_HEADS):
        o = d_a[2 * h] / d_l[2 * h] - lam * (d_a[2 * h + 1] / d_l[2 * h + 1])
        o = o * lax.rsqrt(jnp.mean(o * o, axis=0, keepdims=True) + EPS) * ng_ref[...]
        outs.append(o * (1.0 - lam_init))
    od_ref[...] = jnp.concatenate(outs, axis=0).T.astype(od_ref.dtype)


def _attention(qs, qd, ks, vs, kd, vd, lam_vecs, norm_g, *, bq, q_base, kv_lo, lam_init):
    b, lq, _ = qs.shape
    lk = ks.shape[1]
    nkb = lk // KEY_BLOCK
    assert bq == KEY_BLOCK and q_base % KEY_BLOCK == 0 and 0 <= kv_lo < KEY_BLOCK
    kblocks = lambda a: a.reshape(b, nkb, KEY_BLOCK, W_HALF)
    vblocks_t = lambda a: a.reshape(b, nkb, KEY_BLOCK, W_HALF).transpose(0, 1, 3, 2)
    qspec = pl.BlockSpec((None, bq, W_HALF), lambda i, j: (i, j, 0))
    kspec = pl.BlockSpec((None, nkb, KEY_BLOCK, W_HALF), lambda i, j: (i, 0, 0, 0))
    vspec = pl.BlockSpec((None, nkb, W_HALF, KEY_BLOCK), lambda i, j: (i, 0, 0, 0))
    vec = pl.BlockSpec((1, DIFF_DH), lambda i, j: (0, 0))
    return pl.pallas_call(
        functools.partial(_attn_kernel, bq=bq, q_base=q_base, kv_lo=kv_lo,
                          n_key_blocks=nkb, lam_init=lam_init),
        grid=(b, lq // bq),
        in_specs=[vec, vec, vec, vec, pl.BlockSpec((2 * DIFF_DH, 1), lambda i, j: (0, 0)),
                  qspec, qspec, kspec, vspec, kspec, vspec],
        out_specs=[qspec, qspec],
        out_shape=[jax.ShapeDtypeStruct((b, lq, W_HALF), BF16), jax.ShapeDtypeStruct((b, lq, W_HALF), BF16)],
        compiler_params=_params("parallel", "arbitrary"),
        name="sb_diff_attention",
    )(*lam_vecs, norm_g, qs, qd, kblocks(ks), vblocks_t(vs), kblocks(kd), vblocks_t(vd))


def _outproj_ln_kernel(yap_ref, ybp_ref, yas_ref, ybs_ref, h_ref, w_ref, g_ref, b_ref, o_ref, *,
                       alpha, prompt_blocks):
    is_prompt = pl.program_id(0) < prompt_blocks
    ya = jnp.where(is_prompt, yap_ref[...], yas_ref[...])
    yb = jnp.where(is_prompt, ybp_ref[...], ybs_ref[...])
    m = (jnp.dot(ya, w_ref[:W_HALF, :], preferred_element_type=F32)
         + jnp.dot(yb, w_ref[W_HALF:, :], preferred_element_type=F32))
    o_ref[...] = _layer_norm(alpha * h_ref[...] + m, g_ref[...], b_ref[...])


def _outproj_ln(ya_p, yb_p, ya_s, yb_s, h, w, g, b, alpha):
    n = h.shape[0]
    n_p, n_s = ya_p.shape[0], ya_s.shape[0]
    tm = _tile(math.gcd(n_p, n_s))
    pb = n_p // tm
    row = pl.BlockSpec((tm, D_MODEL), lambda i: (i, 0))
    prompt = pl.BlockSpec((tm, W_HALF), lambda i: (jnp.minimum(i, pb - 1), 0))
    sample = pl.BlockSpec((tm, W_HALF), lambda i: (jnp.maximum(i - pb, 0), 0))
    full = lambda a: pl.BlockSpec(a.shape, lambda i: (0, 0))
    return pl.pallas_call(
        functools.partial(_outproj_ln_kernel, alpha=alpha, prompt_blocks=pb),
        grid=(n // tm,),
        in_specs=[prompt, prompt, sample, sample, row, full(w), full(g), full(b)],
        out_specs=row,
        out_shape=jax.ShapeDtypeStruct((n, D_MODEL), F32),
        compiler_params=_params("parallel"),
        name="out_proj_ln",
    )(ya_p, yb_p, ya_s, yb_s, h, w, g, b)


def _router_kernel(x_ref, wt_ref, bias_ref, eidx_ref, gate_ref, rank_ref, cnt_ref, run_s, *, tm):
    i = pl.program_id(0)

    @pl.when(i == 0)
    def _():
        run_s[...] = jnp.zeros_like(run_s)

    logits = lax.dot_general(wt_ref[...], x_ref[...], NT_DIMS, precision=HIGHEST,
                             preferred_element_type=F32)
    scores = jax.nn.sigmoid(logits)
    sel = scores + bias_ref[...]
    ninf = -jnp.inf

    sel3 = sel.reshape(N_GROUPS, GROUP_SIZE, tm)
    within = lax.broadcasted_iota(I32, (N_GROUPS, GROUP_SIZE, tm), 1)
    m1 = jnp.max(sel3, axis=1, keepdims=True)
    first = jnp.min(jnp.where(sel3 == m1, within, GROUP_SIZE), axis=1, keepdims=True)
    m2 = jnp.max(jnp.where(within == first, ninf, sel3), axis=1, keepdims=True)
    gscore = (m1 + m2).reshape(N_GROUPS, tm)

    gid = lax.broadcasted_iota(I32, (N_GROUPS, tm), 0)
    gsel = jnp.zeros((N_GROUPS, tm), jnp.bool_)
    cur = gscore
    for _ in range(TOPK_GROUPS):
        m = jnp.max(cur, axis=0, keepdims=True)
        f = jnp.min(jnp.where(cur == m, gid, N_GROUPS), axis=0, keepdims=True)
        pick = gid == f
        gsel = gsel | pick
        cur = jnp.where(pick, ninf, cur)
    emask = jnp.broadcast_to(gsel.reshape(N_GROUPS, 1, tm), (N_GROUPS, GROUP_SIZE, tm)).reshape(N_EXPERTS, tm)

    eid = lax.broadcasted_iota(I32, (N_EXPERTS, tm), 0)
    cur = jnp.where(emask, sel, ninf)
    picks, idxs, gates = [], [], []
    for _ in range(TOP_K):
        m = jnp.max(cur, axis=0, keepdims=True)
        f = jnp.min(jnp.where(cur == m, eid, N_EXPERTS), axis=0, keepdims=True)
        pick = eid == f
        picks.append(pick)
        idxs.append(f)
        gates.append(jnp.sum(jnp.where(pick, scores, 0.0), axis=0, keepdims=True))
        cur = jnp.where(pick, ninf, cur)
    gsum = gates[0]
    for gk in gates[1:]:
        gsum = gsum + gk

    chosen = picks[0]
    for pk in picks[1:]:
        chosen = chosen | pk
    chosen_f = chosen.astype(F32)
    r_ = lax.broadcasted_iota(I32, (tm, tm), 0)
    c_ = lax.broadcasted_iota(I32, (tm, tm), 1)
    earlier = (r_ < c_).astype(BF16)
    before = run_s[...] + jnp.dot(chosen_f.astype(BF16), earlier, preferred_element_type=F32)
    zrow_i = jnp.zeros((1, tm), I32)
    zrow_f = jnp.zeros((1, tm), F32)
    ranks = [jnp.sum(jnp.where(pk, before, 0.0), axis=0, keepdims=True).astype(I32) for pk in picks]
    eidx_ref[...] = jnp.concatenate(idxs + [zrow_i, zrow_i], axis=0)
    gate_ref[...] = jnp.concatenate([gk / gsum * ROUTED_SCALE for gk in gates] + [zrow_f, zrow_f], axis=0)
    rank_ref[...] = jnp.concatenate(ranks + [zrow_i, zrow_i], axis=0)
    run_s[...] = run_s[...] + jnp.sum(chosen_f, axis=1, keepdims=True)
    cnt_ref[...] = jnp.broadcast_to(run_s[...], cnt_ref.shape)


def _router(x, wt, bias):
    n = x.shape[0]
    tm = _tile(n, (512, 256, 128))
    lane_blk = pl.BlockSpec((8, tm), lambda i: (0, i))
    return pl.pallas_call(
        functools.partial(_router_kernel, tm=tm),
        grid=(n // tm,),
        in_specs=[pl.BlockSpec((tm, D_MODEL), lambda i: (i, 0)),
                  pl.BlockSpec(wt.shape, lambda i: (0, 0)),
                  pl.BlockSpec(bias.shape, lambda i: (0, 0))],
        out_specs=[lane_blk, lane_blk, lane_blk, pl.BlockSpec((N_EXPERTS, 128), lambda i: (0, 0))],
        out_shape=[jax.ShapeDtypeStruct((8, n), I32), jax.ShapeDtypeStruct((8, n), F32),
                   jax.ShapeDtypeStruct((8, n), I32), jax.ShapeDtypeStruct((N_EXPERTS, 128), F32)],
        scratch_shapes=[pltpu.VMEM((N_EXPERTS, 1), F32)],
        compiler_params=_params("arbitrary"),
        name="moe_router",
    )(x, wt, bias)


ROW_SUB = D_MODEL // 2 // LANES
HIGH_HALF = -65536
DMA_PRIORITIES = 2


def _pack_rows(dst_ref, x, r0=0):
    rows = x.shape[0]
    bits = lambda v: lax.bitcast_convert_type(v.astype(BF16).astype(F32), I32)
    w = (lax.shift_right_logical(bits(x[:, :D_MODEL // 2]), jnp.full((), 16, I32))
         | (bits(x[:, D_MODEL // 2:]) & HIGH_HALF))
    for c in range(ROW_SUB):
        dst_ref[pl.ds(r0 * ROW_SUB + c, rows, stride=ROW_SUB), :] = w[:, c * LANES:(c + 1) * LANES]


def _unpack_rows(src_ref, rows, r0=0):
    ws = [src_ref[pl.ds(r0 * ROW_SUB + c, rows, stride=ROW_SUB), :] for c in range(ROW_SUB)]
    lo = [lax.bitcast_convert_type(lax.shift_left(w, jnp.full((), 16, I32)), F32) for w in ws]
    hi = [lax.bitcast_convert_type(w & HIGH_HALF, F32) for w in ws]
    return jnp.concatenate(lo + hi, axis=1)


def _dispatch_kernel(*refs, tm):
    n_idx = TOP_K * tm // LANES
    dest_refs, (x_ref, o_hbm, buf, sem) = refs[:n_idx], refs[n_idx:]
    i = pl.program_id(0)
    slot = i % 2
    stage = buf.at[slot]
    _pack_rows(stage, x_ref[...])

    def wait_all():
        for _ in range(TOP_K):
            pltpu.make_async_copy(stage, o_hbm.at[pl.ds(0, tm * ROW_SUB)], sem).wait()

    @pl.when(i > 0)
    def _():
        wait_all()

    for k in range(TOP_K):
        for blk in range(tm // LANES):
            def start(t, carry, k=k, blk=blk):
                for prio in range(DMA_PRIORITIES):
                    jj = t * DMA_PRIORITIES + prio
                    d = pl.multiple_of(dest_refs[k * (tm // LANES) + blk][0, 0, jj] * ROW_SUB, ROW_SUB)
                    s = pl.multiple_of((blk * LANES + jj) * ROW_SUB, ROW_SUB)
                    pltpu.make_async_copy(stage.at[pl.ds(s, ROW_SUB)], o_hbm.at[pl.ds(d, ROW_SUB)],
                                          sem).start(priority=prio)
                return carry
            lax.fori_loop(0, LANES // DMA_PRIORITIES, start, 0, unroll=4)

    @pl.when(i == pl.num_programs(0) - 1)
    def _():
        wait_all()


def _slot_spec():
    return pl.BlockSpec((1, 1, LANES), lambda i: (i, 0, 0), memory_space=pltpu.SMEM)


def _dispatch(x, dests, n_rows, tm):
    n = x.shape[0]
    return pl.pallas_call(
        functools.partial(_dispatch_kernel, tm=tm),
        grid=(n // tm,),
        in_specs=[_slot_spec() for _ in dests] + [pl.BlockSpec((tm, D_MODEL), lambda i: (i, 0))],
        out_specs=pl.BlockSpec(memory_space=pl.ANY),
        out_shape=jax.ShapeDtypeStruct((n_rows * ROW_SUB, LANES), I32),
        scratch_shapes=[pltpu.VMEM((2, tm * ROW_SUB, LANES), I32), pltpu.SemaphoreType.DMA(())],
        compiler_params=_params("arbitrary"),
        name="moe_dispatch",
    )(*dests, x)


def _ffn_kernel(be_ref, nv_ref, xs_ref, w1_ref, w3_ref, w2_ref, o_ref):
    i = pl.program_id(0)
    x = _unpack_rows(xs_ref, EXPERT_BLOCK)
    live = lax.broadcasted_iota(I32, (EXPERT_BLOCK, 1), 0) < nv_ref[i]
    x = jnp.where(live, x, 0.0).astype(BF16)
    a = jnp.dot(x, w1_ref[0], preferred_element_type=F32)
    b = jnp.dot(x, w3_ref[0], preferred_element_type=F32)
    hdn = (a * jax.nn.sigmoid(a) * b).astype(BF16)
    _pack_rows(o_ref, jnp.dot(hdn, w2_ref[0], preferred_element_type=F32))


def _expert_ffn(xs, blk_e, blk_valid, w1, w3, w2):
    nb = xs.shape[0] // (EXPERT_BLOCK * ROW_SUB)
    de = w1.shape[-1]
    rows = pl.BlockSpec((EXPERT_BLOCK * ROW_SUB, LANES), lambda i, be, nv: (i, 0))
    grid_spec = pltpu.PrefetchScalarGridSpec(
        num_scalar_prefetch=2,
        grid=(nb,),
        in_specs=[rows,
                  pl.BlockSpec((1, D_MODEL, de), lambda i, be, nv: (be[i], 0, 0)),
                  pl.BlockSpec((1, D_MODEL, de), lambda i, be, nv: (be[i], 0, 0)),
                  pl.BlockSpec((1, de, D_MODEL), lambda i, be, nv: (be[i], 0, 0))],
        out_specs=rows,
    )
    return pl.pallas_call(
        _ffn_kernel,
        grid_spec=grid_spec,
        out_shape=jax.ShapeDtypeStruct(xs.shape, I32),
        compiler_params=_params("arbitrary"),
        name="moe_expert_ffn",
    )(blk_e, blk_valid, xs, w1, w3, w2)


def _combine_kernel(*refs, tm, alpha):
    n_idx = TOP_K * tm // LANES
    dest_refs = refs[:n_idx]
    gate_ref, x_ref, ys_hbm, sw1_ref, sw3_ref, sw2_ref, g_ref, b_ref, o_ref, buf, sem = refs[n_idx:]
    for k in range(TOP_K):
        for blk in range(tm // LANES):
            def start(t, carry, k=k, blk=blk):
                for prio in range(DMA_PRIORITIES):
                    jj = t * DMA_PRIORITIES + prio
                    s = pl.multiple_of(dest_refs[k * (tm // LANES) + blk][0, 0, jj] * ROW_SUB, ROW_SUB)
                    d = pl.multiple_of((k * tm + blk * LANES + jj) * ROW_SUB, ROW_SUB)
                    pltpu.make_async_copy(ys_hbm.at[pl.ds(s, ROW_SUB)], buf.at[pl.ds(d, ROW_SUB)],
                                          sem).start(priority=prio)
                return carry
            lax.fori_loop(0, LANES // DMA_PRIORITIES, start, 0, unroll=4)

    x = x_ref[...]
    xb = x.astype(BF16)
    a = jnp.dot(xb, sw1_ref[...], preferred_element_type=F32)
    b = jnp.dot(xb, sw3_ref[...], preferred_element_type=F32)
    shared = jnp.dot((a * jax.nn.sigmoid(a) * b).astype(BF16), sw2_ref[...], preferred_element_type=F32)

    for k in range(TOP_K):
        grp = pl.ds(k * tm * ROW_SUB, tm * ROW_SUB)
        pltpu.make_async_copy(ys_hbm.at[pl.ds(0, tm * ROW_SUB)], buf.at[grp], sem).wait()

    gate = gate_ref[...]
    routed = gate[:, 0:1] * _unpack_rows(buf, tm, 0)
    for k in range(1, TOP_K):
        routed = routed + gate[:, k:k + 1] * _unpack_rows(buf, tm, k * tm)
    o_ref[...] = _layer_norm(alpha * x + routed + shared, g_ref[...], b_ref[...])


def _combine(dests, gate_t, x, ys, sw1, sw3, sw2, g, b, alpha, tm):
    n = x.shape[0]
    full = lambda a: pl.BlockSpec(a.shape, lambda i: (0,) * a.ndim)
    return pl.pallas_call(
        functools.partial(_combine_kernel, tm=tm, alpha=alpha),
        grid=(n // tm,),
        in_specs=[_slot_spec() for _ in dests] + [
                  pl.BlockSpec((tm, 8), lambda i: (i, 0)),
                  pl.BlockSpec((tm, D_MODEL), lambda i: (i, 0)),
                  pl.BlockSpec(memory_space=pl.ANY),
                  full(sw1), full(sw3), full(sw2), full(g), full(b)],
        out_specs=pl.BlockSpec((tm, D_MODEL), lambda i: (i, 0)),
        out_shape=jax.ShapeDtypeStruct((n, D_MODEL), F32),
        scratch_shapes=[pltpu.VMEM((TOP_K * tm * ROW_SUB, LANES), I32), pltpu.SemaphoreType.DMA(())],
        compiler_params=_params("arbitrary"),
        name="moe_combine_ln",
    )(*dests, gate_t, x, ys, sw1, sw3, sw2, g, b)


def _moe(x, rw_t, r_bias, w1, w3, w2, sw1, sw3, sw2, g, b, alpha):
    n = x.shape[0]
    eidx, gate, rank, cnt = _router(x, rw_t, r_bias)
    counts = cnt[:, 0].astype(I32)
    padded = (counts + EXPERT_BLOCK - 1) // EXPERT_BLOCK * EXPERT_BLOCK
    pad_end = jnp.cumsum(padded)
    pad_start = pad_end - padded
    onehot = eidx[:, :, None] == jnp.arange(N_EXPERTS, dtype=I32)[None, None, :]
    dest = jnp.sum(jnp.where(onehot, pad_start[None, None, :], 0), axis=-1) + rank
    n_rows = -(-(n * TOP_K + N_EXPERTS * (EXPERT_BLOCK - 1)) // EXPERT_BLOCK) * EXPERT_BLOCK
    nb = n_rows // EXPERT_BLOCK
    blk_lo = jnp.arange(nb, dtype=I32) * EXPERT_BLOCK
    blk_e = jnp.minimum(jnp.sum((pad_end[None, :] <= blk_lo[:, None]).astype(I32), axis=1), N_EXPERTS - 1)
    blk_valid = jnp.clip(jnp.take(pad_start + counts, blk_e) - blk_lo, 0, EXPERT_BLOCK).astype(I32)
    tm = _tile(n, (256, 128))
    runs = dest.reshape(8, n // tm, tm // LANES, 1, LANES)
    dests = [runs[k, :, r] for k in range(TOP_K) for r in range(tm // LANES)]
    xs = _dispatch(x, dests, n_rows, tm)
    ys = _expert_ffn(xs, blk_e, blk_valid, w1, w3, w2)
    return _combine(dests, gate.T, x, ys, sw1, sw3, sw2, g, b, alpha, tm)


def kernel(x_prompt, x_sample, cache_sb_k, cache_sb_v, cache_diff_k, cache_diff_v, state_s5_re, state_s5_im, state_ret, meta_tokens, w_in_e, s5_lam_re, s5_lam_im, s5_log_dt, s5_b_re, s5_b_im, s5_c_re, s5_c_im, s5_d, s5_w_glu, w_out_e, w_in_o, diff_lq1, diff_lk1, diff_lq2, diff_lk2, diff_norm_g, w_out_o, ln1_g, ln1_b, ln2_g, ln2_b, router_w, router_bias, exp_w1, exp_w3, exp_w2, sh_w1, sh_w3, sh_w2):
    bsz, seq, _ = x_prompt.shape
    bsz_s, t_s, _ = x_sample.shape
    past = cache_sb_k.shape[2]
    depth = ln1_g.shape[0]
    alpha = (2 * depth) ** 0.25
    lp = PAD + N_META + seq
    n_p = bsz * lp
    n_s = bsz_s * t_s
    lt = N_META + seq

    meta = jnp.broadcast_to(meta_tokens[None].astype(F32), (bsz, N_META, D_MODEL))
    hp = jnp.concatenate([jnp.zeros((bsz, PAD, D_MODEL), F32), meta, x_prompt], 1)
    h = jnp.concatenate([hp.reshape(n_p, D_MODEL), x_sample.reshape(n_s, D_MODEL)], 0)

    outs_p = {k: [] for k in ("sbk", "sbv", "dk", "dv", "s5r", "s5i", "ret")}
    outs_s = {k: [] for k in ("sbk", "sbv", "dk", "dv", "s5r", "s5i", "ret")}
    half = lambda c0, dt=F32: (c0 * W_HALF, W_HALF, dt)

    for layer in range(depth):
        i = layer // 2
        if layer % 2 == 0:
            w_in = w_in_e[i]
            splits = [half(c) for c in range(5)]
            u_p, q_p, k_p, v_p, g_p = _project(h, w_in.astype(BF16), splits, 0, n_p)
            u_s, q_s, k_s, v_s, g_s = _project(h, w_in, splits, n_p, n_s, precise=True)
            s5p = _s5_prepare(s5_lam_re[i], s5_lam_im[i], s5_log_dt[i], s5_b_re[i], s5_b_im[i],
                              s5_c_re[i], s5_c_im[i], s5_d[i], s5_w_glu[i])
            seq_p = lambda a: a.reshape(bsz, lp, W_HALF)
            seq_s = lambda a: a.reshape(bsz_s, t_s, W_HALF)
            z5 = jnp.zeros((bsz, S5_WIDTH), F32)
            y5_p, fr_p, fi_p = _s5_mixer(seq_p(u_p), z5, z5, s5p, Q_BLOCK, False)
            y5_s, fr_s, fi_s = _s5_mixer(seq_s(u_s), state_s5_re[i].reshape(bsz_s, S5_WIDTH).astype(F32),
                                         state_s5_im[i].reshape(bsz_s, S5_WIDTH).astype(F32), s5p, t_s, True)
            yr_p, rt_p = _ret_mixer(seq_p(q_p), seq_p(k_p), seq_p(v_p), seq_p(g_p),
                                    jnp.zeros((bsz, RET_HEADS, RET_DK, RET_DK), F32),
                                    jnp.arange(lp) - PAD, Q_BLOCK)
            yr_s, rt_s = _ret_mixer(seq_s(q_s), seq_s(k_s), seq_s(v_s), seq_s(g_s),
                                    state_ret[i].astype(F32), past + jnp.arange(t_s), t_s)
            mixed = (y5_p.reshape(n_p, W_HALF), yr_p.reshape(n_p, W_HALF),
                     y5_s.reshape(n_s, W_HALF), yr_s.reshape(n_s, W_HALF))
            w_out = w_out_e[i]
            st = lambda a, nb_: a.reshape(nb_, S5_GROUPS, S5_STATE)
            outs_p["s5r"].append(st(fr_p, bsz)); outs_p["s5i"].append(st(fi_p, bsz)); outs_p["ret"].append(rt_p)
            outs_s["s5r"].append(st(fr_s, bsz_s)); outs_s["s5i"].append(st(fi_s, bsz_s)); outs_s["ret"].append(rt_s)
        else:
            splits = [half(0, BF16), half(1), half(2), half(3, BF16), half(4), half(5),
                      half(1, BF16), half(2, BF16), half(4, BF16), half(5, BF16)]
            w_in = w_in_o[i].astype(BF16)
            (qs, qd, ksb, vsb, kdb, vdb, ksf, vsf, kdf, vdf) = _project_prompt_kv(h, w_in, bsz, lp)
            (qs_s, ksf_s, vsf_s, qd_s, kdf_s, vdf_s, ksb_s, vsb_s, kdb_s, vdb_s) = _project(h, w_in, splits, n_p, n_s)
            lam_init = 0.8 - 0.6 * math.exp(-0.3 * layer)
            lam_vecs = [v[i].reshape(1, DIFF_DH).astype(F32) for v in (diff_lq1, diff_lk1, diff_lq2, diff_lk2)]
            norm_g = diff_norm_g[i].reshape(2 * DIFF_DH, 1).astype(F32)
            seq_p = lambda a: a.reshape(bsz, lp, W_HALF)
            seq_s = lambda a: a.reshape(bsz_s, t_s, W_HALF)
            osb_p, od_p = _attention(seq_p(qs), seq_p(qd), seq_p(ksb), seq_p(vsb), seq_p(kdb), seq_p(vdb),
                                     lam_vecs, norm_g, bq=Q_BLOCK, q_base=0, kv_lo=PAD, lam_init=lam_init)
            bq_s = -(-t_s // Q_BLOCK) * Q_BLOCK
            lk_s = -(-(past + bq_s) // KEY_BLOCK) * KEY_BLOCK
            qpad = lambda a: jnp.pad(seq_s(a), ((0, 0), (0, bq_s - t_s), (0, 0)))

            def keys(cache, new):
                c = cache.reshape(bsz_s, past, W_HALF).astype(BF16)
                z = jnp.zeros((bsz_s, lk_s - past - t_s, W_HALF), BF16)
                return jnp.concatenate([c, seq_s(new), z], 1)
            osb_s, od_s = _attention(qpad(qs_s), qpad(qd_s), keys(cache_sb_k[i], ksb_s), keys(cache_sb_v[i], vsb_s),
                                     keys(cache_diff_k[i], kdb_s), keys(cache_diff_v[i], vdb_s),
                                     lam_vecs, norm_g, bq=bq_s, q_base=past, kv_lo=0, lam_init=lam_init)
            mixed = (osb_p.reshape(n_p, W_HALF), od_p.reshape(n_p, W_HALF),
                     osb_s[:, :t_s].reshape(n_s, W_HALF), od_s[:, :t_s].reshape(n_s, W_HALF))
            w_out = w_out_o[i]
            kv_p = lambda a, hh, dd: a.reshape(bsz, lt, hh, dd)
            kv_s = lambda a, hh, dd: a.reshape(bsz_s, t_s, hh, dd)
            outs_p["sbk"].append(kv_p(ksf, SB_HEADS, SB_DH)); outs_p["sbv"].append(kv_p(vsf, SB_HEADS, SB_DH))
            outs_p["dk"].append(kv_p(kdf, DIFF_HEADS, 2 * DIFF_DH)); outs_p["dv"].append(kv_p(vdf, DIFF_HEADS, 2 * DIFF_DH))
            outs_s["sbk"].append(kv_s(ksf_s, SB_HEADS, SB_DH)); outs_s["sbv"].append(kv_s(vsf_s, SB_HEADS, SB_DH))
            outs_s["dk"].append(kv_s(kdf_s, DIFF_HEADS, 2 * DIFF_DH)); outs_s["dv"].append(kv_s(vdf_s, DIFF_HEADS, 2 * DIFF_DH))

        row = lambda a: a[layer].reshape(1, D_MODEL).astype(F32)
        h = _outproj_ln(*mixed, h, w_out.astype(BF16), row(ln1_g), row(ln1_b), alpha)
        h = _moe(h, router_w[layer].T.astype(F32), router_bias[layer].reshape(N_EXPERTS, 1).astype(F32),
                 exp_w1[layer].astype(BF16), exp_w3[layer].astype(BF16), exp_w2[layer].astype(BF16),
                 sh_w1[layer].astype(BF16), sh_w3[layer].astype(BF16), sh_w2[layer].astype(BF16),
                 row(ln2_g), row(ln2_b), alpha)

    y_prompt = h[:n_p].reshape(bsz, lp, D_MODEL)[:, PAD + N_META:]
    y_sample = h[n_p:].reshape(bsz_s, t_s, D_MODEL)
    order = ("sbk", "sbv", "dk", "dv", "s5r", "s5i", "ret")
    return ((y_prompt, y_sample) + tuple(jnp.stack(outs_p[k]) for k in order)
            + tuple(jnp.stack(outs_s[k]) for k in order))
```

```python
import functools
import math

import jax
import jax.numpy as jnp
from jax import lax
from jax.experimental import pallas as pl
from jax.experimental.pallas import tpu as pltpu

F32 = jnp.float32
BF16 = jnp.bfloat16
I32 = jnp.int32
HIGHEST = lax.Precision.HIGHEST

D_MODEL = 1024
W_HALF = 512
N_META = 16
Q_BLOCK = 128
PAD = Q_BLOCK - N_META
CHUNK = 64
CHUNK_SHIFT = 6
S5_GROUPS = 32
S5_GROUP_CH = 16
S5_STATE = 64
S5_WIDTH = S5_GROUPS * S5_STATE
S5_BLOCKS = 4
RET_HEADS = 8
RET_DK = 64
SB_HEADS = 8
SB_DH = 64
DIFF_HEADS = 4
DIFF_DH = 64
N_EXPERTS = 64
N_GROUPS = 8
GROUP_SIZE = N_EXPERTS // N_GROUPS
TOPK_GROUPS = 4
TOP_K = 6
ROUTED_SCALE = 2.5
EXPERT_BLOCK = 1024
EPS = 1e-5
NEG_BIG = -1e30
KEY_BLOCK = 128
KEY_SHIFT = 7
SUBLANES = 8
LANES = 128
VMEM_LIMIT = 56 * 1024 * 1024

NT_DIMS = (((1,), (1,)), ((), ()))


def _params(*sem):
    return pltpu.CompilerParams(dimension_semantics=sem, vmem_limit_bytes=VMEM_LIMIT)


def _tile(n, cands=(512, 256, 128, 64)):
    for c in cands:
        if n % c == 0:
            return c
    raise ValueError(f"no tile for {n}")


def _layer_norm(r, g, b):
    mu = jnp.mean(r, -1, keepdims=True)
    d = r - mu
    var = jnp.mean(d * d, -1, keepdims=True)
    return d * lax.rsqrt(var + EPS) * g + b


def _proj_kernel(x_ref, w_ref, *o_refs, cols, precise):
    x = x_ref[...]
    xb = x.astype(F32) if precise else x.astype(BF16)
    for o_ref, c0 in zip(o_refs, cols):
        wd = o_ref.shape[-1]
        w = w_ref[:, c0:c0 + wd]
        if precise:
            y = jnp.dot(xb, w, precision=HIGHEST, preferred_element_type=F32)
        else:
            y = jnp.dot(xb, w, preferred_element_type=F32)
        o_ref[...] = y.astype(o_ref.dtype)


def _project(x, w, outs, row0, n, precise=False):
    k = x.shape[1]
    tm = _tile(math.gcd(n, row0) if row0 else n)
    blk0 = row0 // tm
    cols = tuple(o[0] for o in outs)
    return pl.pallas_call(
        functools.partial(_proj_kernel, cols=cols, precise=precise),
        grid=(n // tm,),
        in_specs=[pl.BlockSpec((tm, k), lambda i: (i + blk0, 0)),
                  pl.BlockSpec(w.shape, lambda i: (0, 0))],
        out_specs=[pl.BlockSpec((tm, o[1]), lambda i: (i, 0)) for o in outs],
        out_shape=[jax.ShapeDtypeStruct((n, o[1]), o[2]) for o in outs],
        compiler_params=_params("parallel"),
        name="in_proj",
    )(x, w)


def _proj_kv_kernel(x_ref, w_ref, qs_ref, qd_ref, ks_ref, vs_ref, kd_ref, vd_ref,
                    ksf_hbm, vsf_hbm, kdf_hbm, vdf_hbm, stage, sem, *, rows, nq, lt):
    b, q = pl.program_id(0), pl.program_id(1)
    step = b * nq + q
    slot = step % 2
    caches = (ksf_hbm, vsf_hbm, kdf_hbm, vdf_hbm)
    xb = x_ref[...].astype(BF16)
    col = lambda c: jnp.dot(xb, w_ref[:, c * W_HALF:(c + 1) * W_HALF], preferred_element_type=F32)
    qs_ref[...] = col(0).astype(BF16)
    qd_ref[...] = col(3).astype(BF16)
    for a, (c, o_ref) in enumerate(zip((1, 2, 4, 5), (ks_ref, vs_ref, kd_ref, vd_ref))):
        y = col(c)
        o_ref[...] = y.astype(BF16)
        stage[slot, a] = y

    def copy(slot_, a, b_, q_, first):
        if first:
            return pltpu.make_async_copy(stage.at[slot_, a, pl.ds(PAD, rows - PAD)],
                                         caches[a].at[pl.ds(pl.multiple_of(b_ * lt, SUBLANES), rows - PAD)], sem)
        dst = pl.multiple_of(b_ * lt + q_ * rows - PAD, SUBLANES)
        return pltpu.make_async_copy(stage.at[slot_, a], caches[a].at[pl.ds(dst, rows)], sem)

    def start_all(first):
        for a in range(len(caches)):
            copy(slot, a, b, q, first).start()

    def wait_all(first):
        for a in range(len(caches)):
            copy(0, a, 0, 1, first).wait()

    @pl.when((step > 0) & (q == 1))
    def _():
        wait_all(True)

    @pl.when((step > 0) & (q != 1))
    def _():
        wait_all(False)

    @pl.when(q == 0)
    def _():
        start_all(True)

    @pl.when(q != 0)
    def _():
        start_all(False)

    @pl.when(step == pl.num_programs(0) * nq - 1)
    def _():
        wait_all(False)


def _project_prompt_kv(x, w, bsz, lp):
    nq = next(c for c in (4, 3, 2) if lp % c == 0 and (lp // c) % SUBLANES == 0 and lp // c > PAD)
    rows = lp // nq
    lt = lp - PAD
    blk = lambda wd: pl.BlockSpec((rows, wd), lambda i, j: (i * nq + j, 0))
    pad_out = jax.ShapeDtypeStruct((bsz * lp, W_HALF), BF16)
    cache_out = jax.ShapeDtypeStruct((bsz * lt, W_HALF), F32)
    return pl.pallas_call(
        functools.partial(_proj_kv_kernel, rows=rows, nq=nq, lt=lt),
        grid=(bsz, nq),
        in_specs=[blk(x.shape[1]), pl.BlockSpec(w.shape, lambda i, j: (0, 0))],
        out_specs=[blk(W_HALF)] * 6 + [pl.BlockSpec(memory_space=pl.ANY)] * 4,
        out_shape=[pad_out] * 6 + [cache_out] * 4,
        scratch_shapes=[pltpu.VMEM((2, 4, rows, W_HALF), F32), pltpu.SemaphoreType.DMA(())],
        compiler_params=_params("arbitrary", "arbitrary"),
        name="in_proj_kv",
    )(x, w)


def _s5_kernel(u_ref, x0r_ref, x0i_ref, lr_ref, li_ref, bre_ref, bim_ref, cre_ref, cim_ref, d_ref,
               wglu_ref, y_ref, fr_ref, fi_ref, xr_s, xi_s, sr_s, si_s, *, tc, sg, precise):
    c = pl.program_id(1)

    @pl.when(c == 0)
    def _():
        xr_s[...] = x0r_ref[...]
        xi_s[...] = x0i_ref[...]

    def mm(a, b):
        if precise:
            return jnp.dot(a, b, precision=HIGHEST, preferred_element_type=F32)
        return jnp.dot(a.astype(BF16), b, preferred_element_type=F32)

    cw, sw = W_HALF // S5_BLOCKS, S5_WIDTH // S5_BLOCKS
    for s in range(sg):
        u = u_ref[s]
        sr_s[s] = jnp.concatenate(
            [mm(u[:, b * cw:(b + 1) * cw], bre_ref[b]) for b in range(S5_BLOCKS)], axis=1)
        si_s[s] = jnp.concatenate(
            [mm(u[:, b * cw:(b + 1) * cw], bim_ref[b]) for b in range(S5_BLOCKS)], axis=1)
    lr = lr_ref[...]
    li = li_ref[...]

    def step(t, carry):
        out = []
        for s, (xr, xi) in enumerate(carry):
            nr = lr * xr - li * xi + sr_s[s, pl.ds(t, 1), :]
            ni = lr * xi + li * xr + si_s[s, pl.ds(t, 1), :]
            sr_s[s, pl.ds(t, 1), :] = nr
            si_s[s, pl.ds(t, 1), :] = ni
            out.append((nr, ni))
        return tuple(out)

    init = tuple((xr_s[pl.ds(s, 1), :], xi_s[pl.ds(s, 1), :]) for s in range(sg))
    last = lax.fori_loop(0, tc, step, init, unroll=2)
    xr = jnp.concatenate([x[0] for x in last], axis=0)
    xi = jnp.concatenate([x[1] for x in last], axis=0)
    xr_s[...] = xr
    xi_s[...] = xi
    for s in range(sg):
        y = jnp.concatenate(
            [mm(sr_s[s, :, b * sw:(b + 1) * sw], cre_ref[b]) - mm(si_s[s, :, b * sw:(b + 1) * sw], cim_ref[b])
             for b in range(S5_BLOCKS)], axis=1) + d_ref[...] * u_ref[s]
        y = jax.nn.gelu(y)
        y = y * jax.nn.sigmoid(mm(y, wglu_ref[...]))
        y_ref[s] = y.astype(y_ref.dtype)

    @pl.when(c == pl.num_programs(1) - 1)
    def _():
        fr_ref[...] = xr
        fi_ref[...] = xi


def _s5_mixer(u, x0r, x0i, p, tc, precise):
    b, l, _ = u.shape
    nc = l // tc
    sg = math.gcd(b, SUBLANES)
    wdt = F32 if precise else BF16
    full = lambda a: pl.BlockSpec(a.shape, lambda i, j: (0,) * a.ndim)
    st_spec = pl.BlockSpec((sg, S5_WIDTH), lambda i, j: (i, 0))
    seq_spec = pl.BlockSpec((sg, tc, W_HALF), lambda i, j: (i, j, 0))
    mats = [p["bre"].astype(wdt), p["bim"].astype(wdt), p["cre"].astype(wdt), p["cim"].astype(wdt)]
    wglu = p["wglu"].astype(wdt)
    return pl.pallas_call(
        functools.partial(_s5_kernel, tc=tc, sg=sg, precise=precise),
        grid=(b // sg, nc),
        in_specs=[seq_spec, st_spec, st_spec,
                  full(p["lr"]), full(p["li"])] + [full(m) for m in mats] + [full(p["d"]), full(wglu)],
        out_specs=[seq_spec, st_spec, st_spec],
        out_shape=[jax.ShapeDtypeStruct((b, l, W_HALF), BF16),
                   jax.ShapeDtypeStruct((b, S5_WIDTH), F32),
                   jax.ShapeDtypeStruct((b, S5_WIDTH), F32)],
        scratch_shapes=[pltpu.VMEM((sg, S5_WIDTH), F32), pltpu.VMEM((sg, S5_WIDTH), F32),
                        pltpu.VMEM((sg, tc, S5_WIDTH), F32), pltpu.VMEM((sg, tc, S5_WIDTH), F32)],
        compiler_params=_params("parallel", "arbitrary"),
        name="s5_mixer",
    )(u, x0r, x0i, p["lr"], p["li"], *mats, p["d"], wglu)


def _s5_prepare(lam_re, lam_im, log_dt, b_re, b_im, c_re, c_im, d, w_glu):
    dt = jnp.exp(log_dt)[:, None]
    mag = jnp.exp(lam_re * dt)
    lb_re = mag * jnp.cos(lam_im * dt)
    lb_im = mag * jnp.sin(lam_im * dt)
    nr, ni = lb_re - 1.0, lb_im
    den = jnp.square(lam_re) + jnp.square(lam_im)
    cr = (nr * lam_re + ni * lam_im) / den
    ci = (ni * lam_re - nr * lam_im) / den
    bb_re = cr[..., None] * b_re - ci[..., None] * b_im
    bb_im = cr[..., None] * b_im + ci[..., None] * b_re
    gb = S5_GROUPS // S5_BLOCKS
    eye = jnp.eye(gb, dtype=F32)
    per_blk = lambda m: m.reshape((S5_BLOCKS, gb) + m.shape[1:])
    blk_in = lambda m: jnp.einsum("bgpi,gh->bgihp", per_blk(m), eye).reshape(
        S5_BLOCKS, W_HALF // S5_BLOCKS, S5_WIDTH // S5_BLOCKS)
    blk_out = lambda m: jnp.einsum("bgop,gh->bgpho", per_blk(m), eye).reshape(
        S5_BLOCKS, S5_WIDTH // S5_BLOCKS, W_HALF // S5_BLOCKS)
    return dict(lr=lb_re.reshape(1, S5_WIDTH), li=lb_im.reshape(1, S5_WIDTH),
                bre=blk_in(bb_re), bim=blk_in(bb_im), cre=blk_out(c_re), cim=blk_out(c_im),
                d=d.reshape(1, W_HALF), wglu=w_glu)


def _ret_kernel(q_ref, k_ref, v_ref, g_ref, cos_ref, sin_ref, dmat_ref, inter_ref, zeta_ref, gc_ref,
                r0_ref, y_ref, rf_ref, r_s, *, tc):
    c = pl.program_id(1)

    @pl.when(c == 0)
    def _():
        r_s[...] = r0_ref[0]

    lane = lax.broadcasted_iota(I32, (tc, W_HALF), 1)
    first_half = (lane & (RET_DK - 1)) < (RET_DK // 2)
    cos = cos_ref[...]
    sin = sin_ref[...]

    def rot(x):
        ahead = pltpu.roll(x, W_HALF - RET_DK // 2, 1)
        behind = pltpu.roll(x, RET_DK // 2, 1)
        return x * cos + jnp.where(first_half, ahead, behind) * sin

    q = rot(q_ref[...])
    k = rot(k_ref[...]) * (RET_DK ** -0.5)
    v = v_ref[...]
    vz = v * zeta_ref[...]
    g = g_ref[...]
    gate = g * jax.nn.sigmoid(g)
    heads = [slice(h * RET_DK, (h + 1) * RET_DK) for h in range(RET_HEADS)]
    qb = q.astype(BF16)
    kb = k.astype(BF16)
    vb = v.astype(BF16)
    vzb = vz.astype(BF16)
    kt = k.T.astype(BF16)
    scores = [lax.dot_general(qb[:, sl], kb[:, sl], NT_DIMS, preferred_element_type=F32) for sl in heads]
    cross = [jnp.dot(qb[:, sl], r_s[h].astype(BF16), preferred_element_type=F32)
             for h, sl in enumerate(heads)]
    new_r = [gc_ref[h] * r_s[h] + jnp.dot(kt[sl, :], vzb[:, sl], preferred_element_type=F32)
             for h, sl in enumerate(heads)]
    decayed = [(scores[h] * dmat_ref[h]).astype(BF16) for h in range(RET_HEADS)]
    outs = [jnp.dot(decayed[h], vb[:, sl], preferred_element_type=F32) + cross[h] * inter_ref[:, sl]
            for h, sl in enumerate(heads)]
    for h, sl in enumerate(heads):
        r_s[h] = new_r[h]
        o = outs[h]
        mu = jnp.mean(o, -1, keepdims=True)
        dlt = o - mu
        var = jnp.mean(dlt * dlt, -1, keepdims=True)
        y_ref[:, sl] = (dlt * lax.rsqrt(var + EPS) * gate[:, sl]).astype(y_ref.dtype)

    @pl.when(c == pl.num_programs(1) - 1)
    def _():
        rf_ref[0] = r_s[...]


def _ret_mixer(q, k, v, g, r0, pos, tc):
    b, l, _ = q.shape
    nc = l // tc
    inv = 1.0 / (10000.0 ** jnp.linspace(0.0, 1.0, RET_DK // 2, dtype=F32))
    ang = pos.astype(F32)[:, None] * inv[None]
    cos = jnp.tile(jnp.concatenate([jnp.cos(ang), jnp.cos(ang)], 1), (1, RET_HEADS))
    sin = jnp.tile(jnp.concatenate([-jnp.sin(ang), jnp.sin(ang)], 1), (1, RET_HEADS))
    log_gamma = jnp.log(1.0 - 2.0 ** (-5.0 - jnp.arange(RET_HEADS, dtype=F32)))
    n = jnp.arange(tc, dtype=F32)
    diff = n[:, None] - n[None, :]
    causal = diff >= 0
    dmat = jnp.where(causal[None], jnp.exp(jnp.where(causal, diff, 0.0)[None] * log_gamma[:, None, None]), 0.0)
    per_lane = lambda m: jnp.repeat(m, RET_DK, axis=1)
    inter = per_lane(jnp.exp((n[:, None] + 1.0) * log_gamma[None]))
    zeta = per_lane(jnp.exp((tc - 1.0 - n)[:, None] * log_gamma[None]))
    gc = jnp.broadcast_to(jnp.exp(tc * log_gamma)[:, None, None], (RET_HEADS, RET_DK, RET_DK))
    seq = pl.BlockSpec((None, tc, W_HALF), lambda i, j: (i, j, 0))
    tab = pl.BlockSpec((tc, W_HALF), lambda i, j: (j, 0))
    full = lambda a: pl.BlockSpec(a.shape, lambda i, j: (0,) * a.ndim)
    st = pl.BlockSpec((1, RET_HEADS, RET_DK, RET_DK), lambda i, j: (i, 0, 0, 0))
    return pl.pallas_call(
        functools.partial(_ret_kernel, tc=tc),
        grid=(b, nc),
        in_specs=[seq, seq, seq, seq, tab, tab, full(dmat), full(inter), full(zeta), full(gc), st],
        out_specs=[seq, st],
        out_shape=[jax.ShapeDtypeStruct((b, l, W_HALF), BF16),
                   jax.ShapeDtypeStruct((b, RET_HEADS, RET_DK, RET_DK), F32)],
        scratch_shapes=[pltpu.VMEM((RET_HEADS, RET_DK, RET_DK), F32)],
        compiler_params=_params("parallel", "arbitrary"),
        name="ret_mixer",
    )(q, k, v, g, cos, sin, dmat, inter, zeta, gc, r0)


def _attn_kernel(lq1_ref, lk1_ref, lq2_ref, lk2_ref, ng_ref, qs_ref, qd_ref, ks_ref, vst_ref, kd_ref,
                 vdt_ref, osb_ref, od_ref, *, bq, q_base, kv_lo, n_key_blocks, lam_init):
    qi = pl.program_id(1)
    q0 = q_base + qi * bq
    nkb = jnp.minimum(jnp.right_shift(q0 + bq + KEY_BLOCK - 1, KEY_SHIFT), n_key_blocks)
    qpos = q0 + lax.broadcasted_iota(I32, (KEY_BLOCK, bq), 1)
    koff = lax.broadcasted_iota(I32, (KEY_BLOCK, bq), 0)
    qchunk = jnp.right_shift(qpos, CHUNK_SHIFT)
    row = lax.broadcasted_iota(I32, (KEY_BLOCK, KEY_BLOCK), 0)
    col = lax.broadcasted_iota(I32, (KEY_BLOCK, KEY_BLOCK), 1)
    newer_keys = (col > row).astype(BF16)

    qs = (qs_ref[...].astype(F32) * (SB_DH ** -0.5)).astype(BF16)
    qd = (qd_ref[...].astype(F32) * (DIFF_DH ** -0.5)).astype(BF16)
    sb_q = [qs[:, h * SB_DH:(h + 1) * SB_DH] for h in range(SB_HEADS)]
    d_q = [qd[:, c * DIFF_DH:(c + 1) * DIFF_DH] for c in range(2 * DIFF_HEADS)]

    def body(t, carry, edge, nblk, j0):
        sb_acc, sb_tail, d_m, d_l, d_a = carry
        blocks = range(nblk)
        js = [j0 + nblk * t + i for i in blocks]
        kbs = [nkb - 1 - j for j in js]
        kblks = [ks_ref[kb] for kb in kbs]
        dblks = [kd_ref[j] for j in js]
        if edge:
            masks, dmasks = [], []
            for i in blocks:
                kpos = kbs[i] * KEY_BLOCK + koff
                masks.append((kpos >= kv_lo) & (kpos < qpos))
                dpos = js[i] * KEY_BLOCK + koff
                dmasks.append((dpos >= kv_lo) & (jnp.right_shift(dpos, CHUNK_SHIFT) <= qchunk))
            keep = lambda m, x, other: jnp.where(m, x, other)
        else:
            masks = dmasks = [None] * nblk
            keep = lambda m, x, other: x
        zs = [[lax.dot_general(kblks[i][:, h * SB_DH:(h + 1) * SB_DH], sb_q[h], NT_DIMS,
                               preferred_element_type=F32) for h in range(SB_HEADS)] for i in blocks]
        ss = [[lax.dot_general(dblks[i][:, c * DIFF_DH:(c + 1) * DIFF_DH], d_q[c], NT_DIMS,
                               preferred_element_type=F32) for c in range(2 * DIFF_HEADS)] for i in blocks]
        log_betas = [[None] * SB_HEADS for _ in blocks]
        splits = [[None] * SB_HEADS for _ in blocks]
        tails = [[None] * SB_HEADS for _ in blocks]
        new_tail = []
        for h in range(SB_HEADS):
            tail = sb_tail[h]
            for i in blocks:
                z = zs[i][h]
                sp = jnp.log(1.0 + jnp.exp(-jnp.abs(z)))
                log_betas[i][h] = jnp.minimum(z, 0.0) - sp
                cost = keep(masks[i], z - log_betas[i][h], 0.0)
                splits[i][h] = cost.astype(BF16)
                tails[i][h] = tail
                tail = tail + jnp.sum(cost, axis=0, keepdims=True)
            new_tail.append(tail)
        new_m, new_l, alphas = [], [], []
        probs = [[None] * (2 * DIFF_HEADS) for _ in blocks]
        for c in range(2 * DIFF_HEADS):
            s = [keep(dmasks[i], ss[i][c], NEG_BIG) for i in blocks]
            mn = d_m[c]
            for i in blocks:
                mn = jnp.maximum(mn, jnp.max(s[i], axis=0, keepdims=True))
            al = jnp.exp(d_m[c] - mn)
            l = al * d_l[c]
            for i in blocks:
                p = jnp.exp(s[i] - mn)
                l = l + jnp.sum(p, axis=0, keepdims=True)
                probs[i][c] = p.astype(BF16)
            new_m.append(mn)
            new_l.append(l)
            alphas.append(al)
        newer = [[jnp.dot(newer_keys, splits[i][h], preferred_element_type=F32)
                  for h in range(SB_HEADS)] for i in blocks]
        new_a = []
        for c in range(2 * DIFF_HEADS):
            a = alphas[c] * d_a[c]
            vsl = slice((c // 2) * 2 * DIFF_DH, (c // 2 + 1) * 2 * DIFF_DH)
            for i in blocks:
                a = a + jnp.dot(vdt_ref[js[i], vsl, :], probs[i][c], preferred_element_type=F32)
            new_a.append(a)
        ws = [[keep(masks[i], jnp.exp(log_betas[i][h] - newer[i][h] - tails[i][h]), 0.0).astype(BF16)
               for h in range(SB_HEADS)] for i in blocks]
        new_acc = []
        for h in range(SB_HEADS):
            acc = sb_acc[h]
            for i in blocks:
                acc = acc + jnp.dot(vst_ref[kbs[i], h * SB_DH:(h + 1) * SB_DH, :], ws[i][h],
                                    preferred_element_type=F32)
            new_acc.append(acc)
        return tuple(new_acc), tuple(new_tail), tuple(new_m), tuple(new_l), tuple(new_a)

    zrow = jnp.zeros((1, bq), F32)
    init = (tuple(jnp.zeros((SB_DH, bq), F32) for _ in range(SB_HEADS)),
            tuple(zrow for _ in range(SB_HEADS)),
            tuple(jnp.full((1, bq), NEG_BIG, F32) for _ in range(2 * DIFF_HEADS)),
            tuple(zrow for _ in range(2 * DIFF_HEADS)),
            tuple(jnp.zeros((2 * DIFF_DH, bq), F32) for _ in range(2 * DIFF_HEADS)))
    edge_body = functools.partial(body, edge=True, nblk=1, j0=0)
    n_pairs = jnp.right_shift(jnp.maximum(nkb - 2, 0), 1)
    carry = lax.fori_loop(0, jnp.minimum(nkb, 1), edge_body, init)
    carry = lax.fori_loop(0, n_pairs, functools.partial(body, edge=False, nblk=2, j0=1), carry)
    carry = lax.fori_loop(1 + 2 * n_pairs, nkb - 1, functools.partial(body, edge=False, nblk=1, j0=0), carry)
    sb_acc, _, _, d_l, d_a = lax.fori_loop(jnp.maximum(nkb - 1, 1), nkb, edge_body, carry)

    osb_ref[...] = jnp.concatenate(sb_acc, axis=0).T.astype(osb_ref.dtype)
    lam = (jnp.exp(jnp.sum(lq1_ref[...] * lk1_ref[...], axis=1, keepdims=True))
           - jnp.exp(jnp.sum(lq2_ref[...] * lk2_ref[...], axis=1, keepdims=True)) + lam_init)
    outs = []
    for h in range(DIFF_HEADS):
        o = d_a[2 * h] / d_l[2 * h] - lam * (d_a[2 * h + 1] / d_l[2 * h + 1])
        o = o * lax.rsqrt(jnp.mean(o * o, axis=0, keepdims=True) + EPS) * ng_ref[...]
        outs.append(o * (1.0 - lam_init))
    od_ref[...] = jnp.concatenate(outs, axis=0).T.astype(od_ref.dtype)


def _attention(qs, qd, ks, vs, kd, vd, lam_vecs, norm_g, *, bq, q_base, kv_lo, lam_init):
    b, lq, _ = qs.shape
    lk = ks.shape[1]
    nkb = lk // KEY_BLOCK
    assert bq == KEY_BLOCK and q_base % KEY_BLOCK == 0 and 0 <= kv_lo < KEY_BLOCK
    kblocks = lambda a: a.reshape(b, nkb, KEY_BLOCK, W_HALF)
    vblocks_t = lambda a: a.reshape(b, nkb, KEY_BLOCK, W_HALF).transpose(0, 1, 3, 2)
    qspec = pl.BlockSpec((None, bq, W_HALF), lambda i, j: (i, j, 0))
    kspec = pl.BlockSpec((None, nkb, KEY_BLOCK, W_HALF), lambda i, j: (i, 0, 0, 0))
    vspec = pl.BlockSpec((None, nkb, W_HALF, KEY_BLOCK), lambda i, j: (i, 0, 0, 0))
    vec = pl.BlockSpec((1, DIFF_DH), lambda i, j: (0, 0))
    return pl.pallas_call(
        functools.partial(_attn_kernel, bq=bq, q_base=q_base, kv_lo=kv_lo,
                          n_key_blocks=nkb, lam_init=lam_init),
        grid=(b, lq // bq),
        in_specs=[vec, vec, vec, vec, pl.BlockSpec((2 * DIFF_DH, 1), lambda i, j: (0, 0)),
                  qspec, qspec, kspec, vspec, kspec, vspec],
        out_specs=[qspec, qspec],
        out_shape=[jax.ShapeDtypeStruct((b, lq, W_HALF), BF16), jax.ShapeDtypeStruct((b, lq, W_HALF), BF16)],
        compiler_params=_params("parallel", "arbitrary"),
        name="sb_diff_attention",
    )(*lam_vecs, norm_g, qs, qd, kblocks(ks), vblocks_t(vs), kblocks(kd), vblocks_t(vd))


def _outproj_ln_kernel(yap_ref, ybp_ref, yas_ref, ybs_ref, h_ref, w_ref, g_ref, b_ref, o_ref, *,
                       alpha, prompt_blocks):
    is_prompt = pl.program_id(0) < prompt_blocks
    ya = jnp.where(is_prompt, yap_ref[...], yas_ref[...])
    yb = jnp.where(is_prompt, ybp_ref[...], ybs_ref[...])
    m = (jnp.dot(ya, w_ref[:W_HALF, :], preferred_element_type=F32)
         + jnp.dot(yb, w_ref[W_HALF:, :], preferred_element_type=F32))
    o_ref[...] = _layer_norm(alpha * h_ref[...] + m, g_ref[...], b_ref[...])


def _outproj_ln(ya_p, yb_p, ya_s, yb_s, h, w, g, b, alpha):
    n = h.shape[0]
    n_p, n_s = ya_p.shape[0], ya_s.shape[0]
    tm = _tile(math.gcd(n_p, n_s))
    pb = n_p // tm
    row = pl.BlockSpec((tm, D_MODEL), lambda i: (i, 0))
    prompt = pl.BlockSpec((tm, W_HALF), lambda i: (jnp.minimum(i, pb - 1), 0))
    sample = pl.BlockSpec((tm, W_HALF), lambda i: (jnp.maximum(i - pb, 0), 0))
    full = lambda a: pl.BlockSpec(a.shape, lambda i: (0, 0))
    return pl.pallas_call(
        functools.partial(_outproj_ln_kernel, alpha=alpha, prompt_blocks=pb),
        grid=(n // tm,),
        in_specs=[prompt, prompt, sample, sample, row, full(w), full(g), full(b)],
        out_specs=row,
        out_shape=jax.ShapeDtypeStruct((n, D_MODEL), F32),
        compiler_params=_params("parallel"),
        name="out_proj_ln",
    )(ya_p, yb_p, ya_s, yb_s, h, w, g, b)


def _router_kernel(x_ref, wt_ref, bias_ref, eidx_ref, gate_ref, rank_ref, cnt_ref, run_s, *, tm):
    i = pl.program_id(0)

    @pl.when(i == 0)
    def _():
        run_s[...] = jnp.zeros_like(run_s)

    logits = lax.dot_general(wt_ref[...], x_ref[...], NT_DIMS, precision=HIGHEST,
                             preferred_element_type=F32)
    scores = jax.nn.sigmoid(logits)
    sel = scores + bias_ref[...]
    ninf = -jnp.inf

    sel3 = sel.reshape(N_GROUPS, GROUP_SIZE, tm)
    within = lax.broadcasted_iota(I32, (N_GROUPS, GROUP_SIZE, tm), 1)
    m1 = jnp.max(sel3, axis=1, keepdims=True)
    first = jnp.min(jnp.where(sel3 == m1, within, GROUP_SIZE), axis=1, keepdims=True)
    m2 = jnp.max(jnp.where(within == first, ninf, sel3), axis=1, keepdims=True)
    gscore = (m1 + m2).reshape(N_GROUPS, tm)

    gid = lax.broadcasted_iota(I32, (N_GROUPS, tm), 0)
    gsel = jnp.zeros((N_GROUPS, tm), jnp.bool_)
    cur = gscore
    for _ in range(TOPK_GROUPS):
        m = jnp.max(cur, axis=0, keepdims=True)
        f = jnp.min(jnp.where(cur == m, gid, N_GROUPS), axis=0, keepdims=True)
        pick = gid == f
        gsel = gsel | pick
        cur = jnp.where(pick, ninf, cur)
    emask = jnp.broadcast_to(gsel.reshape(N_GROUPS, 1, tm), (N_GROUPS, GROUP_SIZE, tm)).reshape(N_EXPERTS, tm)

    eid = lax.broadcasted_iota(I32, (N_EXPERTS, tm), 0)
    cur = jnp.where(emask, sel, ninf)
    picks, idxs, gates = [], [], []
    for _ in range(TOP_K):
        m = jnp.max(cur, axis=0, keepdims=True)
        f = jnp.min(jnp.where(cur == m, eid, N_EXPERTS), axis=0, keepdims=True)
        pick = eid == f
        picks.append(pick)
        idxs.append(f)
        gates.append(jnp.sum(jnp.where(pick, scores, 0.0), axis=0, keepdims=True))
        cur = jnp.where(pick, ninf, cur)
    gsum = gates[0]
    for gk in gates[1:]:
        gsum = gsum + gk

    chosen = picks[0]
    for pk in picks[1:]:
        chosen = chosen | pk
    chosen_f = chosen.astype(F32)
    r_ = lax.broadcasted_iota(I32, (tm, tm), 0)
    c_ = lax.broadcasted_iota(I32, (tm, tm), 1)
    earlier = (r_ < c_).astype(BF16)
    before = run_s[...] + jnp.dot(chosen_f.astype(BF16), earlier, preferred_element_type=F32)
    zrow_i = jnp.zeros((1, tm), I32)
    zrow_f = jnp.zeros((1, tm), F32)
    ranks = [jnp.sum(jnp.where(pk, before, 0.0), axis=0, keepdims=True).astype(I32) for pk in picks]
    eidx_ref[...] = jnp.concatenate(idxs + [zrow_i, zrow_i], axis=0)
    gate_ref[...] = jnp.concatenate([gk / gsum * ROUTED_SCALE for gk in gates] + [zrow_f, zrow_f], axis=0)
    rank_ref[...] = jnp.concatenate(ranks + [zrow_i, zrow_i], axis=0)
    run_s[...] = run_s[...] + jnp.sum(chosen_f, axis=1, keepdims=True)
    cnt_ref[...] = jnp.broadcast_to(run_s[...], cnt_ref.shape)


def _router(x, wt, bias):
    n = x.shape[0]
    tm = _tile(n, (512, 256, 128))
    lane_blk = pl.BlockSpec((8, tm), lambda i: (0, i))
    return pl.pallas_call(
        functools.partial(_router_kernel, tm=tm),
        grid=(n // tm,),
        in_specs=[pl.BlockSpec((tm, D_MODEL), lambda i: (i, 0)),
                  pl.BlockSpec(wt.shape, lambda i: (0, 0)),
                  pl.BlockSpec(bias.shape, lambda i: (0, 0))],
        out_specs=[lane_blk, lane_blk, lane_blk, pl.BlockSpec((N_EXPERTS, 128), lambda i: (0, 0))],
        out_shape=[jax.ShapeDtypeStruct((8, n), I32), jax.ShapeDtypeStruct((8, n), F32),
                   jax.ShapeDtypeStruct((8, n), I32), jax.ShapeDtypeStruct((N_EXPERTS, 128), F32)],
        scratch_shapes=[pltpu.VMEM((N_EXPERTS, 1), F32)],
        compiler_params=_params("arbitrary"),
        name="moe_router",
    )(x, wt, bias)


ROW_SUB = D_MODEL // 2 // LANES
HIGH_HALF = -65536
DMA_PRIORITIES = 2


def _pack_rows(dst_ref, x, r0=0):
    rows = x.shape[0]
    bits = lambda v: lax.bitcast_convert_type(v.astype(BF16).astype(F32), I32)
    w = (lax.shift_right_logical(bits(x[:, :D_MODEL // 2]), jnp.full((), 16, I32))
         | (bits(x[:, D_MODEL // 2:]) & HIGH_HALF))
    for c in range(ROW_SUB):
        dst_ref[pl.ds(r0 * ROW_SUB + c, rows, stride=ROW_SUB), :] = w[:, c * LANES:(c + 1) * LANES]


def _unpack_rows(src_ref, rows, r0=0):
    ws = [src_ref[pl.ds(r0 * ROW_SUB + c, rows, stride=ROW_SUB), :] for c in range(ROW_SUB)]
    lo = [lax.bitcast_convert_type(lax.shift_left(w, jnp.full((), 16, I32)), F32) for w in ws]
    hi = [lax.bitcast_convert_type(w & HIGH_HALF, F32) for w in ws]
    return jnp.concatenate(lo + hi, axis=1)


def _dispatch_kernel(*refs, tm):
    n_idx = TOP_K * tm // LANES
    dest_refs, (x_ref, o_hbm, buf, sem) = refs[:n_idx], refs[n_idx:]
    i = pl.program_id(0)
    slot = i % 2
    stage = buf.at[slot]
    _pack_rows(stage, x_ref[...])

    def wait_all():
        for _ in range(TOP_K):
            pltpu.make_async_copy(stage, o_hbm.at[pl.ds(0, tm * ROW_SUB)], sem).wait()

    @pl.when(i > 0)
    def _():
        wait_all()

    for k in range(TOP_K):
        for blk in range(tm // LANES):
            def start(t, carry, k=k, blk=blk):
                for prio in range(DMA_PRIORITIES):
                    jj = t * DMA_PRIORITIES + prio
                    d = pl.multiple_of(dest_refs[k * (tm // LANES) + blk][0, 0, jj] * ROW_SUB, ROW_SUB)
                    s = pl.multiple_of((blk * LANES + jj) * ROW_SUB, ROW_SUB)
                    pltpu.make_async_copy(stage.at[pl.ds(s, ROW_SUB)], o_hbm.at[pl.ds(d, ROW_SUB)],
                                          sem).start(priority=prio)
                return carry
            lax.fori_loop(0, LANES // DMA_PRIORITIES, start, 0, unroll=4)

    @pl.when(i == pl.num_programs(0) - 1)
    def _():
        wait_all()


def _slot_spec():
    return pl.BlockSpec((1, 1, LANES), lambda i: (i, 0, 0), memory_space=pltpu.SMEM)


def _dispatch(x, dests, n_rows, tm):
    n = x.shape[0]
    return pl.pallas_call(
        functools.partial(_dispatch_kernel, tm=tm),
        grid=(n // tm,),
        in_specs=[_slot_spec() for _ in dests] + [pl.BlockSpec((tm, D_MODEL), lambda i: (i, 0))],
        out_specs=pl.BlockSpec(memory_space=pl.ANY),
        out_shape=jax.ShapeDtypeStruct((n_rows * ROW_SUB, LANES), I32),
        scratch_shapes=[pltpu.VMEM((2, tm * ROW_SUB, LANES), I32), pltpu.SemaphoreType.DMA(())],
        compiler_params=_params("arbitrary"),
        name="moe_dispatch",
    )(*dests, x)


def _ffn_kernel(be_ref, nv_ref, xs_ref, w1_ref, w3_ref, w2_ref, o_ref, wb1, wb3, wb2):
    i = pl.program_id(0)

    @pl.when((i == 0) | (be_ref[i] != be_ref[jnp.maximum(i - 1, 0)]))
    def _():
        wb1[...] = w1_ref[0].astype(BF16)
        wb3[...] = w3_ref[0].astype(BF16)
        wb2[...] = w2_ref[0].astype(BF16)

    x = _unpack_rows(xs_ref, EXPERT_BLOCK)
    live = lax.broadcasted_iota(I32, (EXPERT_BLOCK, 1), 0) < nv_ref[i]
    x = jnp.where(live, x, 0.0).astype(BF16)
    a = jnp.dot(x, wb1[...], preferred_element_type=F32)
    b = jnp.dot(x, wb3[...], preferred_element_type=F32)
    hdn = (a * jax.nn.sigmoid(a) * b).astype(BF16)
    _pack_rows(o_ref, jnp.dot(hdn, wb2[...], preferred_element_type=F32))


def _expert_ffn(xs, blk_e, blk_valid, w1, w3, w2):
    nb = xs.shape[0] // (EXPERT_BLOCK * ROW_SUB)
    de = w1.shape[-1]
    rows = pl.BlockSpec((EXPERT_BLOCK * ROW_SUB, LANES), lambda i, be, nv: (i, 0))
    grid_spec = pltpu.PrefetchScalarGridSpec(
        num_scalar_prefetch=2,
        grid=(nb,),
        in_specs=[rows,
                  pl.BlockSpec((1, D_MODEL, de), lambda i, be, nv: (be[i], 0, 0)),
                  pl.BlockSpec((1, D_MODEL, de), lambda i, be, nv: (be[i], 0, 0)),
                  pl.BlockSpec((1, de, D_MODEL), lambda i, be, nv: (be[i], 0, 0))],
        out_specs=rows,
        scratch_shapes=[pltpu.VMEM((D_MODEL, de), BF16), pltpu.VMEM((D_MODEL, de), BF16),
                        pltpu.VMEM((de, D_MODEL), BF16)],
    )
    return pl.pallas_call(
        _ffn_kernel,
        grid_spec=grid_spec,
        out_shape=jax.ShapeDtypeStruct(xs.shape, I32),
        compiler_params=_params("arbitrary"),
        name="moe_expert_ffn",
    )(blk_e, blk_valid, xs, w1, w3, w2)


def _combine_kernel(*refs, tm, alpha):
    n_idx = TOP_K * tm // LANES
    dest_refs = refs[:n_idx]
    gate_ref, x_ref, ys_hbm, sw1_ref, sw3_ref, sw2_ref, g_ref, b_ref, o_ref, buf, sem = refs[n_idx:]
    for k in range(TOP_K):
        for blk in range(tm // LANES):
            def start(t, carry, k=k, blk=blk):
                for prio in range(DMA_PRIORITIES):
                    jj = t * DMA_PRIORITIES + prio
                    s = pl.multiple_of(dest_refs[k * (tm // LANES) + blk][0, 0, jj] * ROW_SUB, ROW_SUB)
                    d = pl.multiple_of((k * tm + blk * LANES + jj) * ROW_SUB, ROW_SUB)
                    pltpu.make_async_copy(ys_hbm.at[pl.ds(s, ROW_SUB)], buf.at[pl.ds(d, ROW_SUB)],
                                          sem.at[k]).start(priority=prio)
                return carry
            lax.fori_loop(0, LANES // DMA_PRIORITIES, start, 0, unroll=4)

    x = x_ref[...]
    xb = x.astype(BF16)
    a = jnp.dot(xb, sw1_ref[...], preferred_element_type=F32)
    b = jnp.dot(xb, sw3_ref[...], preferred_element_type=F32)
    shared = jnp.dot((a * jax.nn.sigmoid(a) * b).astype(BF16), sw2_ref[...], preferred_element_type=F32)

    gate = gate_ref[...]
    routed = shared
    for k in range(TOP_K):
        grp = pl.ds(k * tm * ROW_SUB, tm * ROW_SUB)
        pltpu.make_async_copy(ys_hbm.at[pl.ds(0, tm * ROW_SUB)], buf.at[grp], sem.at[k]).wait()
        routed = routed + gate[:, k:k + 1] * _unpack_rows(buf, tm, k * tm)
    o_ref[...] = _layer_norm(alpha * x + routed, g_ref[...], b_ref[...])


def _combine(dests, gate_t, x, ys, sw1, sw3, sw2, g, b, alpha, tm):
    n = x.shape[0]
    full = lambda a: pl.BlockSpec(a.shape, lambda i: (0,) * a.ndim)
    return pl.pallas_call(
        functools.partial(_combine_kernel, tm=tm, alpha=alpha),
        grid=(n // tm,),
        in_specs=[_slot_spec() for _ in dests] + [
                  pl.BlockSpec((tm, 8), lambda i: (i, 0)),
                  pl.BlockSpec((tm, D_MODEL), lambda i: (i, 0)),
                  pl.BlockSpec(memory_space=pl.ANY),
                  full(sw1), full(sw3), full(sw2), full(g), full(b)],
        out_specs=pl.BlockSpec((tm, D_MODEL), lambda i: (i, 0)),
        out_shape=jax.ShapeDtypeStruct((n, D_MODEL), F32),
        scratch_shapes=[pltpu.VMEM((TOP_K * tm * ROW_SUB, LANES), I32), pltpu.SemaphoreType.DMA((TOP_K,))],
        compiler_params=_params("arbitrary"),
        name="moe_combine_ln",
    )(*dests, gate_t, x, ys, sw1, sw3, sw2, g, b)


def _moe(x, rw_t, r_bias, w1, w3, w2, sw1, sw3, sw2, g, b, alpha):
    n = x.shape[0]
    eidx, gate, rank, cnt = _router(x, rw_t, r_bias)
    counts = cnt[:, 0].astype(I32)
    padded = (counts + EXPERT_BLOCK - 1) // EXPERT_BLOCK * EXPERT_BLOCK
    pad_end = jnp.cumsum(padded)
    pad_start = pad_end - padded
    onehot = eidx[:, :, None] == jnp.arange(N_EXPERTS, dtype=I32)[None, None, :]
    dest = jnp.sum(jnp.where(onehot, pad_start[None, None, :], 0), axis=-1) + rank
    n_rows = -(-(n * TOP_K + N_EXPERTS * (EXPERT_BLOCK - 1)) // EXPERT_BLOCK) * EXPERT_BLOCK
    nb = n_rows // EXPERT_BLOCK
    blk_lo = jnp.arange(nb, dtype=I32) * EXPERT_BLOCK
    blk_e = jnp.minimum(jnp.sum((pad_end[None, :] <= blk_lo[:, None]).astype(I32), axis=1), N_EXPERTS - 1)
    blk_valid = jnp.clip(jnp.take(pad_start + counts, blk_e) - blk_lo, 0, EXPERT_BLOCK).astype(I32)
    tm = _tile(n, (256, 128))
    runs = dest.reshape(8, n // tm, tm // LANES, 1, LANES)
    dests = [runs[k, :, r] for k in range(TOP_K) for r in range(tm // LANES)]
    xs = _dispatch(x, dests, n_rows, tm)
    ys = _expert_ffn(xs, blk_e, blk_valid, w1, w3, w2)
    return _combine(dests, gate.T, x, ys, sw1, sw3, sw2, g, b, alpha, tm)


def kernel(x_prompt, x_sample, cache_sb_k, cache_sb_v, cache_diff_k, cache_diff_v, state_s5_re, state_s5_im, state_ret, meta_tokens, w_in_e, s5_lam_re, s5_lam_im, s5_log_dt, s5_b_re, s5_b_im, s5_c_re, s5_c_im, s5_d, s5_w_glu, w_out_e, w_in_o, diff_lq1, diff_lk1, diff_lq2, diff_lk2, diff_norm_g, w_out_o, ln1_g, ln1_b, ln2_g, ln2_b, router_w, router_bias, exp_w1, exp_w3, exp_w2, sh_w1, sh_w3, sh_w2):
    bsz, seq, _ = x_prompt.shape
    bsz_s, t_s, _ = x_sample.shape
    past = cache_sb_k.shape[2]
    depth = ln1_g.shape[0]
    alpha = (2 * depth) ** 0.25
    lp = PAD + N_META + seq
    n_p = bsz * lp
    n_s = bsz_s * t_s
    lt = N_META + seq

    meta = jnp.broadcast_to(meta_tokens[None].astype(F32), (bsz, N_META, D_MODEL))
    hp = jnp.concatenate([jnp.zeros((bsz, PAD, D_MODEL), F32), meta, x_prompt], 1)
    h = jnp.concatenate([hp.reshape(n_p, D_MODEL), x_sample.reshape(n_s, D_MODEL)], 0)

    outs_p = {k: [] for k in ("sbk", "sbv", "dk", "dv", "s5r", "s5i", "ret")}
    outs_s = {k: [] for k in ("sbk", "sbv", "dk", "dv", "s5r", "s5i", "ret")}
    half = lambda c0, dt=F32: (c0 * W_HALF, W_HALF, dt)

    for layer in range(depth):
        i = layer // 2
        if layer % 2 == 0:
            w_in = w_in_e[i]
            splits = [half(c) for c in range(5)]
            u_p, q_p, k_p, v_p, g_p = _project(h, w_in.astype(BF16), splits, 0, n_p)
            u_s, q_s, k_s, v_s, g_s = _project(h, w_in, splits, n_p, n_s, precise=True)
            s5p = _s5_prepare(s5_lam_re[i], s5_lam_im[i], s5_log_dt[i], s5_b_re[i], s5_b_im[i],
                              s5_c_re[i], s5_c_im[i], s5_d[i], s5_w_glu[i])
            seq_p = lambda a: a.reshape(bsz, lp, W_HALF)
            seq_s = lambda a: a.reshape(bsz_s, t_s, W_HALF)
            z5 = jnp.zeros((bsz, S5_WIDTH), F32)
            y5_p, fr_p, fi_p = _s5_mixer(seq_p(u_p), z5, z5, s5p, Q_BLOCK, False)
            y5_s, fr_s, fi_s = _s5_mixer(seq_s(u_s), state_s5_re[i].reshape(bsz_s, S5_WIDTH).astype(F32),
                                         state_s5_im[i].reshape(bsz_s, S5_WIDTH).astype(F32), s5p, t_s, True)
            yr_p, rt_p = _ret_mixer(seq_p(q_p), seq_p(k_p), seq_p(v_p), seq_p(g_p),
                                    jnp.zeros((bsz, RET_HEADS, RET_DK, RET_DK), F32),
                                    jnp.arange(lp) - PAD, Q_BLOCK)
            yr_s, rt_s = _ret_mixer(seq_s(q_s), seq_s(k_s), seq_s(v_s), seq_s(g_s),
                                    state_ret[i].astype(F32), past + jnp.arange(t_s), t_s)
            mixed = (y5_p.reshape(n_p, W_HALF), yr_p.reshape(n_p, W_HALF),
                     y5_s.reshape(n_s, W_HALF), yr_s.reshape(n_s, W_HALF))
            w_out = w_out_e[i]
            st = lambda a, nb_: a.reshape(nb_, S5_GROUPS, S5_STATE)
            outs_p["s5r"].append(st(fr_p, bsz)); outs_p["s5i"].append(st(fi_p, bsz)); outs_p["ret"].append(rt_p)
            outs_s["s5r"].append(st(fr_s, bsz_s)); outs_s["s5i"].append(st(fi_s, bsz_s)); outs_s["ret"].append(rt_s)
        else:
            splits = [half(0, BF16), half(1), half(2), half(3, BF16), half(4), half(5),
                      half(1, BF16), half(2, BF16), half(4, BF16), half(5, BF16)]
            w_in = w_in_o[i].astype(BF16)
            (qs, qd, ksb, vsb, kdb, vdb, ksf, vsf, kdf, vdf) = _project_prompt_kv(h, w_in, bsz, lp)
            (qs_s, ksf_s, vsf_s, qd_s, kdf_s, vdf_s, ksb_s, vsb_s, kdb_s, vdb_s) = _project(h, w_in, splits, n_p, n_s)
            lam_init = 0.8 - 0.6 * math.exp(-0.3 * layer)
            lam_vecs = [v[i].reshape(1, DIFF_DH).astype(F32) for v in (diff_lq1, diff_lk1, diff_lq2, diff_lk2)]
            norm_g = diff_norm_g[i].reshape(2 * DIFF_DH, 1).astype(F32)
            seq_p = lambda a: a.reshape(bsz, lp, W_HALF)
            seq_s = lambda a: a.reshape(bsz_s, t_s, W_HALF)
            osb_p, od_p = _attention(seq_p(qs), seq_p(qd), seq_p(ksb), seq_p(vsb), seq_p(kdb), seq_p(vdb),
                                     lam_vecs, norm_g, bq=Q_BLOCK, q_base=0, kv_lo=PAD, lam_init=lam_init)
            bq_s = -(-t_s // Q_BLOCK) * Q_BLOCK
            lk_s = -(-(past + bq_s) // KEY_BLOCK) * KEY_BLOCK
            qpad = lambda a: jnp.pad(seq_s(a), ((0, 0), (0, bq_s - t_s), (0, 0)))

            def keys(cache, new):
                c = cache.reshape(bsz_s, past, W_HALF).astype(BF16)
                z = jnp.zeros((bsz_s, lk_s - past - t_s, W_HALF), BF16)
                return jnp.concatenate([c, seq_s(new), z], 1)
            osb_s, od_s = _attention(qpad(qs_s), qpad(qd_s), keys(cache_sb_k[i], ksb_s), keys(cache_sb_v[i], vsb_s),
                                     keys(cache_diff_k[i], kdb_s), keys(cache_diff_v[i], vdb_s),
                                     lam_vecs, norm_g, bq=bq_s, q_base=past, kv_lo=0, lam_init=lam_init)
            mixed = (osb_p.reshape(n_p, W_HALF), od_p.reshape(n_p, W_HALF),
                     osb_s[:, :t_s].reshape(n_s, W_HALF), od_s[:, :t_s].reshape(n_s, W_HALF))
            w_out = w_out_o[i]
            kv_p = lambda a, hh, dd: a.reshape(bsz, lt, hh, dd)
            kv_s = lambda a, hh, dd: a.reshape(bsz_s, t_s, hh, dd)
            outs_p["sbk"].append(kv_p(ksf, SB_HEADS, SB_DH)); outs_p["sbv"].append(kv_p(vsf, SB_HEADS, SB_DH))
            outs_p["dk"].append(kv_p(kdf, DIFF_HEADS, 2 * DIFF_DH)); outs_p["dv"].append(kv_p(vdf, DIFF_HEADS, 2 * DIFF_DH))
            outs_s["sbk"].append(kv_s(ksf_s, SB_HEADS, SB_DH)); outs_s["sbv"].append(kv_s(vsf_s, SB_HEADS, SB_DH))
            outs_s["dk"].append(kv_s(kdf_s, DIFF_HEADS, 2 * DIFF_DH)); outs_s["dv"].append(kv_s(vdf_s, DIFF_HEADS, 2 * DIFF_DH))

        row = lambda a: a[layer].reshape(1, D_MODEL).astype(F32)
        h = _outproj_ln(*mixed, h, w_out.astype(BF16), row(ln1_g), row(ln1_b), alpha)
        h = _moe(h, router_w[layer].T.astype(F32), router_bias[layer].reshape(N_EXPERTS, 1).astype(F32),
                 exp_w1[layer].astype(F32), exp_w3[layer].astype(F32), exp_w2[layer].astype(F32),
                 sh_w1[layer].astype(BF16), sh_w3[layer].astype(BF16), sh_w2[layer].astype(BF16),
                 row(ln2_g), row(ln2_b), alpha)

    y_prompt = h[:n_p].reshape(bsz, lp, D_MODEL)[:, PAD + N_META:]
    y_sample = h[n_p:].reshape(bsz_s, t_s, D_MODEL)
    order = ("sbk", "sbv", "dk", "dv", "s5r", "s5i", "ret")
    return ((y_prompt, y_sample) + tuple(jnp.stack(outs_p[k]) for k in order)
            + tuple(jnp.stack(outs_s[k]) for k in order))
```

```python
import functools
import math

import jax
import jax.numpy as jnp
from jax import lax
from jax.experimental import pallas as pl
from jax.experimental.pallas import tpu as pltpu

F32 = jnp.float32
BF16 = jnp.bfloat16
I32 = jnp.int32
HIGHEST = lax.Precision.HIGHEST

D_MODEL = 1024
W_HALF = 512
N_META = 16
Q_BLOCK = 128
PAD = Q_BLOCK - N_META
CHUNK = 64
CHUNK_SHIFT = 6
S5_GROUPS = 32
S5_GROUP_CH = 16
S5_STATE = 64
S5_WIDTH = S5_GROUPS * S5_STATE
S5_BLOCKS = 4
RET_HEADS = 8
RET_DK = 64
SB_HEADS = 8
SB_DH = 64
DIFF_HEADS = 4
DIFF_DH = 64
N_EXPERTS = 64
N_GROUPS = 8
GROUP_SIZE = N_EXPERTS // N_GROUPS
TOPK_GROUPS = 4
TOP_K = 6
ROUTED_SCALE = 2.5
EXPERT_BLOCK = 1024
EPS = 1e-5
NEG_BIG = -1e30
KEY_BLOCK = 128
KEY_SHIFT = 7
SUBLANES = 8
LANES = 128
VMEM_LIMIT = 56 * 1024 * 1024

NT_DIMS = (((1,), (1,)), ((), ()))


def _params(*sem):
    return pltpu.CompilerParams(dimension_semantics=sem, vmem_limit_bytes=VMEM_LIMIT)


def _tile(n, cands=(512, 256, 128, 64)):
    for c in cands:
        if n % c == 0:
            return c
    raise ValueError(f"no tile for {n}")


def _layer_norm(r, g, b):
    mu = jnp.mean(r, -1, keepdims=True)
    d = r - mu
    var = jnp.mean(d * d, -1, keepdims=True)
    return d * lax.rsqrt(var + EPS) * g + b


def _proj_kernel(x_ref, w_ref, *o_refs, cols, precise):
    x = x_ref[...]
    xb = x.astype(F32) if precise else x.astype(BF16)
    for o_ref, c0 in zip(o_refs, cols):
        wd = o_ref.shape[-1]
        w = w_ref[:, c0:c0 + wd]
        if precise:
            y = jnp.dot(xb, w, precision=HIGHEST, preferred_element_type=F32)
        else:
            y = jnp.dot(xb, w, preferred_element_type=F32)
        o_ref[...] = y.astype(o_ref.dtype)


def _project(x, w, outs, row0, n, precise=False):
    k = x.shape[1]
    tm = _tile(math.gcd(n, row0) if row0 else n)
    blk0 = row0 // tm
    cols = tuple(o[0] for o in outs)
    return pl.pallas_call(
        functools.partial(_proj_kernel, cols=cols, precise=precise),
        grid=(n // tm,),
        in_specs=[pl.BlockSpec((tm, k), lambda i: (i + blk0, 0)),
                  pl.BlockSpec(w.shape, lambda i: (0, 0))],
        out_specs=[pl.BlockSpec((tm, o[1]), lambda i: (i, 0)) for o in outs],
        out_shape=[jax.ShapeDtypeStruct((n, o[1]), o[2]) for o in outs],
        compiler_params=_params("parallel"),
        name="in_proj",
    )(x, w)


def _proj_kv_kernel(x_ref, w_ref, qs_ref, qd_ref, ks_ref, vs_ref, kd_ref, vd_ref,
                    ksf_hbm, vsf_hbm, kdf_hbm, vdf_hbm, stage, sem, *, rows, nq, lt):
    b, q = pl.program_id(0), pl.program_id(1)
    step = b * nq + q
    slot = step % 2
    caches = (ksf_hbm, vsf_hbm, kdf_hbm, vdf_hbm)
    xb = x_ref[...].astype(BF16)
    col = lambda c: jnp.dot(xb, w_ref[:, c * W_HALF:(c + 1) * W_HALF], preferred_element_type=F32)
    qs_ref[...] = col(0).astype(BF16)
    qd_ref[...] = col(3).astype(BF16)
    for a, (c, o_ref) in enumerate(zip((1, 2, 4, 5), (ks_ref, vs_ref, kd_ref, vd_ref))):
        y = col(c)
        o_ref[...] = y.astype(BF16)
        stage[slot, a] = y

    def copy(slot_, a, b_, q_, first):
        if first:
            return pltpu.make_async_copy(stage.at[slot_, a, pl.ds(PAD, rows - PAD)],
                                         caches[a].at[pl.ds(pl.multiple_of(b_ * lt, SUBLANES), rows - PAD)], sem)
        dst = pl.multiple_of(b_ * lt + q_ * rows - PAD, SUBLANES)
        return pltpu.make_async_copy(stage.at[slot_, a], caches[a].at[pl.ds(dst, rows)], sem)

    def start_all(first):
        for a in range(len(caches)):
            copy(slot, a, b, q, first).start()

    def wait_all(first):
        for a in range(len(caches)):
            copy(0, a, 0, 1, first).wait()

    @pl.when((step > 0) & (q == 1))
    def _():
        wait_all(True)

    @pl.when((step > 0) & (q != 1))
    def _():
        wait_all(False)

    @pl.when(q == 0)
    def _():
        start_all(True)

    @pl.when(q != 0)
    def _():
        start_all(False)

    @pl.when(step == pl.num_programs(0) * nq - 1)
    def _():
        wait_all(False)


def _project_prompt_kv(x, w, bsz, lp):
    nq = next(c for c in (4, 3, 2) if lp % c == 0 and (lp // c) % SUBLANES == 0 and lp // c > PAD)
    rows = lp // nq
    lt = lp - PAD
    blk = lambda wd: pl.BlockSpec((rows, wd), lambda i, j: (i * nq + j, 0))
    pad_out = jax.ShapeDtypeStruct((bsz * lp, W_HALF), BF16)
    cache_out = jax.ShapeDtypeStruct((bsz * lt, W_HALF), F32)
    return pl.pallas_call(
        functools.partial(_proj_kv_kernel, rows=rows, nq=nq, lt=lt),
        grid=(bsz, nq),
        in_specs=[blk(x.shape[1]), pl.BlockSpec(w.shape, lambda i, j: (0, 0))],
        out_specs=[blk(W_HALF)] * 6 + [pl.BlockSpec(memory_space=pl.ANY)] * 4,
        out_shape=[pad_out] * 6 + [cache_out] * 4,
        scratch_shapes=[pltpu.VMEM((2, 4, rows, W_HALF), F32), pltpu.SemaphoreType.DMA(())],
        compiler_params=_params("arbitrary", "arbitrary"),
        name="in_proj_kv",
    )(x, w)


def _s5_kernel(u_ref, x0r_ref, x0i_ref, lr_ref, li_ref, bre_ref, bim_ref, cre_ref, cim_ref, d_ref,
               wglu_ref, y_ref, fr_ref, fi_ref, xr_s, xi_s, sr_s, si_s, *, tc, sg, precise):
    c = pl.program_id(1)

    @pl.when(c == 0)
    def _():
        xr_s[...] = x0r_ref[...]
        xi_s[...] = x0i_ref[...]

    def mm(a, b):
        if precise:
            return jnp.dot(a, b, precision=HIGHEST, preferred_element_type=F32)
        return jnp.dot(a.astype(BF16), b, preferred_element_type=F32)

    cw, sw = W_HALF // S5_BLOCKS, S5_WIDTH // S5_BLOCKS
    for s in range(sg):
        u = u_ref[s]
        sr_s[s] = jnp.concatenate(
            [mm(u[:, b * cw:(b + 1) * cw], bre_ref[b]) for b in range(S5_BLOCKS)], axis=1)
        si_s[s] = jnp.concatenate(
            [mm(u[:, b * cw:(b + 1) * cw], bim_ref[b]) for b in range(S5_BLOCKS)], axis=1)
    lr = lr_ref[...]
    li = li_ref[...]

    def step(t, carry):
        out = []
        for s, (xr, xi) in enumerate(carry):
            nr = lr * xr - li * xi + sr_s[s, pl.ds(t, 1), :]
            ni = lr * xi + li * xr + si_s[s, pl.ds(t, 1), :]
            sr_s[s, pl.ds(t, 1), :] = nr
            si_s[s, pl.ds(t, 1), :] = ni
            out.append((nr, ni))
        return tuple(out)

    init = tuple((xr_s[pl.ds(s, 1), :], xi_s[pl.ds(s, 1), :]) for s in range(sg))
    last = lax.fori_loop(0, tc, step, init, unroll=2)
    xr = jnp.concatenate([x[0] for x in last], axis=0)
    xi = jnp.concatenate([x[1] for x in last], axis=0)
    xr_s[...] = xr
    xi_s[...] = xi
    for s in range(sg):
        y = jnp.concatenate(
            [mm(sr_s[s, :, b * sw:(b + 1) * sw], cre_ref[b]) - mm(si_s[s, :, b * sw:(b + 1) * sw], cim_ref[b])
             for b in range(S5_BLOCKS)], axis=1) + d_ref[...] * u_ref[s]
        y = jax.nn.gelu(y)
        y = y * jax.nn.sigmoid(mm(y, wglu_ref[...]))
        y_ref[s] = y.astype(y_ref.dtype)

    @pl.when(c == pl.num_programs(1) - 1)
    def _():
        fr_ref[...] = xr
        fi_ref[...] = xi


def _s5_mixer(u, x0r, x0i, p, tc, precise):
    b, l, _ = u.shape
    nc = l // tc
    sg = math.gcd(b, SUBLANES)
    wdt = F32 if precise else BF16
    full = lambda a: pl.BlockSpec(a.shape, lambda i, j: (0,) * a.ndim)
    st_spec = pl.BlockSpec((sg, S5_WIDTH), lambda i, j: (i, 0))
    seq_spec = pl.BlockSpec((sg, tc, W_HALF), lambda i, j: (i, j, 0))
    mats = [p["bre"].astype(wdt), p["bim"].astype(wdt), p["cre"].astype(wdt), p["cim"].astype(wdt)]
    wglu = p["wglu"].astype(wdt)
    return pl.pallas_call(
        functools.partial(_s5_kernel, tc=tc, sg=sg, precise=precise),
        grid=(b // sg, nc),
        in_specs=[seq_spec, st_spec, st_spec,
                  full(p["lr"]), full(p["li"])] + [full(m) for m in mats] + [full(p["d"]), full(wglu)],
        out_specs=[seq_spec, st_spec, st_spec],
        out_shape=[jax.ShapeDtypeStruct((b, l, W_HALF), BF16),
                   jax.ShapeDtypeStruct((b, S5_WIDTH), F32),
                   jax.ShapeDtypeStruct((b, S5_WIDTH), F32)],
        scratch_shapes=[pltpu.VMEM((sg, S5_WIDTH), F32), pltpu.VMEM((sg, S5_WIDTH), F32),
                        pltpu.VMEM((sg, tc, S5_WIDTH), F32), pltpu.VMEM((sg, tc, S5_WIDTH), F32)],
        compiler_params=_params("parallel", "arbitrary"),
        name="s5_mixer",
    )(u, x0r, x0i, p["lr"], p["li"], *mats, p["d"], wglu)


def _s5_prepare(lam_re, lam_im, log_dt, b_re, b_im, c_re, c_im, d, w_glu):
    dt = jnp.exp(log_dt)[:, None]
    mag = jnp.exp(lam_re * dt)
    lb_re = mag * jnp.cos(lam_im * dt)
    lb_im = mag * jnp.sin(lam_im * dt)
    nr, ni = lb_re - 1.0, lb_im
    den = jnp.square(lam_re) + jnp.square(lam_im)
    cr = (nr * lam_re + ni * lam_im) / den
    ci = (ni * lam_re - nr * lam_im) / den
    bb_re = cr[..., None] * b_re - ci[..., None] * b_im
    bb_im = cr[..., None] * b_im + ci[..., None] * b_re
    gb = S5_GROUPS // S5_BLOCKS
    eye = jnp.eye(gb, dtype=F32)
    per_blk = lambda m: m.reshape((S5_BLOCKS, gb) + m.shape[1:])
    blk_in = lambda m: jnp.einsum("bgpi,gh->bgihp", per_blk(m), eye).reshape(
        S5_BLOCKS, W_HALF // S5_BLOCKS, S5_WIDTH // S5_BLOCKS)
    blk_out = lambda m: jnp.einsum("bgop,gh->bgpho", per_blk(m), eye).reshape(
        S5_BLOCKS, S5_WIDTH // S5_BLOCKS, W_HALF // S5_BLOCKS)
    return dict(lr=lb_re.reshape(1, S5_WIDTH), li=lb_im.reshape(1, S5_WIDTH),
                bre=blk_in(bb_re), bim=blk_in(bb_im), cre=blk_out(c_re), cim=blk_out(c_im),
                d=d.reshape(1, W_HALF), wglu=w_glu)


def _ret_kernel(q_ref, k_ref, v_ref, g_ref, cos_ref, sin_ref, dmat_ref, inter_ref, zeta_ref, gc_ref,
                swap_ref, avg_ref, r0_ref, y_ref, rf_ref, r_s, *, tc):
    c = pl.program_id(1)

    @pl.when(c == 0)
    def _():
        r_s[...] = r0_ref[0]

    cos = cos_ref[...]
    sin = sin_ref[...]

    def split(x):
        hi = x.astype(BF16)
        return hi, (x - hi.astype(F32)).astype(BF16)

    def rot(x):
        partner = jnp.dot(x.astype(BF16), swap_ref[...], preferred_element_type=F32)
        return x * cos + partner * sin

    def head_mean(x):
        hi, lo = split(x)
        return (jnp.dot(hi, avg_ref[...], preferred_element_type=F32)
                + jnp.dot(lo, avg_ref[...], preferred_element_type=F32))

    q = rot(q_ref[...])
    k = rot(k_ref[...]) * (RET_DK ** -0.5)
    v = v_ref[...]
    vz = v * zeta_ref[...]
    g = g_ref[...]
    gate = g * jax.nn.sigmoid(g)
    heads = [slice(h * RET_DK, (h + 1) * RET_DK) for h in range(RET_HEADS)]
    qb = q.astype(BF16)
    kb = k.astype(BF16)
    vb = v.astype(BF16)
    vzb = vz.astype(BF16)
    kt = k.T.astype(BF16)
    scores = [lax.dot_general(qb[:, sl], kb[:, sl], NT_DIMS, preferred_element_type=F32) for sl in heads]
    cross = [jnp.dot(qb[:, sl], r_s[h].astype(BF16), preferred_element_type=F32)
             for h, sl in enumerate(heads)]
    new_r = [gc_ref[h] * r_s[h] + jnp.dot(kt[sl, :], vzb[:, sl], preferred_element_type=F32)
             for h, sl in enumerate(heads)]
    decayed = [(scores[h] * dmat_ref[h]).astype(BF16) for h in range(RET_HEADS)]
    outs = [jnp.dot(decayed[h], vb[:, sl], preferred_element_type=F32) + cross[h] * inter_ref[:, sl]
            for h, sl in enumerate(heads)]
    for h in range(RET_HEADS):
        r_s[h] = new_r[h]
    o = jnp.concatenate(outs, axis=1)
    dlt = o - head_mean(o)
    var = head_mean(dlt * dlt)
    y_ref[...] = (dlt * lax.rsqrt(var + EPS) * gate).astype(y_ref.dtype)

    @pl.when(c == pl.num_programs(1) - 1)
    def _():
        rf_ref[0] = r_s[...]


def _ret_mixer(q, k, v, g, r0, pos, tc):
    b, l, _ = q.shape
    nc = l // tc
    inv = 1.0 / (10000.0 ** jnp.linspace(0.0, 1.0, RET_DK // 2, dtype=F32))
    ang = pos.astype(F32)[:, None] * inv[None]
    cos = jnp.tile(jnp.concatenate([jnp.cos(ang), jnp.cos(ang)], 1), (1, RET_HEADS))
    sin = jnp.tile(jnp.concatenate([-jnp.sin(ang), jnp.sin(ang)], 1), (1, RET_HEADS))
    log_gamma = jnp.log(1.0 - 2.0 ** (-5.0 - jnp.arange(RET_HEADS, dtype=F32)))
    n = jnp.arange(tc, dtype=F32)
    diff = n[:, None] - n[None, :]
    causal = diff >= 0
    dmat = jnp.where(causal[None], jnp.exp(jnp.where(causal, diff, 0.0)[None] * log_gamma[:, None, None]), 0.0)
    per_lane = lambda m: jnp.repeat(m, RET_DK, axis=1)
    inter = per_lane(jnp.exp((n[:, None] + 1.0) * log_gamma[None]))
    zeta = per_lane(jnp.exp((tc - 1.0 - n)[:, None] * log_gamma[None]))
    gc = jnp.broadcast_to(jnp.exp(tc * log_gamma)[:, None, None], (RET_HEADS, RET_DK, RET_DK))
    lane = jnp.arange(W_HALF)
    partner = jnp.where(lane % RET_DK < RET_DK // 2, lane + RET_DK // 2, lane - RET_DK // 2)
    swap = (lane[:, None] == partner[None, :]).astype(BF16)
    avg = ((lane[:, None] // RET_DK == lane[None, :] // RET_DK) / RET_DK).astype(BF16)
    seq = pl.BlockSpec((None, tc, W_HALF), lambda i, j: (i, j, 0))
    tab = pl.BlockSpec((tc, W_HALF), lambda i, j: (j, 0))
    full = lambda a: pl.BlockSpec(a.shape, lambda i, j: (0,) * a.ndim)
    st = pl.BlockSpec((1, RET_HEADS, RET_DK, RET_DK), lambda i, j: (i, 0, 0, 0))
    return pl.pallas_call(
        functools.partial(_ret_kernel, tc=tc),
        grid=(b, nc),
        in_specs=[seq, seq, seq, seq, tab, tab, full(dmat), full(inter), full(zeta), full(gc),
                  full(swap), full(avg), st],
        out_specs=[seq, st],
        out_shape=[jax.ShapeDtypeStruct((b, l, W_HALF), BF16),
                   jax.ShapeDtypeStruct((b, RET_HEADS, RET_DK, RET_DK), F32)],
        scratch_shapes=[pltpu.VMEM((RET_HEADS, RET_DK, RET_DK), F32)],
        compiler_params=_params("parallel", "arbitrary"),
        name="ret_mixer",
    )(q, k, v, g, cos, sin, dmat, inter, zeta, gc, swap, avg, r0)


def _attn_kernel(lq1_ref, lk1_ref, lq2_ref, lk2_ref, ng_ref, qs_ref, qd_ref, ks_ref, vst_ref, kd_ref,
                 vdt_ref, osb_ref, od_ref, *, bq, q_base, kv_lo, n_key_blocks, lam_init):
    qi = pl.program_id(1)
    q0 = q_base + qi * bq
    nkb = jnp.minimum(jnp.right_shift(q0 + bq + KEY_BLOCK - 1, KEY_SHIFT), n_key_blocks)
    qpos = q0 + lax.broadcasted_iota(I32, (KEY_BLOCK, bq), 1)
    koff = lax.broadcasted_iota(I32, (KEY_BLOCK, bq), 0)
    qchunk = jnp.right_shift(qpos, CHUNK_SHIFT)
    row = lax.broadcasted_iota(I32, (KEY_BLOCK, KEY_BLOCK), 0)
    col = lax.broadcasted_iota(I32, (KEY_BLOCK, KEY_BLOCK), 1)
    newer_keys = (col > row).astype(BF16)

    qs = (qs_ref[...].astype(F32) * (SB_DH ** -0.5)).astype(BF16)
    qd = (qd_ref[...].astype(F32) * (DIFF_DH ** -0.5)).astype(BF16)
    sb_q = [qs[:, h * SB_DH:(h + 1) * SB_DH] for h in range(SB_HEADS)]
    d_q = [qd[:, c * DIFF_DH:(c + 1) * DIFF_DH] for c in range(2 * DIFF_HEADS)]

    def body(t, carry, edge, nblk, j0):
        sb_acc, sb_tail, d_m, d_l, d_a = carry
        blocks = range(nblk)
        js = [j0 + nblk * t + i for i in blocks]
        kbs = [nkb - 1 - j for j in js]
        kblks = [ks_ref[kb] for kb in kbs]
        dblks = [kd_ref[j] for j in js]
        if edge:
            masks, dmasks = [], []
            for i in blocks:
                kpos = kbs[i] * KEY_BLOCK + koff
                masks.append((kpos >= kv_lo) & (kpos < qpos))
                dpos = js[i] * KEY_BLOCK + koff
                dmasks.append((dpos >= kv_lo) & (jnp.right_shift(dpos, CHUNK_SHIFT) <= qchunk))
            keep = lambda m, x, other: jnp.where(m, x, other)
        else:
            masks = dmasks = [None] * nblk
            keep = lambda m, x, other: x
        zs = [[lax.dot_general(kblks[i][:, h * SB_DH:(h + 1) * SB_DH], sb_q[h], NT_DIMS,
                               preferred_element_type=F32) for h in range(SB_HEADS)] for i in blocks]
        ss = [[lax.dot_general(dblks[i][:, c * DIFF_DH:(c + 1) * DIFF_DH], d_q[c], NT_DIMS,
                               preferred_element_type=F32) for c in range(2 * DIFF_HEADS)] for i in blocks]
        log_betas = [[None] * SB_HEADS for _ in blocks]
        splits = [[None] * SB_HEADS for _ in blocks]
        tails = [[None] * SB_HEADS for _ in blocks]
        new_tail = []
        for h in range(SB_HEADS):
            tail = sb_tail[h]
            for i in blocks:
                z = zs[i][h]
                sp = jnp.log(1.0 + jnp.exp(-jnp.abs(z)))
                log_betas[i][h] = jnp.minimum(z, 0.0) - sp
                cost = keep(masks[i], z - log_betas[i][h], 0.0)
                splits[i][h] = cost.astype(BF16)
                tails[i][h] = tail
                tail = tail + jnp.sum(cost, axis=0, keepdims=True)
            new_tail.append(tail)
        new_m, new_l, alphas = [], [], []
        probs = [[None] * (2 * DIFF_HEADS) for _ in blocks]
        for c in range(2 * DIFF_HEADS):
            s = [keep(dmasks[i], ss[i][c], NEG_BIG) for i in blocks]
            mn = d_m[c]
            for i in blocks:
                mn = jnp.maximum(mn, jnp.max(s[i], axis=0, keepdims=True))
            al = jnp.exp(d_m[c] - mn)
            l = al * d_l[c]
            for i in blocks:
                p = jnp.exp(s[i] - mn)
                l = l + jnp.sum(p, axis=0, keepdims=True)
                probs[i][c] = p.astype(BF16)
            new_m.append(mn)
            new_l.append(l)
            alphas.append(al)
        newer = [[jnp.dot(newer_keys, splits[i][h], preferred_element_type=F32)
                  for h in range(SB_HEADS)] for i in blocks]
        new_a = []
        for c in range(2 * DIFF_HEADS):
            a = alphas[c] * d_a[c]
            vsl = slice((c // 2) * 2 * DIFF_DH, (c // 2 + 1) * 2 * DIFF_DH)
            for i in blocks:
                a = a + jnp.dot(vdt_ref[js[i], vsl, :], probs[i][c], preferred_element_type=F32)
            new_a.append(a)
        ws = [[keep(masks[i], jnp.exp(log_betas[i][h] - newer[i][h] - tails[i][h]), 0.0).astype(BF16)
               for h in range(SB_HEADS)] for i in blocks]
        new_acc = []
        for h in range(SB_HEADS):
            acc = sb_acc[h]
            for i in blocks:
                acc = acc + jnp.dot(vst_ref[kbs[i], h * SB_DH:(h + 1) * SB_DH, :], ws[i][h],
                                    preferred_element_type=F32)
            new_acc.append(acc)
        return tuple(new_acc), tuple(new_tail), tuple(new_m), tuple(new_l), tuple(new_a)

    zrow = jnp.zeros((1, bq), F32)
    init = (tuple(jnp.zeros((SB_DH, bq), F32) for _ in range(SB_HEADS)),
            tuple(zrow for _ in range(SB_HEADS)),
            tuple(jnp.full((1, bq), NEG_BIG, F32) for _ in range(2 * DIFF_HEADS)),
            tuple(zrow for _ in range(2 * DIFF_HEADS)),
            tuple(jnp.zeros((2 * DIFF_DH, bq), F32) for _ in range(2 * DIFF_HEADS)))
    edge_body = functools.partial(body, edge=True, nblk=1, j0=0)
    n_pairs = jnp.right_shift(jnp.maximum(nkb - 2, 0), 1)
    carry = lax.fori_loop(0, jnp.minimum(nkb, 1), edge_body, init)
    carry = lax.fori_loop(0, n_pairs, functools.partial(body, edge=False, nblk=2, j0=1), carry)
    carry = lax.fori_loop(1 + 2 * n_pairs, nkb - 1, functools.partial(body, edge=False, nblk=1, j0=0), carry)
    sb_acc, _, _, d_l, d_a = lax.fori_loop(jnp.maximum(nkb - 1, 1), nkb, edge_body, carry)

    osb_ref[...] = jnp.concatenate(sb_acc, axis=0).T.astype(osb_ref.dtype)
    lam = (jnp.exp(jnp.sum(lq1_ref[...] * lk1_ref[...], axis=1, keepdims=True))
           - jnp.exp(jnp.sum(lq2_ref[...] * lk2_ref[...], axis=1, keepdims=True)) + lam_init)
    outs = []
    for h in range(DIFF_HEADS):
        o = d_a[2 * h] / d_l[2 * h] - lam * (d_a[2 * h + 1] / d_l[2 * h + 1])
        o = o * lax.rsqrt(jnp.mean(o * o, axis=0, keepdims=True) + EPS) * ng_ref[...]
        outs.append(o * (1.0 - lam_init))
    od_ref[...] = jnp.concatenate(outs, axis=0).T.astype(od_ref.dtype)


def _attention(qs, qd, ks, vs, kd, vd, lam_vecs, norm_g, *, bq, q_base, kv_lo, lam_init):
    b, lq, _ = qs.shape
    lk = ks.shape[1]
    nkb = lk // KEY_BLOCK
    assert bq == KEY_BLOCK and q_base % KEY_BLOCK == 0 and 0 <= kv_lo < KEY_BLOCK
    kblocks = lambda a: a.reshape(b, nkb, KEY_BLOCK, W_HALF)
    vblocks_t = lambda a: a.reshape(b, nkb, KEY_BLOCK, W_HALF).transpose(0, 1, 3, 2)
    qspec = pl.BlockSpec((None, bq, W_HALF), lambda i, j: (i, j, 0))
    kspec = pl.BlockSpec((None, nkb, KEY_BLOCK, W_HALF), lambda i, j: (i, 0, 0, 0))
    vspec = pl.BlockSpec((None, nkb, W_HALF, KEY_BLOCK), lambda i, j: (i, 0, 0, 0))
    vec = pl.BlockSpec((1, DIFF_DH), lambda i, j: (0, 0))
    return pl.pallas_call(
        functools.partial(_attn_kernel, bq=bq, q_base=q_base, kv_lo=kv_lo,
                          n_key_blocks=nkb, lam_init=lam_init),
        grid=(b, lq // bq),
        in_specs=[vec, vec, vec, vec, pl.BlockSpec((2 * DIFF_DH, 1), lambda i, j: (0, 0)),
                  qspec, qspec, kspec, vspec, kspec, vspec],
        out_specs=[qspec, qspec],
        out_shape=[jax.ShapeDtypeStruct((b, lq, W_HALF), BF16), jax.ShapeDtypeStruct((b, lq, W_HALF), BF16)],
        compiler_params=_params("parallel", "arbitrary"),
        name="sb_diff_attention",
    )(*lam_vecs, norm_g, qs, qd, kblocks(ks), vblocks_t(vs), kblocks(kd), vblocks_t(vd))


def _outproj_ln_kernel(yap_ref, ybp_ref, yas_ref, ybs_ref, h_ref, w_ref, g_ref, b_ref, o_ref, *,
                       alpha, prompt_blocks):
    is_prompt = pl.program_id(0) < prompt_blocks
    ya = jnp.where(is_prompt, yap_ref[...], yas_ref[...])
    yb = jnp.where(is_prompt, ybp_ref[...], ybs_ref[...])
    m = (jnp.dot(ya, w_ref[:W_HALF, :], preferred_element_type=F32)
         + jnp.dot(yb, w_ref[W_HALF:, :], preferred_element_type=F32))
    o_ref[...] = _layer_norm(alpha * h_ref[...] + m, g_ref[...], b_ref[...])


def _outproj_ln(ya_p, yb_p, ya_s, yb_s, h, w, g, b, alpha):
    n = h.shape[0]
    n_p, n_s = ya_p.shape[0], ya_s.shape[0]
    tm = _tile(math.gcd(n_p, n_s))
    pb = n_p // tm
    row = pl.BlockSpec((tm, D_MODEL), lambda i: (i, 0))
    prompt = pl.BlockSpec((tm, W_HALF), lambda i: (jnp.minimum(i, pb - 1), 0))
    sample = pl.BlockSpec((tm, W_HALF), lambda i: (jnp.maximum(i - pb, 0), 0))
    full = lambda a: pl.BlockSpec(a.shape, lambda i: (0, 0))
    return pl.pallas_call(
        functools.partial(_outproj_ln_kernel, alpha=alpha, prompt_blocks=pb),
        grid=(n // tm,),
        in_specs=[prompt, prompt, sample, sample, row, full(w), full(g), full(b)],
        out_specs=row,
        out_shape=jax.ShapeDtypeStruct((n, D_MODEL), F32),
        compiler_params=_params("parallel"),
        name="out_proj_ln",
    )(ya_p, yb_p, ya_s, yb_s, h, w, g, b)


def _router_kernel(x_ref, wt_ref, bias_ref, eidx_ref, gate_ref, rank_ref, cnt_ref, run_s, *, tm):
    i = pl.program_id(0)

    @pl.when(i == 0)
    def _():
        run_s[...] = jnp.zeros_like(run_s)

    logits = lax.dot_general(wt_ref[...], x_ref[...], NT_DIMS, precision=HIGHEST,
                             preferred_element_type=F32)
    scores = jax.nn.sigmoid(logits)
    sel = scores + bias_ref[...]
    ninf = -jnp.inf

    sel3 = sel.reshape(N_GROUPS, GROUP_SIZE, tm)
    within = lax.broadcasted_iota(I32, (N_GROUPS, GROUP_SIZE, tm), 1)
    m1 = jnp.max(sel3, axis=1, keepdims=True)
    first = jnp.min(jnp.where(sel3 == m1, within, GROUP_SIZE), axis=1, keepdims=True)
    m2 = jnp.max(jnp.where(within == first, ninf, sel3), axis=1, keepdims=True)
    gscore = (m1 + m2).reshape(N_GROUPS, tm)

    gid = lax.broadcasted_iota(I32, (N_GROUPS, tm), 0)
    gsel = jnp.zeros((N_GROUPS, tm), jnp.bool_)
    cur = gscore
    for _ in range(TOPK_GROUPS):
        m = jnp.max(cur, axis=0, keepdims=True)
        f = jnp.min(jnp.where(cur == m, gid, N_GROUPS), axis=0, keepdims=True)
        pick = gid == f
        gsel = gsel | pick
        cur = jnp.where(pick, ninf, cur)
    emask = jnp.broadcast_to(gsel.reshape(N_GROUPS, 1, tm), (N_GROUPS, GROUP_SIZE, tm)).reshape(N_EXPERTS, tm)

    eid = lax.broadcasted_iota(I32, (N_EXPERTS, tm), 0)
    cur = jnp.where(emask, sel, ninf)
    picks, idxs, gates = [], [], []
    for _ in range(TOP_K):
        m = jnp.max(cur, axis=0, keepdims=True)
        f = jnp.min(jnp.where(cur == m, eid, N_EXPERTS), axis=0, keepdims=True)
        pick = eid == f
        picks.append(pick)
        idxs.append(f)
        gates.append(jnp.sum(jnp.where(pick, scores, 0.0), axis=0, keepdims=True))
        cur = jnp.where(pick, ninf, cur)
    gsum = gates[0]
    for gk in gates[1:]:
        gsum = gsum + gk

    chosen = picks[0]
    for pk in picks[1:]:
        chosen = chosen | pk
    chosen_f = chosen.astype(F32)
    r_ = lax.broadcasted_iota(I32, (tm, tm), 0)
    c_ = lax.broadcasted_iota(I32, (tm, tm), 1)
    earlier = (r_ < c_).astype(BF16)
    before = run_s[...] + jnp.dot(chosen_f.astype(BF16), earlier, preferred_element_type=F32)
    zrow_i = jnp.zeros((1, tm), I32)
    zrow_f = jnp.zeros((1, tm), F32)
    ranks = [jnp.sum(jnp.where(pk, before, 0.0), axis=0, keepdims=True).astype(I32) for pk in picks]
    eidx_ref[...] = jnp.concatenate(idxs + [zrow_i, zrow_i], axis=0)
    gate_ref[...] = jnp.concatenate([gk / gsum * ROUTED_SCALE for gk in gates] + [zrow_f, zrow_f], axis=0)
    rank_ref[...] = jnp.concatenate(ranks + [zrow_i, zrow_i], axis=0)
    run_s[...] = run_s[...] + jnp.sum(chosen_f, axis=1, keepdims=True)
    cnt_ref[...] = jnp.broadcast_to(run_s[...], cnt_ref.shape)


def _router(x, wt, bias):
    n = x.shape[0]
    tm = _tile(n, (512, 256, 128))
    lane_blk = pl.BlockSpec((8, tm), lambda i: (0, i))
    return pl.pallas_call(
        functools.partial(_router_kernel, tm=tm),
        grid=(n // tm,),
        in_specs=[pl.BlockSpec((tm, D_MODEL), lambda i: (i, 0)),
                  pl.BlockSpec(wt.shape, lambda i: (0, 0)),
                  pl.BlockSpec(bias.shape, lambda i: (0, 0))],
        out_specs=[lane_blk, lane_blk, lane_blk, pl.BlockSpec((N_EXPERTS, 128), lambda i: (0, 0))],
        out_shape=[jax.ShapeDtypeStruct((8, n), I32), jax.ShapeDtypeStruct((8, n), F32),
                   jax.ShapeDtypeStruct((8, n), I32), jax.ShapeDtypeStruct((N_EXPERTS, 128), F32)],
        scratch_shapes=[pltpu.VMEM((N_EXPERTS, 1), F32)],
        compiler_params=_params("arbitrary"),
        name="moe_router",
    )(x, wt, bias)


ROW_SUB = D_MODEL // 2 // LANES
HIGH_HALF = -65536
DMA_PRIORITIES = 2


def _pack_rows(dst_ref, x, r0=0):
    rows = x.shape[0]
    bits = lambda v: lax.bitcast_convert_type(v.astype(BF16).astype(F32), I32)
    w = (lax.shift_right_logical(bits(x[:, :D_MODEL // 2]), jnp.full((), 16, I32))
         | (bits(x[:, D_MODEL // 2:]) & HIGH_HALF))
    for c in range(ROW_SUB):
        dst_ref[pl.ds(r0 * ROW_SUB + c, rows, stride=ROW_SUB), :] = w[:, c * LANES:(c + 1) * LANES]


def _unpack_rows(src_ref, rows, r0=0):
    ws = [src_ref[pl.ds(r0 * ROW_SUB + c, rows, stride=ROW_SUB), :] for c in range(ROW_SUB)]
    lo = [lax.bitcast_convert_type(lax.shift_left(w, jnp.full((), 16, I32)), F32) for w in ws]
    hi = [lax.bitcast_convert_type(w & HIGH_HALF, F32) for w in ws]
    return jnp.concatenate(lo + hi, axis=1)


def _dispatch_kernel(*refs, tm):
    n_idx = TOP_K * tm // LANES
    dest_refs, (x_ref, o_hbm, buf, sem) = refs[:n_idx], refs[n_idx:]
    i = pl.program_id(0)
    slot = i % 2
    stage = buf.at[slot]
    _pack_rows(stage, x_ref[...])

    def wait_all():
        for _ in range(TOP_K):
            pltpu.make_async_copy(stage, o_hbm.at[pl.ds(0, tm * ROW_SUB)], sem).wait()

    @pl.when(i > 0)
    def _():
        wait_all()

    for k in range(TOP_K):
        for blk in range(tm // LANES):
            def start(t, carry, k=k, blk=blk):
                for prio in range(DMA_PRIORITIES):
                    jj = t * DMA_PRIORITIES + prio
                    d = pl.multiple_of(dest_refs[k * (tm // LANES) + blk][0, 0, jj] * ROW_SUB, ROW_SUB)
                    s = pl.multiple_of((blk * LANES + jj) * ROW_SUB, ROW_SUB)
                    pltpu.make_async_copy(stage.at[pl.ds(s, ROW_SUB)], o_hbm.at[pl.ds(d, ROW_SUB)],
                                          sem).start(priority=prio)
                return carry
            lax.fori_loop(0, LANES // DMA_PRIORITIES, start, 0, unroll=4)

    @pl.when(i == pl.num_programs(0) - 1)
    def _():
        wait_all()


def _slot_spec():
    return pl.BlockSpec((1, 1, LANES), lambda i: (i, 0, 0), memory_space=pltpu.SMEM)


def _dispatch(x, dests, n_rows, tm):
    n = x.shape[0]
    return pl.pallas_call(
        functools.partial(_dispatch_kernel, tm=tm),
        grid=(n // tm,),
        in_specs=[_slot_spec() for _ in dests] + [pl.BlockSpec((tm, D_MODEL), lambda i: (i, 0))],
        out_specs=pl.BlockSpec(memory_space=pl.ANY),
        out_shape=jax.ShapeDtypeStruct((n_rows * ROW_SUB, LANES), I32),
        scratch_shapes=[pltpu.VMEM((2, tm * ROW_SUB, LANES), I32), pltpu.SemaphoreType.DMA(())],
        compiler_params=_params("arbitrary"),
        name="moe_dispatch",
    )(*dests, x)


def _ffn_kernel(be_ref, nv_ref, xs_ref, w1_ref, w3_ref, w2_ref, o_ref, wb1, wb3, wb2):
    i = pl.program_id(0)

    @pl.when((i == 0) | (be_ref[i] != be_ref[jnp.maximum(i - 1, 0)]))
    def _():
        wb1[...] = w1_ref[0].astype(BF16)
        wb3[...] = w3_ref[0].astype(BF16)
        wb2[...] = w2_ref[0].astype(BF16)

    x = _unpack_rows(xs_ref, EXPERT_BLOCK)
    live = lax.broadcasted_iota(I32, (EXPERT_BLOCK, 1), 0) < nv_ref[i]
    x = jnp.where(live, x, 0.0).astype(BF16)
    a = jnp.dot(x, wb1[...], preferred_element_type=F32)
    b = jnp.dot(x, wb3[...], preferred_element_type=F32)
    hdn = (a * jax.nn.sigmoid(a) * b).astype(BF16)
    _pack_rows(o_ref, jnp.dot(hdn, wb2[...], preferred_element_type=F32))


def _expert_ffn(xs, blk_e, blk_valid, w1, w3, w2):
    nb = xs.shape[0] // (EXPERT_BLOCK * ROW_SUB)
    de = w1.shape[-1]
    rows = pl.BlockSpec((EXPERT_BLOCK * ROW_SUB, LANES), lambda i, be, nv: (i, 0))
    grid_spec = pltpu.PrefetchScalarGridSpec(
        num_scalar_prefetch=2,
        grid=(nb,),
        in_specs=[rows,
                  pl.BlockSpec((1, D_MODEL, de), lambda i, be, nv: (be[i], 0, 0)),
                  pl.BlockSpec((1, D_MODEL, de), lambda i, be, nv: (be[i], 0, 0)),
                  pl.BlockSpec((1, de, D_MODEL), lambda i, be, nv: (be[i], 0, 0))],
        out_specs=rows,
        scratch_shapes=[pltpu.VMEM((D_MODEL, de), BF16), pltpu.VMEM((D_MODEL, de), BF16),
                        pltpu.VMEM((de, D_MODEL), BF16)],
    )
    return pl.pallas_call(
        _ffn_kernel,
        grid_spec=grid_spec,
        out_shape=jax.ShapeDtypeStruct(xs.shape, I32),
        compiler_params=_params("arbitrary"),
        name="moe_expert_ffn",
    )(blk_e, blk_valid, xs, w1, w3, w2)


def _combine_kernel(*refs, tm, alpha):
    n_idx = TOP_K * tm // LANES
    dest_refs = refs[:n_idx]
    gate_ref, x_ref, ys_hbm, sw1_ref, sw3_ref, sw2_ref, g_ref, b_ref, o_ref, buf, sem = refs[n_idx:]
    for k in range(TOP_K):
        for blk in range(tm // LANES):
            def start(t, carry, k=k, blk=blk):
                for prio in range(DMA_PRIORITIES):
                    jj = t * DMA_PRIORITIES + prio
                    s = pl.multiple_of(dest_refs[k * (tm // LANES) + blk][0, 0, jj] * ROW_SUB, ROW_SUB)
                    d = pl.multiple_of((k * tm + blk * LANES + jj) * ROW_SUB, ROW_SUB)
                    pltpu.make_async_copy(ys_hbm.at[pl.ds(s, ROW_SUB)], buf.at[pl.ds(d, ROW_SUB)],
                                          sem.at[k]).start(priority=prio)
                return carry
            lax.fori_loop(0, LANES // DMA_PRIORITIES, start, 0, unroll=4)

    x = x_ref[...]
    xb = x.astype(BF16)
    a = jnp.dot(xb, sw1_ref[...], preferred_element_type=F32)
    b = jnp.dot(xb, sw3_ref[...], preferred_element_type=F32)
    shared = jnp.dot((a * jax.nn.sigmoid(a) * b).astype(BF16), sw2_ref[...], preferred_element_type=F32)

    gate = gate_ref[...]
    routed = shared
    for k in range(TOP_K):
        grp = pl.ds(k * tm * ROW_SUB, tm * ROW_SUB)
        pltpu.make_async_copy(ys_hbm.at[pl.ds(0, tm * ROW_SUB)], buf.at[grp], sem.at[k]).wait()
        routed = routed + gate[:, k:k + 1] * _unpack_rows(buf, tm, k * tm)
    o_ref[...] = _layer_norm(alpha * x + routed, g_ref[...], b_ref[...])


def _combine(dests, gate_t, x, ys, sw1, sw3, sw2, g, b, alpha, tm):
    n = x.shape[0]
    full = lambda a: pl.BlockSpec(a.shape, lambda i: (0,) * a.ndim)
    return pl.pallas_call(
        functools.partial(_combine_kernel, tm=tm, alpha=alpha),
        grid=(n // tm,),
        in_specs=[_slot_spec() for _ in dests] + [
                  pl.BlockSpec((tm, 8), lambda i: (i, 0)),
                  pl.BlockSpec((tm, D_MODEL), lambda i: (i, 0)),
                  pl.BlockSpec(memory_space=pl.ANY),
                  full(sw1), full(sw3), full(sw2), full(g), full(b)],
        out_specs=pl.BlockSpec((tm, D_MODEL), lambda i: (i, 0)),
        out_shape=jax.ShapeDtypeStruct((n, D_MODEL), F32),
        scratch_shapes=[pltpu.VMEM((TOP_K * tm * ROW_SUB, LANES), I32), pltpu.SemaphoreType.DMA((TOP_K,))],
        compiler_params=_params("arbitrary"),
        name="moe_combine_ln",
    )(*dests, gate_t, x, ys, sw1, sw3, sw2, g, b)


def _moe(x, rw_t, r_bias, w1, w3, w2, sw1, sw3, sw2, g, b, alpha):
    n = x.shape[0]
    eidx, gate, rank, cnt = _router(x, rw_t, r_bias)
    counts = cnt[:, 0].astype(I32)
    padded = (counts + EXPERT_BLOCK - 1) // EXPERT_BLOCK * EXPERT_BLOCK
    pad_end = jnp.cumsum(padded)
    pad_start = pad_end - padded
    onehot = eidx[:, :, None] == jnp.arange(N_EXPERTS, dtype=I32)[None, None, :]
    dest = jnp.sum(jnp.where(onehot, pad_start[None, None, :], 0), axis=-1) + rank
    n_rows = -(-(n * TOP_K + N_EXPERTS * (EXPERT_BLOCK - 1)) // EXPERT_BLOCK) * EXPERT_BLOCK
    nb = n_rows // EXPERT_BLOCK
    blk_lo = jnp.arange(nb, dtype=I32) * EXPERT_BLOCK
    blk_e = jnp.minimum(jnp.sum((pad_end[None, :] <= blk_lo[:, None]).astype(I32), axis=1), N_EXPERTS - 1)
    blk_valid = jnp.clip(jnp.take(pad_start + counts, blk_e) - blk_lo, 0, EXPERT_BLOCK).astype(I32)
    tm = _tile(n, (256, 128))
    runs = dest.reshape(8, n // tm, tm // LANES, 1, LANES)
    dests = [runs[k, :, r] for k in range(TOP_K) for r in range(tm // LANES)]
    xs = _dispatch(x, dests, n_rows, tm)
    ys = _expert_ffn(xs, blk_e, blk_valid, w1, w3, w2)
    return _combine(dests, gate.T, x, ys, sw1, sw3, sw2, g, b, alpha, tm)


def kernel(x_prompt, x_sample, cache_sb_k, cache_sb_v, cache_diff_k, cache_diff_v, state_s5_re, state_s5_im, state_ret, meta_tokens, w_in_e, s5_lam_re, s5_lam_im, s5_log_dt, s5_b_re, s5_b_im, s5_c_re, s5_c_im, s5_d, s5_w_glu, w_out_e, w_in_o, diff_lq1, diff_lk1, diff_lq2, diff_lk2, diff_norm_g, w_out_o, ln1_g, ln1_b, ln2_g, ln2_b, router_w, router_bias, exp_w1, exp_w3, exp_w2, sh_w1, sh_w3, sh_w2):
    bsz, seq, _ = x_prompt.shape
    bsz_s, t_s, _ = x_sample.shape
    past = cache_sb_k.shape[2]
    depth = ln1_g.shape[0]
    alpha = (2 * depth) ** 0.25
    lp = PAD + N_META + seq
    n_p = bsz * lp
    n_s = bsz_s * t_s
    lt = N_META + seq

    meta = jnp.broadcast_to(meta_tokens[None].astype(F32), (bsz, N_META, D_MODEL))
    hp = jnp.concatenate([jnp.zeros((bsz, PAD, D_MODEL), F32), meta, x_prompt], 1)
    h = jnp.concatenate([hp.reshape(n_p, D_MODEL), x_sample.reshape(n_s, D_MODEL)], 0)

    outs_p = {k: [] for k in ("sbk", "sbv", "dk", "dv", "s5r", "s5i", "ret")}
    outs_s = {k: [] for k in ("sbk", "sbv", "dk", "dv", "s5r", "s5i", "ret")}
    half = lambda c0, dt=F32: (c0 * W_HALF, W_HALF, dt)

    for layer in range(depth):
        i = layer // 2
        if layer % 2 == 0:
            w_in = w_in_e[i]
            splits = [half(c) for c in range(5)]
            u_p, q_p, k_p, v_p, g_p = _project(h, w_in.astype(BF16), splits, 0, n_p)
            u_s, q_s, k_s, v_s, g_s = _project(h, w_in, splits, n_p, n_s, precise=True)
            s5p = _s5_prepare(s5_lam_re[i], s5_lam_im[i], s5_log_dt[i], s5_b_re[i], s5_b_im[i],
                              s5_c_re[i], s5_c_im[i], s5_d[i], s5_w_glu[i])
            seq_p = lambda a: a.reshape(bsz, lp, W_HALF)
            seq_s = lambda a: a.reshape(bsz_s, t_s, W_HALF)
            z5 = jnp.zeros((bsz, S5_WIDTH), F32)
            y5_p, fr_p, fi_p = _s5_mixer(seq_p(u_p), z5, z5, s5p, Q_BLOCK, False)
            y5_s, fr_s, fi_s = _s5_mixer(seq_s(u_s), state_s5_re[i].reshape(bsz_s, S5_WIDTH).astype(F32),
                                         state_s5_im[i].reshape(bsz_s, S5_WIDTH).astype(F32), s5p, t_s, True)
            yr_p, rt_p = _ret_mixer(seq_p(q_p), seq_p(k_p), seq_p(v_p), seq_p(g_p),
                                    jnp.zeros((bsz, RET_HEADS, RET_DK, RET_DK), F32),
                                    jnp.arange(lp) - PAD, Q_BLOCK)
            yr_s, rt_s = _ret_mixer(seq_s(q_s), seq_s(k_s), seq_s(v_s), seq_s(g_s),
                                    state_ret[i].astype(F32), past + jnp.arange(t_s), t_s)
            mixed = (y5_p.reshape(n_p, W_HALF), yr_p.reshape(n_p, W_HALF),
                     y5_s.reshape(n_s, W_HALF), yr_s.reshape(n_s, W_HALF))
            w_out = w_out_e[i]
            st = lambda a, nb_: a.reshape(nb_, S5_GROUPS, S5_STATE)
            outs_p["s5r"].append(st(fr_p, bsz)); outs_p["s5i"].append(st(fi_p, bsz)); outs_p["ret"].append(rt_p)
            outs_s["s5r"].append(st(fr_s, bsz_s)); outs_s["s5i"].append(st(fi_s, bsz_s)); outs_s["ret"].append(rt_s)
        else:
            splits = [half(0, BF16), half(1), half(2), half(3, BF16), half(4), half(5),
                      half(1, BF16), half(2, BF16), half(4, BF16), half(5, BF16)]
            w_in = w_in_o[i].astype(BF16)
            (qs, qd, ksb, vsb, kdb, vdb, ksf, vsf, kdf, vdf) = _project_prompt_kv(h, w_in, bsz, lp)
            (qs_s, ksf_s, vsf_s, qd_s, kdf_s, vdf_s, ksb_s, vsb_s, kdb_s, vdb_s) = _project(h, w_in, splits, n_p, n_s)
            lam_init = 0.8 - 0.6 * math.exp(-0.3 * layer)
            lam_vecs = [v[i].reshape(1, DIFF_DH).astype(F32) for v in (diff_lq1, diff_lk1, diff_lq2, diff_lk2)]
            norm_g = diff_norm_g[i].reshape(2 * DIFF_DH, 1).astype(F32)
            seq_p = lambda a: a.reshape(bsz, lp, W_HALF)
            seq_s = lambda a: a.reshape(bsz_s, t_s, W_HALF)
            osb_p, od_p = _attention(seq_p(qs), seq_p(qd), seq_p(ksb), seq_p(vsb), seq_p(kdb), seq_p(vdb),
                                     lam_vecs, norm_g, bq=Q_BLOCK, q_base=0, kv_lo=PAD, lam_init=lam_init)
            bq_s = -(-t_s // Q_BLOCK) * Q_BLOCK
            lk_s = -(-(past + bq_s) // KEY_BLOCK) * KEY_BLOCK
            qpad = lambda a: jnp.pad(seq_s(a), ((0, 0), (0, bq_s - t_s), (0, 0)))

            def keys(cache, new):
                c = cache.reshape(bsz_s, past, W_HALF).astype(BF16)
                z = jnp.zeros((bsz_s, lk_s - past - t_s, W_HALF), BF16)
                return jnp.concatenate([c, seq_s(new), z], 1)
            osb_s, od_s = _attention(qpad(qs_s), qpad(qd_s), keys(cache_sb_k[i], ksb_s), keys(cache_sb_v[i], vsb_s),
                                     keys(cache_diff_k[i], kdb_s), keys(cache_diff_v[i], vdb_s),
                                     lam_vecs, norm_g, bq=bq_s, q_base=past, kv_lo=0, lam_init=lam_init)
            mixed = (osb_p.reshape(n_p, W_HALF), od_p.reshape(n_p, W_HALF),
                     osb_s[:, :t_s].reshape(n_s, W_HALF), od_s[:, :t_s].reshape(n_s, W_HALF))
            w_out = w_out_o[i]
            kv_p = lambda a, hh, dd: a.reshape(bsz, lt, hh, dd)
            kv_s = lambda a, hh, dd: a.reshape(bsz_s, t_s, hh, dd)
            outs_p["sbk"].append(kv_p(ksf, SB_HEADS, SB_DH)); outs_p["sbv"].append(kv_p(vsf, SB_HEADS, SB_DH))
            outs_p["dk"].append(kv_p(kdf, DIFF_HEADS, 2 * DIFF_DH)); outs_p["dv"].append(kv_p(vdf, DIFF_HEADS, 2 * DIFF_DH))
            outs_s["sbk"].append(kv_s(ksf_s, SB_HEADS, SB_DH)); outs_s["sbv"].append(kv_s(vsf_s, SB_HEADS, SB_DH))
            outs_s["dk"].append(kv_s(kdf_s, DIFF_HEADS, 2 * DIFF_DH)); outs_s["dv"].append(kv_s(vdf_s, DIFF_HEADS, 2 * DIFF_DH))

        row = lambda a: a[layer].reshape(1, D_MODEL).astype(F32)
        h = _outproj_ln(*mixed, h, w_out.astype(BF16), row(ln1_g), row(ln1_b), alpha)
        h = _moe(h, router_w[layer].T.astype(F32), router_bias[layer].reshape(N_EXPERTS, 1).astype(F32),
                 exp_w1[layer].astype(F32), exp_w3[layer].astype(F32), exp_w2[layer].astype(F32),
                 sh_w1[layer].astype(BF16), sh_w3[layer].astype(BF16), sh_w2[layer].astype(BF16),
                 row(ln2_g), row(ln2_b), alpha)

    y_prompt = h[:n_p].reshape(bsz, lp, D_MODEL)[:, PAD + N_META:]
    y_sample = h[n_p:].reshape(bsz_s, t_s, D_MODEL)
    order = ("sbk", "sbv", "dk", "dv", "s5r", "s5i", "ret")
    return ((y_prompt, y_sample) + tuple(jnp.stack(outs_p[k]) for k in order)
            + tuple(jnp.stack(outs_s[k]) for k in order))
```

```python
import functools
import math

import jax
import jax.numpy as jnp
from jax import lax
from jax.experimental import pallas as pl
from jax.experimental.pallas import tpu as pltpu

F32 = jnp.float32
BF16 = jnp.bfloat16
I32 = jnp.int32
HIGHEST = lax.Precision.HIGHEST

D_MODEL = 1024
W_HALF = 512
N_META = 16
Q_BLOCK = 128
PAD = Q_BLOCK - N_META
CHUNK = 64
CHUNK_SHIFT = 6
S5_GROUPS = 32
S5_GROUP_CH = 16
S5_STATE = 64
S5_WIDTH = S5_GROUPS * S5_STATE
S5_BLOCKS = 4
RET_HEADS = 8
RET_DK = 64
SB_HEADS = 8
SB_DH = 64
DIFF_HEADS = 4
DIFF_DH = 64
N_EXPERTS = 64
N_GROUPS = 8
GROUP_SIZE = N_EXPERTS // N_GROUPS
TOPK_GROUPS = 4
TOP_K = 6
ROUTED_SCALE = 2.5
EXPERT_BLOCK = 1024
EPS = 1e-5
NEG_BIG = -1e30
KEY_BLOCK = 128
KEY_SHIFT = 7
SUBLANES = 8
LANES = 128
VMEM_LIMIT = 56 * 1024 * 1024

NT_DIMS = (((1,), (1,)), ((), ()))


def _params(*sem):
    return pltpu.CompilerParams(dimension_semantics=sem, vmem_limit_bytes=VMEM_LIMIT)


def _tile(n, cands=(512, 256, 128, 64)):
    for c in cands:
        if n % c == 0:
            return c
    raise ValueError(f"no tile for {n}")


def _layer_norm(r, g, b):
    mu = jnp.mean(r, -1, keepdims=True)
    d = r - mu
    var = jnp.mean(d * d, -1, keepdims=True)
    return d * lax.rsqrt(var + EPS) * g + b


def _proj_kernel(x_ref, w_ref, *o_refs, cols, precise):
    x = x_ref[...]
    xb = x.astype(F32) if precise else x.astype(BF16)
    for o_ref, c0 in zip(o_refs, cols):
        wd = o_ref.shape[-1]
        w = w_ref[:, c0:c0 + wd]
        if precise:
            y = jnp.dot(xb, w, precision=HIGHEST, preferred_element_type=F32)
        else:
            y = jnp.dot(xb, w, preferred_element_type=F32)
        o_ref[...] = y.astype(o_ref.dtype)


def _project(x, w, outs, row0, n, precise=False):
    k = x.shape[1]
    tm = _tile(math.gcd(n, row0) if row0 else n)
    blk0 = row0 // tm
    cols = tuple(o[0] for o in outs)
    return pl.pallas_call(
        functools.partial(_proj_kernel, cols=cols, precise=precise),
        grid=(n // tm,),
        in_specs=[pl.BlockSpec((tm, k), lambda i: (i + blk0, 0)),
                  pl.BlockSpec(w.shape, lambda i: (0, 0))],
        out_specs=[pl.BlockSpec((tm, o[1]), lambda i: (i, 0)) for o in outs],
        out_shape=[jax.ShapeDtypeStruct((n, o[1]), o[2]) for o in outs],
        compiler_params=_params("parallel"),
        name="in_proj",
    )(x, w)


def _proj_kv_kernel(x_ref, w_ref, qs_ref, qd_ref, ks_ref, vs_ref, kd_ref, vd_ref,
                    ksf_hbm, vsf_hbm, kdf_hbm, vdf_hbm, stage, sem, *, rows, nq, lt):
    b, q = pl.program_id(0), pl.program_id(1)
    step = b * nq + q
    slot = step % 2
    caches = (ksf_hbm, vsf_hbm, kdf_hbm, vdf_hbm)
    xb = x_ref[...].astype(BF16)
    col = lambda c: jnp.dot(xb, w_ref[:, c * W_HALF:(c + 1) * W_HALF], preferred_element_type=F32)
    qs_ref[...] = col(0).astype(BF16)
    qd_ref[...] = col(3).astype(BF16)
    for a, (c, o_ref) in enumerate(zip((1, 2, 4, 5), (ks_ref, vs_ref, kd_ref, vd_ref))):
        y = col(c)
        o_ref[...] = y.astype(BF16)
        stage[slot, a] = y

    def copy(slot_, a, b_, q_, first):
        if first:
            return pltpu.make_async_copy(stage.at[slot_, a, pl.ds(PAD, rows - PAD)],
                                         caches[a].at[pl.ds(pl.multiple_of(b_ * lt, SUBLANES), rows - PAD)], sem)
        dst = pl.multiple_of(b_ * lt + q_ * rows - PAD, SUBLANES)
        return pltpu.make_async_copy(stage.at[slot_, a], caches[a].at[pl.ds(dst, rows)], sem)

    def start_all(first):
        for a in range(len(caches)):
            copy(slot, a, b, q, first).start()

    def wait_all(first):
        for a in range(len(caches)):
            copy(0, a, 0, 1, first).wait()

    @pl.when((step > 0) & (q == 1))
    def _():
        wait_all(True)

    @pl.when((step > 0) & (q != 1))
    def _():
        wait_all(False)

    @pl.when(q == 0)
    def _():
        start_all(True)

    @pl.when(q != 0)
    def _():
        start_all(False)

    @pl.when(step == pl.num_programs(0) * nq - 1)
    def _():
        wait_all(False)


def _project_prompt_kv(x, w, bsz, lp):
    nq = next(c for c in (4, 3, 2) if lp % c == 0 and (lp // c) % SUBLANES == 0 and lp // c > PAD)
    rows = lp // nq
    lt = lp - PAD
    blk = lambda wd: pl.BlockSpec((rows, wd), lambda i, j: (i * nq + j, 0))
    pad_out = jax.ShapeDtypeStruct((bsz * lp, W_HALF), BF16)
    cache_out = jax.ShapeDtypeStruct((bsz * lt, W_HALF), F32)
    return pl.pallas_call(
        functools.partial(_proj_kv_kernel, rows=rows, nq=nq, lt=lt),
        grid=(bsz, nq),
        in_specs=[blk(x.shape[1]), pl.BlockSpec(w.shape, lambda i, j: (0, 0))],
        out_specs=[blk(W_HALF)] * 6 + [pl.BlockSpec(memory_space=pl.ANY)] * 4,
        out_shape=[pad_out] * 6 + [cache_out] * 4,
        scratch_shapes=[pltpu.VMEM((2, 4, rows, W_HALF), F32), pltpu.SemaphoreType.DMA(())],
        compiler_params=_params("arbitrary", "arbitrary"),
        name="in_proj_kv",
    )(x, w)


def _s5_kernel(u_ref, x0r_ref, x0i_ref, lr_ref, li_ref, bre_ref, bim_ref, cre_ref, cim_ref, d_ref,
               wglu_ref, y_ref, fr_ref, fi_ref, xr_s, xi_s, sr_s, si_s, *, tc, sg, precise):
    c = pl.program_id(1)

    @pl.when(c == 0)
    def _():
        xr_s[...] = x0r_ref[...]
        xi_s[...] = x0i_ref[...]

    def mm(a, b):
        if precise:
            return jnp.dot(a, b, precision=HIGHEST, preferred_element_type=F32)
        return jnp.dot(a.astype(BF16), b, preferred_element_type=F32)

    cw, sw = W_HALF // S5_BLOCKS, S5_WIDTH // S5_BLOCKS
    for s in range(sg):
        u = u_ref[s]
        sr_s[s] = jnp.concatenate(
            [mm(u[:, b * cw:(b + 1) * cw], bre_ref[b]) for b in range(S5_BLOCKS)], axis=1)
        si_s[s] = jnp.concatenate(
            [mm(u[:, b * cw:(b + 1) * cw], bim_ref[b]) for b in range(S5_BLOCKS)], axis=1)
    lr = lr_ref[...]
    li = li_ref[...]

    def step(t, carry):
        out = []
        for s, (xr, xi) in enumerate(carry):
            nr = lr * xr - li * xi + sr_s[s, pl.ds(t, 1), :]
            ni = lr * xi + li * xr + si_s[s, pl.ds(t, 1), :]
            sr_s[s, pl.ds(t, 1), :] = nr
            si_s[s, pl.ds(t, 1), :] = ni
            out.append((nr, ni))
        return tuple(out)

    init = tuple((xr_s[pl.ds(s, 1), :], xi_s[pl.ds(s, 1), :]) for s in range(sg))
    last = lax.fori_loop(0, tc, step, init, unroll=2)
    xr = jnp.concatenate([x[0] for x in last], axis=0)
    xi = jnp.concatenate([x[1] for x in last], axis=0)
    xr_s[...] = xr
    xi_s[...] = xi
    for s in range(sg):
        y = jnp.concatenate(
            [mm(sr_s[s, :, b * sw:(b + 1) * sw], cre_ref[b]) - mm(si_s[s, :, b * sw:(b + 1) * sw], cim_ref[b])
             for b in range(S5_BLOCKS)], axis=1) + d_ref[...] * u_ref[s]
        y = jax.nn.gelu(y)
        y = y * jax.nn.sigmoid(mm(y, wglu_ref[...]))
        y_ref[s] = y.astype(y_ref.dtype)

    @pl.when(c == pl.num_programs(1) - 1)
    def _():
        fr_ref[...] = xr
        fi_ref[...] = xi


def _s5_mixer(u, x0r, x0i, p, tc, precise):
    b, l, _ = u.shape
    nc = l // tc
    sg = math.gcd(b, SUBLANES)
    wdt = F32 if precise else BF16
    full = lambda a: pl.BlockSpec(a.shape, lambda i, j: (0,) * a.ndim)
    st_spec = pl.BlockSpec((sg, S5_WIDTH), lambda i, j: (i, 0))
    seq_spec = pl.BlockSpec((sg, tc, W_HALF), lambda i, j: (i, j, 0))
    mats = [p["bre"].astype(wdt), p["bim"].astype(wdt), p["cre"].astype(wdt), p["cim"].astype(wdt)]
    wglu = p["wglu"].astype(wdt)
    return pl.pallas_call(
        functools.partial(_s5_kernel, tc=tc, sg=sg, precise=precise),
        grid=(b // sg, nc),
        in_specs=[seq_spec, st_spec, st_spec,
                  full(p["lr"]), full(p["li"])] + [full(m) for m in mats] + [full(p["d"]), full(wglu)],
        out_specs=[seq_spec, st_spec, st_spec],
        out_shape=[jax.ShapeDtypeStruct((b, l, W_HALF), BF16),
                   jax.ShapeDtypeStruct((b, S5_WIDTH), F32),
                   jax.ShapeDtypeStruct((b, S5_WIDTH), F32)],
        scratch_shapes=[pltpu.VMEM((sg, S5_WIDTH), F32), pltpu.VMEM((sg, S5_WIDTH), F32),
                        pltpu.VMEM((sg, tc, S5_WIDTH), F32), pltpu.VMEM((sg, tc, S5_WIDTH), F32)],
        compiler_params=_params("parallel", "arbitrary"),
        name="s5_mixer",
    )(u, x0r, x0i, p["lr"], p["li"], *mats, p["d"], wglu)


def _s5_prepare(lam_re, lam_im, log_dt, b_re, b_im, c_re, c_im, d, w_glu):
    dt = jnp.exp(log_dt)[:, None]
    mag = jnp.exp(lam_re * dt)
    lb_re = mag * jnp.cos(lam_im * dt)
    lb_im = mag * jnp.sin(lam_im * dt)
    nr, ni = lb_re - 1.0, lb_im
    den = jnp.square(lam_re) + jnp.square(lam_im)
    cr = (nr * lam_re + ni * lam_im) / den
    ci = (ni * lam_re - nr * lam_im) / den
    bb_re = cr[..., None] * b_re - ci[..., None] * b_im
    bb_im = cr[..., None] * b_im + ci[..., None] * b_re
    gb = S5_GROUPS // S5_BLOCKS
    eye = jnp.eye(gb, dtype=F32)
    per_blk = lambda m: m.reshape((S5_BLOCKS, gb) + m.shape[1:])
    blk_in = lambda m: jnp.einsum("bgpi,gh->bgihp", per_blk(m), eye).reshape(
        S5_BLOCKS, W_HALF // S5_BLOCKS, S5_WIDTH // S5_BLOCKS)
    blk_out = lambda m: jnp.einsum("bgop,gh->bgpho", per_blk(m), eye).reshape(
        S5_BLOCKS, S5_WIDTH // S5_BLOCKS, W_HALF // S5_BLOCKS)
    return dict(lr=lb_re.reshape(1, S5_WIDTH), li=lb_im.reshape(1, S5_WIDTH),
                bre=blk_in(bb_re), bim=blk_in(bb_im), cre=blk_out(c_re), cim=blk_out(c_im),
                d=d.reshape(1, W_HALF), wglu=w_glu)


def _ret_kernel(q_ref, k_ref, v_ref, g_ref, cos_ref, sin_ref, dmat_ref, inter_ref, zeta_ref, gc_ref,
                swap_ref, avg_ref, r0_ref, y_ref, rf_ref, r_s, *, tc):
    c = pl.program_id(1)

    @pl.when(c == 0)
    def _():
        r_s[...] = r0_ref[0]

    cos = cos_ref[...]
    sin = sin_ref[...]

    def split(x):
        hi = x.astype(BF16)
        return hi, (x - hi.astype(F32)).astype(BF16)

    def rot(x):
        partner = jnp.dot(x.astype(BF16), swap_ref[...], preferred_element_type=F32)
        return x * cos + partner * sin

    def head_mean(x):
        hi, lo = split(x)
        return (jnp.dot(hi, avg_ref[...], preferred_element_type=F32)
                + jnp.dot(lo, avg_ref[...], preferred_element_type=F32))

    q = rot(q_ref[...])
    k = rot(k_ref[...]) * (RET_DK ** -0.5)
    v = v_ref[...]
    vz = v * zeta_ref[...]
    g = g_ref[...]
    gate = g * jax.nn.sigmoid(g)
    heads = [slice(h * RET_DK, (h + 1) * RET_DK) for h in range(RET_HEADS)]
    qb = q.astype(BF16)
    kb = k.astype(BF16)
    vb = v.astype(BF16)
    vzb = vz.astype(BF16)
    kt = k.T.astype(BF16)
    scores = [lax.dot_general(qb[:, sl], kb[:, sl], NT_DIMS, preferred_element_type=F32) for sl in heads]
    cross = [jnp.dot(qb[:, sl], r_s[h].astype(BF16), preferred_element_type=F32)
             for h, sl in enumerate(heads)]
    new_r = [gc_ref[h] * r_s[h] + jnp.dot(kt[sl, :], vzb[:, sl], preferred_element_type=F32)
             for h, sl in enumerate(heads)]
    decayed = [(scores[h] * dmat_ref[h]).astype(BF16) for h in range(RET_HEADS)]
    outs = [jnp.dot(decayed[h], vb[:, sl], preferred_element_type=F32) + cross[h] * inter_ref[:, sl]
            for h, sl in enumerate(heads)]
    for h in range(RET_HEADS):
        r_s[h] = new_r[h]
    o = jnp.concatenate(outs, axis=1)
    dlt = o - head_mean(o)
    var = head_mean(dlt * dlt)
    y_ref[...] = (dlt * lax.rsqrt(var + EPS) * gate).astype(y_ref.dtype)

    @pl.when(c == pl.num_programs(1) - 1)
    def _():
        rf_ref[0] = r_s[...]


def _ret_mixer(q, k, v, g, r0, pos, tc):
    b, l, _ = q.shape
    nc = l // tc
    inv = 1.0 / (10000.0 ** jnp.linspace(0.0, 1.0, RET_DK // 2, dtype=F32))
    ang = pos.astype(F32)[:, None] * inv[None]
    cos = jnp.tile(jnp.concatenate([jnp.cos(ang), jnp.cos(ang)], 1), (1, RET_HEADS))
    sin = jnp.tile(jnp.concatenate([-jnp.sin(ang), jnp.sin(ang)], 1), (1, RET_HEADS))
    log_gamma = jnp.log(1.0 - 2.0 ** (-5.0 - jnp.arange(RET_HEADS, dtype=F32)))
    n = jnp.arange(tc, dtype=F32)
    diff = n[:, None] - n[None, :]
    causal = diff >= 0
    dmat = jnp.where(causal[None], jnp.exp(jnp.where(causal, diff, 0.0)[None] * log_gamma[:, None, None]), 0.0)
    per_lane = lambda m: jnp.repeat(m, RET_DK, axis=1)
    inter = per_lane(jnp.exp((n[:, None] + 1.0) * log_gamma[None]))
    zeta = per_lane(jnp.exp((tc - 1.0 - n)[:, None] * log_gamma[None]))
    gc = jnp.broadcast_to(jnp.exp(tc * log_gamma)[:, None, None], (RET_HEADS, RET_DK, RET_DK))
    lane = jnp.arange(W_HALF)
    partner = jnp.where(lane % RET_DK < RET_DK // 2, lane + RET_DK // 2, lane - RET_DK // 2)
    swap = (lane[:, None] == partner[None, :]).astype(BF16)
    avg = ((lane[:, None] // RET_DK == lane[None, :] // RET_DK) / RET_DK).astype(BF16)
    seq = pl.BlockSpec((None, tc, W_HALF), lambda i, j: (i, j, 0))
    tab = pl.BlockSpec((tc, W_HALF), lambda i, j: (j, 0))
    full = lambda a: pl.BlockSpec(a.shape, lambda i, j: (0,) * a.ndim)
    st = pl.BlockSpec((1, RET_HEADS, RET_DK, RET_DK), lambda i, j: (i, 0, 0, 0))
    return pl.pallas_call(
        functools.partial(_ret_kernel, tc=tc),
        grid=(b, nc),
        in_specs=[seq, seq, seq, seq, tab, tab, full(dmat), full(inter), full(zeta), full(gc),
                  full(swap), full(avg), st],
        out_specs=[seq, st],
        out_shape=[jax.ShapeDtypeStruct((b, l, W_HALF), BF16),
                   jax.ShapeDtypeStruct((b, RET_HEADS, RET_DK, RET_DK), F32)],
        scratch_shapes=[pltpu.VMEM((RET_HEADS, RET_DK, RET_DK), F32)],
        compiler_params=_params("parallel", "arbitrary"),
        name="ret_mixer",
    )(q, k, v, g, cos, sin, dmat, inter, zeta, gc, swap, avg, r0)


def _attn_kernel(lq1_ref, lk1_ref, lq2_ref, lk2_ref, ng_ref, qs_ref, qd_ref, ks_ref, vst_ref, kd_ref,
                 vdt_ref, osb_ref, od_ref, *, bq, q_base, kv_lo, n_key_blocks, lam_init):
    qi = pl.program_id(1)
    q0 = q_base + qi * bq
    nkb = jnp.minimum(jnp.right_shift(q0 + bq + KEY_BLOCK - 1, KEY_SHIFT), n_key_blocks)
    qpos = q0 + lax.broadcasted_iota(I32, (KEY_BLOCK, bq), 1)
    koff = lax.broadcasted_iota(I32, (KEY_BLOCK, bq), 0)
    qchunk = jnp.right_shift(qpos, CHUNK_SHIFT)
    row = lax.broadcasted_iota(I32, (KEY_BLOCK, KEY_BLOCK), 0)
    col = lax.broadcasted_iota(I32, (KEY_BLOCK, KEY_BLOCK), 1)
    newer_keys = (col > row).astype(BF16)

    qs = (qs_ref[...].astype(F32) * (SB_DH ** -0.5)).astype(BF16)
    qd = (qd_ref[...].astype(F32) * (DIFF_DH ** -0.5)).astype(BF16)
    sb_q = [qs[:, h * SB_DH:(h + 1) * SB_DH] for h in range(SB_HEADS)]
    d_q = [qd[:, c * DIFF_DH:(c + 1) * DIFF_DH] for c in range(2 * DIFF_HEADS)]

    def body(t, carry, edge, nblk, j0):
        sb_acc, sb_tail, d_m, d_l, d_a = carry
        blocks = range(nblk)
        js = [j0 + nblk * t + i for i in blocks]
        kbs = [nkb - 1 - j for j in js]
        kblks = [ks_ref[kb] for kb in kbs]
        dblks = [kd_ref[j] for j in js]
        if edge:
            masks, dmasks = [], []
            for i in blocks:
                kpos = kbs[i] * KEY_BLOCK + koff
                masks.append((kpos >= kv_lo) & (kpos < qpos))
                dpos = js[i] * KEY_BLOCK + koff
                dmasks.append((dpos >= kv_lo) & (jnp.right_shift(dpos, CHUNK_SHIFT) <= qchunk))
            keep = lambda m, x, other: jnp.where(m, x, other)
        else:
            masks = dmasks = [None] * nblk
            keep = lambda m, x, other: x
        zs = [[lax.dot_general(kblks[i][:, h * SB_DH:(h + 1) * SB_DH], sb_q[h], NT_DIMS,
                               preferred_element_type=F32) for h in range(SB_HEADS)] for i in blocks]
        ss = [[lax.dot_general(dblks[i][:, c * DIFF_DH:(c + 1) * DIFF_DH], d_q[c], NT_DIMS,
                               preferred_element_type=F32) for c in range(2 * DIFF_HEADS)] for i in blocks]
        log_betas = [[None] * SB_HEADS for _ in blocks]
        splits = [[None] * SB_HEADS for _ in blocks]
        tails = [[None] * SB_HEADS for _ in blocks]
        new_tail = []
        for h in range(SB_HEADS):
            tail = sb_tail[h]
            for i in blocks:
                z = zs[i][h]
                sp = jnp.log(1.0 + jnp.exp(-jnp.abs(z)))
                log_betas[i][h] = jnp.minimum(z, 0.0) - sp
                cost = keep(masks[i], z - log_betas[i][h], 0.0)
                splits[i][h] = cost.astype(BF16)
                tails[i][h] = tail
                tail = tail + jnp.sum(cost, axis=0, keepdims=True)
            new_tail.append(tail)
        new_m, new_l, alphas = [], [], []
        probs = [[None] * (2 * DIFF_HEADS) for _ in blocks]
        for c in range(2 * DIFF_HEADS):
            s = [keep(dmasks[i], ss[i][c], NEG_BIG) for i in blocks]
            mn = d_m[c]
            for i in blocks:
                mn = jnp.maximum(mn, jnp.max(s[i], axis=0, keepdims=True))
            al = jnp.exp(d_m[c] - mn)
            l = al * d_l[c]
            for i in blocks:
                p = jnp.exp(s[i] - mn)
                l = l + jnp.sum(p, axis=0, keepdims=True)
                probs[i][c] = p.astype(BF16)
            new_m.append(mn)
            new_l.append(l)
            alphas.append(al)
        newer = [[jnp.dot(newer_keys, splits[i][h], preferred_element_type=F32)
                  for h in range(SB_HEADS)] for i in blocks]
        new_a = []
        for c in range(2 * DIFF_HEADS):
            a = alphas[c] * d_a[c]
            vsl = slice((c // 2) * 2 * DIFF_DH, (c // 2 + 1) * 2 * DIFF_DH)
            for i in blocks:
                a = a + jnp.dot(vdt_ref[js[i], vsl, :], probs[i][c], preferred_element_type=F32)
            new_a.append(a)
        ws = [[keep(masks[i], jnp.exp(log_betas[i][h] - newer[i][h] - tails[i][h]), 0.0).astype(BF16)
               for h in range(SB_HEADS)] for i in blocks]
        new_acc = []
        for h in range(SB_HEADS):
            acc = sb_acc[h]
            for i in blocks:
                acc = acc + jnp.dot(vst_ref[kbs[i], h * SB_DH:(h + 1) * SB_DH, :], ws[i][h],
                                    preferred_element_type=F32)
            new_acc.append(acc)
        return tuple(new_acc), tuple(new_tail), tuple(new_m), tuple(new_l), tuple(new_a)

    zrow = jnp.zeros((1, bq), F32)
    init = (tuple(jnp.zeros((SB_DH, bq), F32) for _ in range(SB_HEADS)),
            tuple(zrow for _ in range(SB_HEADS)),
            tuple(jnp.full((1, bq), NEG_BIG, F32) for _ in range(2 * DIFF_HEADS)),
            tuple(zrow for _ in range(2 * DIFF_HEADS)),
            tuple(jnp.zeros((2 * DIFF_DH, bq), F32) for _ in range(2 * DIFF_HEADS)))
    edge_body = functools.partial(body, edge=True, nblk=1, j0=0)
    n_pairs = jnp.right_shift(jnp.maximum(nkb - 2, 0), 1)
    carry = lax.fori_loop(0, jnp.minimum(nkb, 1), edge_body, init)
    carry = lax.fori_loop(0, n_pairs, functools.partial(body, edge=False, nblk=2, j0=1), carry)
    carry = lax.fori_loop(1 + 2 * n_pairs, nkb - 1, functools.partial(body, edge=False, nblk=1, j0=0), carry)
    sb_acc, _, _, d_l, d_a = lax.fori_loop(jnp.maximum(nkb - 1, 1), nkb, edge_body, carry)

    osb_ref[...] = jnp.concatenate(sb_acc, axis=0).T.astype(osb_ref.dtype)
    lam = (jnp.exp(jnp.sum(lq1_ref[...] * lk1_ref[...], axis=1, keepdims=True))
           - jnp.exp(jnp.sum(lq2_ref[...] * lk2_ref[...], axis=1, keepdims=True)) + lam_init)
    outs = []
    for h in range(DIFF_HEADS):
        o = d_a[2 * h] / d_l[2 * h] - lam * (d_a[2 * h + 1] / d_l[2 * h + 1])
        o = o * lax.rsqrt(jnp.mean(o * o, axis=0, keepdims=True) + EPS) * ng_ref[...]
        outs.append(o * (1.0 - lam_init))
    od_ref[...] = jnp.concatenate(outs, axis=0).T.astype(od_ref.dtype)


def _attention(qs, qd, ks, vs, kd, vd, lam_vecs, norm_g, *, bq, q_base, kv_lo, lam_init):
    b, lq, _ = qs.shape
    lk = ks.shape[1]
    nkb = lk // KEY_BLOCK
    assert bq == KEY_BLOCK and q_base % KEY_BLOCK == 0 and 0 <= kv_lo < KEY_BLOCK
    kblocks = lambda a: a.reshape(b, nkb, KEY_BLOCK, W_HALF)
    vblocks_t = lambda a: a.reshape(b, nkb, KEY_BLOCK, W_HALF).transpose(0, 1, 3, 2)
    qspec = pl.BlockSpec((None, bq, W_HALF), lambda i, j: (i, j, 0))
    kspec = pl.BlockSpec((None, nkb, KEY_BLOCK, W_HALF), lambda i, j: (i, 0, 0, 0))
    vspec = pl.BlockSpec((None, nkb, W_HALF, KEY_BLOCK), lambda i, j: (i, 0, 0, 0))
    vec = pl.BlockSpec((1, DIFF_DH), lambda i, j: (0, 0))
    return pl.pallas_call(
        functools.partial(_attn_kernel, bq=bq, q_base=q_base, kv_lo=kv_lo,
                          n_key_blocks=nkb, lam_init=lam_init),
        grid=(b, lq // bq),
        in_specs=[vec, vec, vec, vec, pl.BlockSpec((2 * DIFF_DH, 1), lambda i, j: (0, 0)),
                  qspec, qspec, kspec, vspec, kspec, vspec],
        out_specs=[qspec, qspec],
        out_shape=[jax.ShapeDtypeStruct((b, lq, W_HALF), BF16), jax.ShapeDtypeStruct((b, lq, W_HALF), BF16)],
        compiler_params=_params("parallel", "arbitrary"),
        name="sb_diff_attention",
    )(*lam_vecs, norm_g, qs, qd, kblocks(ks), vblocks_t(vs), kblocks(kd), vblocks_t(vd))


def _outproj_ln_kernel(yap_ref, ybp_ref, yas_ref, ybs_ref, h_ref, w_ref, g_ref, b_ref, o_ref, *,
                       alpha, prompt_blocks):
    is_prompt = pl.program_id(0) < prompt_blocks
    ya = jnp.where(is_prompt, yap_ref[...], yas_ref[...])
    yb = jnp.where(is_prompt, ybp_ref[...], ybs_ref[...])
    m = (jnp.dot(ya, w_ref[:W_HALF, :], preferred_element_type=F32)
         + jnp.dot(yb, w_ref[W_HALF:, :], preferred_element_type=F32))
    o_ref[...] = _layer_norm(alpha * h_ref[...] + m, g_ref[...], b_ref[...])


def _outproj_ln(ya_p, yb_p, ya_s, yb_s, h, w, g, b, alpha):
    n = h.shape[0]
    n_p, n_s = ya_p.shape[0], ya_s.shape[0]
    tm = _tile(math.gcd(n_p, n_s))
    pb = n_p // tm
    row = pl.BlockSpec((tm, D_MODEL), lambda i: (i, 0))
    prompt = pl.BlockSpec((tm, W_HALF), lambda i: (jnp.minimum(i, pb - 1), 0))
    sample = pl.BlockSpec((tm, W_HALF), lambda i: (jnp.maximum(i - pb, 0), 0))
    full = lambda a: pl.BlockSpec(a.shape, lambda i: (0, 0))
    return pl.pallas_call(
        functools.partial(_outproj_ln_kernel, alpha=alpha, prompt_blocks=pb),
        grid=(n // tm,),
        in_specs=[prompt, prompt, sample, sample, row, full(w), full(g), full(b)],
        out_specs=row,
        out_shape=jax.ShapeDtypeStruct((n, D_MODEL), F32),
        compiler_params=_params("parallel"),
        name="out_proj_ln",
    )(ya_p, yb_p, ya_s, yb_s, h, w, g, b)


def _router_kernel(x_ref, wt_ref, bias_ref, eidx_ref, gate_ref, rank_ref, cnt_ref, run_s, *, tm):
    i = pl.program_id(0)

    @pl.when(i == 0)
    def _():
        run_s[...] = jnp.zeros_like(run_s)

    def split(a):
        hi = a.astype(BF16)
        return hi, (a - hi.astype(F32)).astype(BF16)

    w_hi, w_lo = split(wt_ref[...])
    x_hi, x_lo = split(x_ref[...])
    nt = lambda a, b: lax.dot_general(a, b, NT_DIMS, preferred_element_type=F32)
    logits = nt(w_hi, x_hi) + (nt(w_hi, x_lo) + nt(w_lo, x_hi))
    scores = jax.nn.sigmoid(logits)
    sel = scores + bias_ref[...]
    ninf = -jnp.inf

    sel3 = sel.reshape(N_GROUPS, GROUP_SIZE, tm)
    within = lax.broadcasted_iota(I32, (N_GROUPS, GROUP_SIZE, tm), 1)
    m1 = jnp.max(sel3, axis=1, keepdims=True)
    first = jnp.min(jnp.where(sel3 == m1, within, GROUP_SIZE), axis=1, keepdims=True)
    m2 = jnp.max(jnp.where(within == first, ninf, sel3), axis=1, keepdims=True)
    gscore = (m1 + m2).reshape(N_GROUPS, tm)

    gid = lax.broadcasted_iota(I32, (N_GROUPS, tm), 0)
    gsel = jnp.zeros((N_GROUPS, tm), jnp.bool_)
    cur = gscore
    for _ in range(TOPK_GROUPS):
        m = jnp.max(cur, axis=0, keepdims=True)
        f = jnp.min(jnp.where(cur == m, gid, N_GROUPS), axis=0, keepdims=True)
        pick = gid == f
        gsel = gsel | pick
        cur = jnp.where(pick, ninf, cur)
    emask = jnp.broadcast_to(gsel.reshape(N_GROUPS, 1, tm), (N_GROUPS, GROUP_SIZE, tm)).reshape(N_EXPERTS, tm)

    eid = lax.broadcasted_iota(I32, (N_EXPERTS, tm), 0)
    cur = jnp.where(emask, sel, ninf)
    picks, idxs, gates = [], [], []
    for _ in range(TOP_K):
        m = jnp.max(cur, axis=0, keepdims=True)
        f = jnp.min(jnp.where(cur == m, eid, N_EXPERTS), axis=0, keepdims=True)
        pick = eid == f
        picks.append(pick)
        idxs.append(f)
        gates.append(jnp.sum(jnp.where(pick, scores, 0.0), axis=0, keepdims=True))
        cur = jnp.where(pick, ninf, cur)
    gsum = gates[0]
    for gk in gates[1:]:
        gsum = gsum + gk

    chosen = picks[0]
    for pk in picks[1:]:
        chosen = chosen | pk
    chosen_f = chosen.astype(F32)
    r_ = lax.broadcasted_iota(I32, (tm, tm), 0)
    c_ = lax.broadcasted_iota(I32, (tm, tm), 1)
    earlier = (r_ < c_).astype(BF16)
    before = run_s[...] + jnp.dot(chosen_f.astype(BF16), earlier, preferred_element_type=F32)
    zrow_i = jnp.zeros((1, tm), I32)
    zrow_f = jnp.zeros((1, tm), F32)
    ranks = [jnp.sum(jnp.where(pk, before, 0.0), axis=0, keepdims=True).astype(I32) for pk in picks]
    eidx_ref[...] = jnp.concatenate(idxs + [zrow_i, zrow_i], axis=0)
    gate_ref[...] = jnp.concatenate([gk / gsum * ROUTED_SCALE for gk in gates] + [zrow_f, zrow_f], axis=0)
    rank_ref[...] = jnp.concatenate(ranks + [zrow_i, zrow_i], axis=0)
    run_s[...] = run_s[...] + jnp.sum(chosen_f, axis=1, keepdims=True)
    cnt_ref[...] = jnp.broadcast_to(run_s[...], cnt_ref.shape)


def _router(x, wt, bias):
    n = x.shape[0]
    tm = _tile(n, (512, 256, 128))
    lane_blk = pl.BlockSpec((8, tm), lambda i: (0, i))
    return pl.pallas_call(
        functools.partial(_router_kernel, tm=tm),
        grid=(n // tm,),
        in_specs=[pl.BlockSpec((tm, D_MODEL), lambda i: (i, 0)),
                  pl.BlockSpec(wt.shape, lambda i: (0, 0)),
                  pl.BlockSpec(bias.shape, lambda i: (0, 0))],
        out_specs=[lane_blk, lane_blk, lane_blk, pl.BlockSpec((N_EXPERTS, 128), lambda i: (0, 0))],
        out_shape=[jax.ShapeDtypeStruct((8, n), I32), jax.ShapeDtypeStruct((8, n), F32),
                   jax.ShapeDtypeStruct((8, n), I32), jax.ShapeDtypeStruct((N_EXPERTS, 128), F32)],
        scratch_shapes=[pltpu.VMEM((N_EXPERTS, 1), F32)],
        compiler_params=_params("arbitrary"),
        name="moe_router",
    )(x, wt, bias)


ROW_SUB = D_MODEL // 2 // LANES
HIGH_HALF = -65536
DMA_PRIORITIES = 2


def _pack_rows(dst_ref, x, r0=0):
    rows = x.shape[0]
    bits = lambda v: lax.bitcast_convert_type(v.astype(BF16).astype(F32), I32)
    w = (lax.shift_right_logical(bits(x[:, :D_MODEL // 2]), jnp.full((), 16, I32))
         | (bits(x[:, D_MODEL // 2:]) & HIGH_HALF))
    for c in range(ROW_SUB):
        dst_ref[pl.ds(r0 * ROW_SUB + c, rows, stride=ROW_SUB), :] = w[:, c * LANES:(c + 1) * LANES]


def _unpack_rows(src_ref, rows, r0=0):
    ws = [src_ref[pl.ds(r0 * ROW_SUB + c, rows, stride=ROW_SUB), :] for c in range(ROW_SUB)]
    lo = [lax.bitcast_convert_type(lax.shift_left(w, jnp.full((), 16, I32)), F32) for w in ws]
    hi = [lax.bitcast_convert_type(w & HIGH_HALF, F32) for w in ws]
    return jnp.concatenate(lo + hi, axis=1)


def _dispatch_kernel(*refs, tm):
    n_idx = TOP_K * tm // LANES
    dest_refs, (x_ref, o_hbm, buf, sem) = refs[:n_idx], refs[n_idx:]
    i = pl.program_id(0)
    slot = i % 2
    stage = buf.at[slot]
    _pack_rows(stage, x_ref[...])

    def wait_all():
        for _ in range(TOP_K):
            pltpu.make_async_copy(stage, o_hbm.at[pl.ds(0, tm * ROW_SUB)], sem).wait()

    @pl.when(i > 0)
    def _():
        wait_all()

    for k in range(TOP_K):
        for blk in range(tm // LANES):
            def start(t, carry, k=k, blk=blk):
                for prio in range(DMA_PRIORITIES):
                    jj = t * DMA_PRIORITIES + prio
                    d = pl.multiple_of(dest_refs[k * (tm // LANES) + blk][0, 0, jj] * ROW_SUB, ROW_SUB)
                    s = pl.multiple_of((blk * LANES + jj) * ROW_SUB, ROW_SUB)
                    pltpu.make_async_copy(stage.at[pl.ds(s, ROW_SUB)], o_hbm.at[pl.ds(d, ROW_SUB)],
                                          sem).start(priority=prio)
                return carry
            lax.fori_loop(0, LANES // DMA_PRIORITIES, start, 0, unroll=4)

    @pl.when(i == pl.num_programs(0) - 1)
    def _():
        wait_all()


def _slot_spec():
    return pl.BlockSpec((1, 1, LANES), lambda i: (i, 0, 0), memory_space=pltpu.SMEM)


def _dispatch(x, dests, n_rows, tm):
    n = x.shape[0]
    return pl.pallas_call(
        functools.partial(_dispatch_kernel, tm=tm),
        grid=(n // tm,),
        in_specs=[_slot_spec() for _ in dests] + [pl.BlockSpec((tm, D_MODEL), lambda i: (i, 0))],
        out_specs=pl.BlockSpec(memory_space=pl.ANY),
        out_shape=jax.ShapeDtypeStruct((n_rows * ROW_SUB, LANES), I32),
        scratch_shapes=[pltpu.VMEM((2, tm * ROW_SUB, LANES), I32), pltpu.SemaphoreType.DMA(())],
        compiler_params=_params("arbitrary"),
        name="moe_dispatch",
    )(*dests, x)


def _ffn_kernel(be_ref, nv_ref, xs_ref, w1_ref, w3_ref, w2_ref, o_ref, wb1, wb3, wb2):
    i = pl.program_id(0)

    @pl.when((i == 0) | (be_ref[i] != be_ref[jnp.maximum(i - 1, 0)]))
    def _():
        wb1[...] = w1_ref[0].astype(BF16)
        wb3[...] = w3_ref[0].astype(BF16)
        wb2[...] = w2_ref[0].astype(BF16)

    x = _unpack_rows(xs_ref, EXPERT_BLOCK)
    live = lax.broadcasted_iota(I32, (EXPERT_BLOCK, 1), 0) < nv_ref[i]
    x = jnp.where(live, x, 0.0).astype(BF16)
    a = jnp.dot(x, wb1[...], preferred_element_type=F32)
    b = jnp.dot(x, wb3[...], preferred_element_type=F32)
    hdn = (a * jax.nn.sigmoid(a) * b).astype(BF16)
    _pack_rows(o_ref, jnp.dot(hdn, wb2[...], preferred_element_type=F32))


def _expert_ffn(xs, blk_e, blk_valid, w1, w3, w2):
    nb = xs.shape[0] // (EXPERT_BLOCK * ROW_SUB)
    de = w1.shape[-1]
    rows = pl.BlockSpec((EXPERT_BLOCK * ROW_SUB, LANES), lambda i, be, nv: (i, 0))
    grid_spec = pltpu.PrefetchScalarGridSpec(
        num_scalar_prefetch=2,
        grid=(nb,),
        in_specs=[rows,
                  pl.BlockSpec((1, D_MODEL, de), lambda i, be, nv: (be[i], 0, 0)),
                  pl.BlockSpec((1, D_MODEL, de), lambda i, be, nv: (be[i], 0, 0)),
                  pl.BlockSpec((1, de, D_MODEL), lambda i, be, nv: (be[i], 0, 0))],
        out_specs=rows,
        scratch_shapes=[pltpu.VMEM((D_MODEL, de), BF16), pltpu.VMEM((D_MODEL, de), BF16),
                        pltpu.VMEM((de, D_MODEL), BF16)],
    )
    return pl.pallas_call(
        _ffn_kernel,
        grid_spec=grid_spec,
        out_shape=jax.ShapeDtypeStruct(xs.shape, I32),
        compiler_params=_params("arbitrary"),
        name="moe_expert_ffn",
    )(blk_e, blk_valid, xs, w1, w3, w2)


def _combine_kernel(*refs, tm, alpha):
    n_idx = TOP_K * tm // LANES
    dest_refs = refs[:n_idx]
    gate_ref, x_ref, ys_hbm, sw1_ref, sw3_ref, sw2_ref, g_ref, b_ref, o_ref, buf, sem = refs[n_idx:]
    for k in range(TOP_K):
        for blk in range(tm // LANES):
            def start(t, carry, k=k, blk=blk):
                for prio in range(DMA_PRIORITIES):
                    jj = t * DMA_PRIORITIES + prio
                    s = pl.multiple_of(dest_refs[k * (tm // LANES) + blk][0, 0, jj] * ROW_SUB, ROW_SUB)
                    d = pl.multiple_of((k * tm + blk * LANES + jj) * ROW_SUB, ROW_SUB)
                    pltpu.make_async_copy(ys_hbm.at[pl.ds(s, ROW_SUB)], buf.at[pl.ds(d, ROW_SUB)],
                                          sem.at[k]).start(priority=prio)
                return carry
            lax.fori_loop(0, LANES // DMA_PRIORITIES, start, 0, unroll=4)

    x = x_ref[...]
    xb = x.astype(BF16)
    a = jnp.dot(xb, sw1_ref[...], preferred_element_type=F32)
    b = jnp.dot(xb, sw3_ref[...], preferred_element_type=F32)
    shared = jnp.dot((a * jax.nn.sigmoid(a) * b).astype(BF16), sw2_ref[...], preferred_element_type=F32)

    gate = gate_ref[...]
    routed = shared
    for k in range(TOP_K):
        grp = pl.ds(k * tm * ROW_SUB, tm * ROW_SUB)
        pltpu.make_async_copy(ys_hbm.at[pl.ds(0, tm * ROW_SUB)], buf.at[grp], sem.at[k]).wait()
        routed = routed + gate[:, k:k + 1] * _unpack_rows(buf, tm, k * tm)
    o_ref[...] = _layer_norm(alpha * x + routed, g_ref[...], b_ref[...])


def _combine(dests, gate_t, x, ys, sw1, sw3, sw2, g, b, alpha, tm):
    n = x.shape[0]
    full = lambda a: pl.BlockSpec(a.shape, lambda i: (0,) * a.ndim)
    return pl.pallas_call(
        functools.partial(_combine_kernel, tm=tm, alpha=alpha),
        grid=(n // tm,),
        in_specs=[_slot_spec() for _ in dests] + [
                  pl.BlockSpec((tm, 8), lambda i: (i, 0)),
                  pl.BlockSpec((tm, D_MODEL), lambda i: (i, 0)),
                  pl.BlockSpec(memory_space=pl.ANY),
                  full(sw1), full(sw3), full(sw2), full(g), full(b)],
        out_specs=pl.BlockSpec((tm, D_MODEL), lambda i: (i, 0)),
        out_shape=jax.ShapeDtypeStruct((n, D_MODEL), F32),
        scratch_shapes=[pltpu.VMEM((TOP_K * tm * ROW_SUB, LANES), I32), pltpu.SemaphoreType.DMA((TOP_K,))],
        compiler_params=_params("arbitrary"),
        name="moe_combine_ln",
    )(*dests, gate_t, x, ys, sw1, sw3, sw2, g, b)


def _moe(x, rw_t, r_bias, w1, w3, w2, sw1, sw3, sw2, g, b, alpha):
    n = x.shape[0]
    eidx, gate, rank, cnt = _router(x, rw_t, r_bias)
    counts = cnt[:, 0].astype(I32)
    padded = (counts + EXPERT_BLOCK - 1) // EXPERT_BLOCK * EXPERT_BLOCK
    pad_end = jnp.cumsum(padded)
    pad_start = pad_end - padded
    onehot = eidx[:, :, None] == jnp.arange(N_EXPERTS, dtype=I32)[None, None, :]
    dest = jnp.sum(jnp.where(onehot, pad_start[None, None, :], 0), axis=-1) + rank
    n_rows = -(-(n * TOP_K + N_EXPERTS * (EXPERT_BLOCK - 1)) // EXPERT_BLOCK) * EXPERT_BLOCK
    nb = n_rows // EXPERT_BLOCK
    blk_lo = jnp.arange(nb, dtype=I32) * EXPERT_BLOCK
    blk_e = jnp.minimum(jnp.sum((pad_end[None, :] <= blk_lo[:, None]).astype(I32), axis=1), N_EXPERTS - 1)
    blk_valid = jnp.clip(jnp.take(pad_start + counts, blk_e) - blk_lo, 0, EXPERT_BLOCK).astype(I32)
    tm = _tile(n, (256, 128))
    runs = dest.reshape(8, n // tm, tm // LANES, 1, LANES)
    dests = [runs[k, :, r] for k in range(TOP_K) for r in range(tm // LANES)]
    xs = _dispatch(x, dests, n_rows, tm)
    ys = _expert_ffn(xs, blk_e, blk_valid, w1, w3, w2)
    return _combine(dests, gate.T, x, ys, sw1, sw3, sw2, g, b, alpha, tm)


def kernel(x_prompt, x_sample, cache_sb_k, cache_sb_v, cache_diff_k, cache_diff_v, state_s5_re, state_s5_im, state_ret, meta_tokens, w_in_e, s5_lam_re, s5_lam_im, s5_log_dt, s5_b_re, s5_b_im, s5_c_re, s5_c_im, s5_d, s5_w_glu, w_out_e, w_in_o, diff_lq1, diff_lk1, diff_lq2, diff_lk2, diff_norm_g, w_out_o, ln1_g, ln1_b, ln2_g, ln2_b, router_w, router_bias, exp_w1, exp_w3, exp_w2, sh_w1, sh_w3, sh_w2):
    bsz, seq, _ = x_prompt.shape
    bsz_s, t_s, _ = x_sample.shape
    past = cache_sb_k.shape[2]
    depth = ln1_g.shape[0]
    alpha = (2 * depth) ** 0.25
    lp = PAD + N_META + seq
    n_p = bsz * lp
    n_s = bsz_s * t_s
    lt = N_META + seq

    meta = jnp.broadcast_to(meta_tokens[None].astype(F32), (bsz, N_META, D_MODEL))
    hp = jnp.concatenate([jnp.zeros((bsz, PAD, D_MODEL), F32), meta, x_prompt], 1)
    h = jnp.concatenate([hp.reshape(n_p, D_MODEL), x_sample.reshape(n_s, D_MODEL)], 0)

    outs_p = {k: [] for k in ("sbk", "sbv", "dk", "dv", "s5r", "s5i", "ret")}
    outs_s = {k: [] for k in ("sbk", "sbv", "dk", "dv", "s5r", "s5i", "ret")}
    half = lambda c0, dt=F32: (c0 * W_HALF, W_HALF, dt)

    for layer in range(depth):
        i = layer // 2
        if layer % 2 == 0:
            w_in = w_in_e[i]
            splits = [half(c) for c in range(5)]
            u_p, q_p, k_p, v_p, g_p = _project(h, w_in.astype(BF16), splits, 0, n_p)
            u_s, q_s, k_s, v_s, g_s = _project(h, w_in, splits, n_p, n_s, precise=True)
            s5p = _s5_prepare(s5_lam_re[i], s5_lam_im[i], s5_log_dt[i], s5_b_re[i], s5_b_im[i],
                              s5_c_re[i], s5_c_im[i], s5_d[i], s5_w_glu[i])
            seq_p = lambda a: a.reshape(bsz, lp, W_HALF)
            seq_s = lambda a: a.reshape(bsz_s, t_s, W_HALF)
            z5 = jnp.zeros((bsz, S5_WIDTH), F32)
            y5_p, fr_p, fi_p = _s5_mixer(seq_p(u_p), z5, z5, s5p, Q_BLOCK, False)
            y5_s, fr_s, fi_s = _s5_mixer(seq_s(u_s), state_s5_re[i].reshape(bsz_s, S5_WIDTH).astype(F32),
                                         state_s5_im[i].reshape(bsz_s, S5_WIDTH).astype(F32), s5p, t_s, True)
            yr_p, rt_p = _ret_mixer(seq_p(q_p), seq_p(k_p), seq_p(v_p), seq_p(g_p),
                                    jnp.zeros((bsz, RET_HEADS, RET_DK, RET_DK), F32),
                                    jnp.arange(lp) - PAD, Q_BLOCK)
            yr_s, rt_s = _ret_mixer(seq_s(q_s), seq_s(k_s), seq_s(v_s), seq_s(g_s),
                                    state_ret[i].astype(F32), past + jnp.arange(t_s), t_s)
            mixed = (y5_p.reshape(n_p, W_HALF), yr_p.reshape(n_p, W_HALF),
                     y5_s.reshape(n_s, W_HALF), yr_s.reshape(n_s, W_HALF))
            w_out = w_out_e[i]
            st = lambda a, nb_: a.reshape(nb_, S5_GROUPS, S5_STATE)
            outs_p["s5r"].append(st(fr_p, bsz)); outs_p["s5i"].append(st(fi_p, bsz)); outs_p["ret"].append(rt_p)
            outs_s["s5r"].append(st(fr_s, bsz_s)); outs_s["s5i"].append(st(fi_s, bsz_s)); outs_s["ret"].append(rt_s)
        else:
            splits = [half(0, BF16), half(1), half(2), half(3, BF16), half(4), half(5),
                      half(1, BF16), half(2, BF16), half(4, BF16), half(5, BF16)]
            w_in = w_in_o[i].astype(BF16)
            (qs, qd, ksb, vsb, kdb, vdb, ksf, vsf, kdf, vdf) = _project_prompt_kv(h, w_in, bsz, lp)
            (qs_s, ksf_s, vsf_s, qd_s, kdf_s, vdf_s, ksb_s, vsb_s, kdb_s, vdb_s) = _project(h, w_in, splits, n_p, n_s)
            lam_init = 0.8 - 0.6 * math.exp(-0.3 * layer)
            lam_vecs = [v[i].reshape(1, DIFF_DH).astype(F32) for v in (diff_lq1, diff_lk1, diff_lq2, diff_lk2)]
            norm_g = diff_norm_g[i].reshape(2 * DIFF_DH, 1).astype(F32)
            seq_p = lambda a: a.reshape(bsz, lp, W_HALF)
            seq_s = lambda a: a.reshape(bsz_s, t_s, W_HALF)
            osb_p, od_p = _attention(seq_p(qs), seq_p(qd), seq_p(ksb), seq_p(vsb), seq_p(kdb), seq_p(vdb),
                                     lam_vecs, norm_g, bq=Q_BLOCK, q_base=0, kv_lo=PAD, lam_init=lam_init)
            bq_s = -(-t_s // Q_BLOCK) * Q_BLOCK
            lk_s = -(-(past + bq_s) // KEY_BLOCK) * KEY_BLOCK
            qpad = lambda a: jnp.pad(seq_s(a), ((0, 0), (0, bq_s - t_s), (0, 0)))

            def keys(cache, new):
                c = cache.reshape(bsz_s, past, W_HALF).astype(BF16)
                z = jnp.zeros((bsz_s, lk_s - past - t_s, W_HALF), BF16)
                return jnp.concatenate([c, seq_s(new), z], 1)
            osb_s, od_s = _attention(qpad(qs_s), qpad(qd_s), keys(cache_sb_k[i], ksb_s), keys(cache_sb_v[i], vsb_s),
                                     keys(cache_diff_k[i], kdb_s), keys(cache_diff_v[i], vdb_s),
                                     lam_vecs, norm_g, bq=bq_s, q_base=past, kv_lo=0, lam_init=lam_init)
            mixed = (osb_p.reshape(n_p, W_HALF), od_p.reshape(n_p, W_HALF),
                     osb_s[:, :t_s].reshape(n_s, W_HALF), od_s[:, :t_s].reshape(n_s, W_HALF))
            w_out = w_out_o[i]
            kv_p = lambda a, hh, dd: a.reshape(bsz, lt, hh, dd)
            kv_s = lambda a, hh, dd: a.reshape(bsz_s, t_s, hh, dd)
            outs_p["sbk"].append(kv_p(ksf, SB_HEADS, SB_DH)); outs_p["sbv"].append(kv_p(vsf, SB_HEADS, SB_DH))
            outs_p["dk"].append(kv_p(kdf, DIFF_HEADS, 2 * DIFF_DH)); outs_p["dv"].append(kv_p(vdf, DIFF_HEADS, 2 * DIFF_DH))
            outs_s["sbk"].append(kv_s(ksf_s, SB_HEADS, SB_DH)); outs_s["sbv"].append(kv_s(vsf_s, SB_HEADS, SB_DH))
            outs_s["dk"].append(kv_s(kdf_s, DIFF_HEADS, 2 * DIFF_DH)); outs_s["dv"].append(kv_s(vdf_s, DIFF_HEADS, 2 * DIFF_DH))

        row = lambda a: a[layer].reshape(1, D_MODEL).astype(F32)
        h = _outproj_ln(*mixed, h, w_out.astype(BF16), row(ln1_g), row(ln1_b), alpha)
        h = _moe(h, router_w[layer].T.astype(F32), router_bias[layer].reshape(N_EXPERTS, 1).astype(F32),
                 exp_w1[layer].astype(F32), exp_w3[layer].astype(F32), exp_w2[layer].astype(F32),
                 sh_w1[layer].astype(BF16), sh_w3[layer].astype(BF16), sh_w2[layer].astype(BF16),
                 row(ln2_g), row(ln2_b), alpha)

    y_prompt = h[:n_p].reshape(bsz, lp, D_MODEL)[:, PAD + N_META:]
    y_sample = h[n_p:].reshape(bsz_s, t_s, D_MODEL)
    order = ("sbk", "sbv", "dk", "dv", "s5r", "s5i", "ret")
    return ((y_prompt, y_sample) + tuple(jnp.stack(outs_p[k]) for k in order)
            + tuple(jnp.stack(outs_s[k]) for k in order))
```

```python
import functools
import math

import jax
import jax.numpy as jnp
from jax import lax
from jax.experimental import pallas as pl
from jax.experimental.pallas import tpu as pltpu

F32 = jnp.float32
BF16 = jnp.bfloat16
I32 = jnp.int32
HIGHEST = lax.Precision.HIGHEST

D_MODEL = 1024
W_HALF = 512
N_META = 16
Q_BLOCK = 128
PAD = Q_BLOCK - N_META
CHUNK = 64
CHUNK_SHIFT = 6
S5_GROUPS = 32
S5_GROUP_CH = 16
S5_STATE = 64
S5_WIDTH = S5_GROUPS * S5_STATE
S5_BLOCKS = 4
RET_HEADS = 8
RET_DK = 64
SB_HEADS = 8
SB_DH = 64
DIFF_HEADS = 4
DIFF_DH = 64
N_EXPERTS = 64
N_GROUPS = 8
GROUP_SIZE = N_EXPERTS // N_GROUPS
TOPK_GROUPS = 4
TOP_K = 6
ROUTED_SCALE = 2.5
EXPERT_BLOCK = 1024
EPS = 1e-5
NEG_BIG = -1e30
KEY_BLOCK = 128
KEY_SHIFT = 7
SUBLANES = 8
LANES = 128
VMEM_LIMIT = 56 * 1024 * 1024

NT_DIMS = (((1,), (1,)), ((), ()))


def _params(*sem):
    return pltpu.CompilerParams(dimension_semantics=sem, vmem_limit_bytes=VMEM_LIMIT)


def _tile(n, cands=(512, 256, 128, 64)):
    for c in cands:
        if n % c == 0:
            return c
    raise ValueError(f"no tile for {n}")


def _layer_norm(r, g, b):
    mu = jnp.mean(r, -1, keepdims=True)
    d = r - mu
    var = jnp.mean(d * d, -1, keepdims=True)
    return d * lax.rsqrt(var + EPS) * g + b


def _proj_kernel(x_ref, w_ref, *o_refs, cols, precise):
    x = x_ref[...]
    xb = x.astype(F32) if precise else x.astype(BF16)
    for o_ref, c0 in zip(o_refs, cols):
        wd = o_ref.shape[-1]
        w = w_ref[:, c0:c0 + wd]
        if precise:
            y = jnp.dot(xb, w, precision=HIGHEST, preferred_element_type=F32)
        else:
            y = jnp.dot(xb, w, preferred_element_type=F32)
        o_ref[...] = y.astype(o_ref.dtype)


def _project(x, w, outs, row0, n, precise=False):
    k = x.shape[1]
    tm = _tile(math.gcd(n, row0) if row0 else n)
    blk0 = row0 // tm
    cols = tuple(o[0] for o in outs)
    return pl.pallas_call(
        functools.partial(_proj_kernel, cols=cols, precise=precise),
        grid=(n // tm,),
        in_specs=[pl.BlockSpec((tm, k), lambda i: (i + blk0, 0)),
                  pl.BlockSpec(w.shape, lambda i: (0, 0))],
        out_specs=[pl.BlockSpec((tm, o[1]), lambda i: (i, 0)) for o in outs],
        out_shape=[jax.ShapeDtypeStruct((n, o[1]), o[2]) for o in outs],
        compiler_params=_params("parallel"),
        name="in_proj",
    )(x, w)


def _proj_kv_kernel(x_ref, w_ref, qs_ref, qd_ref, ks_ref, vs_ref, kd_ref, vd_ref,
                    ksf_hbm, vsf_hbm, kdf_hbm, vdf_hbm, stage, sem, *, rows, nq, lt):
    b, q = pl.program_id(0), pl.program_id(1)
    step = b * nq + q
    slot = step % 2
    caches = (ksf_hbm, vsf_hbm, kdf_hbm, vdf_hbm)
    xb = x_ref[...].astype(BF16)
    col = lambda c: jnp.dot(xb, w_ref[:, c * W_HALF:(c + 1) * W_HALF], preferred_element_type=F32)
    qs_ref[...] = col(0).astype(BF16)
    qd_ref[...] = col(3).astype(BF16)
    for a, (c, o_ref) in enumerate(zip((1, 2, 4, 5), (ks_ref, vs_ref, kd_ref, vd_ref))):
        y = col(c)
        o_ref[...] = y.astype(BF16)
        stage[slot, a] = y

    def copy(slot_, a, b_, q_, first):
        if first:
            return pltpu.make_async_copy(stage.at[slot_, a, pl.ds(PAD, rows - PAD)],
                                         caches[a].at[pl.ds(pl.multiple_of(b_ * lt, SUBLANES), rows - PAD)], sem)
        dst = pl.multiple_of(b_ * lt + q_ * rows - PAD, SUBLANES)
        return pltpu.make_async_copy(stage.at[slot_, a], caches[a].at[pl.ds(dst, rows)], sem)

    def start_all(first):
        for a in range(len(caches)):
            copy(slot, a, b, q, first).start()

    def wait_all(first):
        for a in range(len(caches)):
            copy(0, a, 0, 1, first).wait()

    @pl.when((step > 0) & (q == 1))
    def _():
        wait_all(True)

    @pl.when((step > 0) & (q != 1))
    def _():
        wait_all(False)

    @pl.when(q == 0)
    def _():
        start_all(True)

    @pl.when(q != 0)
    def _():
        start_all(False)

    @pl.when(step == pl.num_programs(0) * nq - 1)
    def _():
        wait_all(False)


def _project_prompt_kv(x, w, bsz, lp):
    nq = next(c for c in (4, 3, 2) if lp % c == 0 and (lp // c) % SUBLANES == 0 and lp // c > PAD)
    rows = lp // nq
    lt = lp - PAD
    blk = lambda wd: pl.BlockSpec((rows, wd), lambda i, j: (i * nq + j, 0))
    pad_out = jax.ShapeDtypeStruct((bsz * lp, W_HALF), BF16)
    cache_out = jax.ShapeDtypeStruct((bsz * lt, W_HALF), F32)
    return pl.pallas_call(
        functools.partial(_proj_kv_kernel, rows=rows, nq=nq, lt=lt),
        grid=(bsz, nq),
        in_specs=[blk(x.shape[1]), pl.BlockSpec(w.shape, lambda i, j: (0, 0))],
        out_specs=[blk(W_HALF)] * 6 + [pl.BlockSpec(memory_space=pl.ANY)] * 4,
        out_shape=[pad_out] * 6 + [cache_out] * 4,
        scratch_shapes=[pltpu.VMEM((2, 4, rows, W_HALF), F32), pltpu.SemaphoreType.DMA(())],
        compiler_params=_params("arbitrary", "arbitrary"),
        name="in_proj_kv",
    )(x, w)


def _s5_kernel(u_ref, x0r_ref, x0i_ref, lr_ref, li_ref, bre_ref, bim_ref, cre_ref, cim_ref, d_ref,
               wglu_ref, y_ref, fr_ref, fi_ref, xr_s, xi_s, sr_s, si_s, *, tc, sg, precise):
    c = pl.program_id(1)

    @pl.when(c == 0)
    def _():
        xr_s[...] = x0r_ref[...]
        xi_s[...] = x0i_ref[...]

    def mm(a, b):
        if precise:
            return jnp.dot(a, b, precision=HIGHEST, preferred_element_type=F32)
        return jnp.dot(a.astype(BF16), b, preferred_element_type=F32)

    cw, sw = W_HALF // S5_BLOCKS, S5_WIDTH // S5_BLOCKS
    for s in range(sg):
        u = u_ref[s]
        sr_s[s] = jnp.concatenate(
            [mm(u[:, b * cw:(b + 1) * cw], bre_ref[b]) for b in range(S5_BLOCKS)], axis=1)
        si_s[s] = jnp.concatenate(
            [mm(u[:, b * cw:(b + 1) * cw], bim_ref[b]) for b in range(S5_BLOCKS)], axis=1)
    lr = lr_ref[...]
    li = li_ref[...]

    def step(t, carry):
        out = []
        for s, (xr, xi) in enumerate(carry):
            nr = lr * xr - li * xi + sr_s[s, pl.ds(t, 1), :]
            ni = lr * xi + li * xr + si_s[s, pl.ds(t, 1), :]
            sr_s[s, pl.ds(t, 1), :] = nr
            si_s[s, pl.ds(t, 1), :] = ni
            out.append((nr, ni))
        return tuple(out)

    init = tuple((xr_s[pl.ds(s, 1), :], xi_s[pl.ds(s, 1), :]) for s in range(sg))
    last = lax.fori_loop(0, tc, step, init, unroll=2)
    xr = jnp.concatenate([x[0] for x in last], axis=0)
    xi = jnp.concatenate([x[1] for x in last], axis=0)
    xr_s[...] = xr
    xi_s[...] = xi
    for s in range(sg):
        y = jnp.concatenate(
            [mm(sr_s[s, :, b * sw:(b + 1) * sw], cre_ref[b]) - mm(si_s[s, :, b * sw:(b + 1) * sw], cim_ref[b])
             for b in range(S5_BLOCKS)], axis=1) + d_ref[...] * u_ref[s]
        y = jax.nn.gelu(y)
        y = y * jax.nn.sigmoid(mm(y, wglu_ref[...]))
        y_ref[s] = y.astype(y_ref.dtype)

    @pl.when(c == pl.num_programs(1) - 1)
    def _():
        fr_ref[...] = xr
        fi_ref[...] = xi


def _s5_mixer(u, x0r, x0i, p, tc, precise):
    b, l, _ = u.shape
    nc = l // tc
    sg = math.gcd(b, SUBLANES)
    wdt = F32 if precise else BF16
    full = lambda a: pl.BlockSpec(a.shape, lambda i, j: (0,) * a.ndim)
    st_spec = pl.BlockSpec((sg, S5_WIDTH), lambda i, j: (i, 0))
    seq_spec = pl.BlockSpec((sg, tc, W_HALF), lambda i, j: (i, j, 0))
    mats = [p["bre"].astype(wdt), p["bim"].astype(wdt), p["cre"].astype(wdt), p["cim"].astype(wdt)]
    wglu = p["wglu"].astype(wdt)
    return pl.pallas_call(
        functools.partial(_s5_kernel, tc=tc, sg=sg, precise=precise),
        grid=(b // sg, nc),
        in_specs=[seq_spec, st_spec, st_spec,
                  full(p["lr"]), full(p["li"])] + [full(m) for m in mats] + [full(p["d"]), full(wglu)],
        out_specs=[seq_spec, st_spec, st_spec],
        out_shape=[jax.ShapeDtypeStruct((b, l, W_HALF), BF16),
                   jax.ShapeDtypeStruct((b, S5_WIDTH), F32),
                   jax.ShapeDtypeStruct((b, S5_WIDTH), F32)],
        scratch_shapes=[pltpu.VMEM((sg, S5_WIDTH), F32), pltpu.VMEM((sg, S5_WIDTH), F32),
                        pltpu.VMEM((sg, tc, S5_WIDTH), F32), pltpu.VMEM((sg, tc, S5_WIDTH), F32)],
        compiler_params=_params("parallel", "arbitrary"),
        name="s5_mixer",
    )(u, x0r, x0i, p["lr"], p["li"], *mats, p["d"], wglu)


def _s5_prepare(lam_re, lam_im, log_dt, b_re, b_im, c_re, c_im, d, w_glu):
    dt = jnp.exp(log_dt)[:, None]
    mag = jnp.exp(lam_re * dt)
    lb_re = mag * jnp.cos(lam_im * dt)
    lb_im = mag * jnp.sin(lam_im * dt)
    nr, ni = lb_re - 1.0, lb_im
    den = jnp.square(lam_re) + jnp.square(lam_im)
    cr = (nr * lam_re + ni * lam_im) / den
    ci = (ni * lam_re - nr * lam_im) / den
    bb_re = cr[..., None] * b_re - ci[..., None] * b_im
    bb_im = cr[..., None] * b_im + ci[..., None] * b_re
    gb = S5_GROUPS // S5_BLOCKS
    eye = jnp.eye(gb, dtype=F32)
    per_blk = lambda m: m.reshape((S5_BLOCKS, gb) + m.shape[1:])
    blk_in = lambda m: jnp.einsum("bgpi,gh->bgihp", per_blk(m), eye).reshape(
        S5_BLOCKS, W_HALF // S5_BLOCKS, S5_WIDTH // S5_BLOCKS)
    blk_out = lambda m: jnp.einsum("bgop,gh->bgpho", per_blk(m), eye).reshape(
        S5_BLOCKS, S5_WIDTH // S5_BLOCKS, W_HALF // S5_BLOCKS)
    return dict(lr=lb_re.reshape(1, S5_WIDTH), li=lb_im.reshape(1, S5_WIDTH),
                bre=blk_in(bb_re), bim=blk_in(bb_im), cre=blk_out(c_re), cim=blk_out(c_im),
                d=d.reshape(1, W_HALF), wglu=w_glu)


def _ret_kernel(q_ref, k_ref, v_ref, g_ref, cos_ref, sin_ref, dmat_ref, inter_ref, zeta_ref, gc_ref,
                swap_ref, avg_ref, r0_ref, y_ref, rf_ref, r_s, *, tc):
    c = pl.program_id(1)

    @pl.when(c == 0)
    def _():
        r_s[...] = r0_ref[0]

    cos = cos_ref[...]
    sin = sin_ref[...]

    def split(x):
        hi = x.astype(BF16)
        return hi, (x - hi.astype(F32)).astype(BF16)

    def rot(x):
        partner = jnp.dot(x.astype(BF16), swap_ref[...], preferred_element_type=F32)
        return x * cos + partner * sin

    def head_mean(x):
        hi, lo = split(x)
        return (jnp.dot(hi, avg_ref[...], preferred_element_type=F32)
                + jnp.dot(lo, avg_ref[...], preferred_element_type=F32))

    q = rot(q_ref[...])
    k = rot(k_ref[...]) * (RET_DK ** -0.5)
    v = v_ref[...]
    vz = v * zeta_ref[...]
    g = g_ref[...]
    gate = g * jax.nn.sigmoid(g)
    heads = [slice(h * RET_DK, (h + 1) * RET_DK) for h in range(RET_HEADS)]
    qb = q.astype(BF16)
    kb = k.astype(BF16)
    vb = v.astype(BF16)
    vzb = vz.astype(BF16)
    kt = k.T.astype(BF16)
    scores = [lax.dot_general(qb[:, sl], kb[:, sl], NT_DIMS, preferred_element_type=F32) for sl in heads]
    cross = [jnp.dot(qb[:, sl], r_s[h].astype(BF16), preferred_element_type=F32)
             for h, sl in enumerate(heads)]
    new_r = [gc_ref[h] * r_s[h] + jnp.dot(kt[sl, :], vzb[:, sl], preferred_element_type=F32)
             for h, sl in enumerate(heads)]
    decayed = [(scores[h] * dmat_ref[h]).astype(BF16) for h in range(RET_HEADS)]
    outs = [jnp.dot(decayed[h], vb[:, sl], preferred_element_type=F32) + cross[h] * inter_ref[:, sl]
            for h, sl in enumerate(heads)]
    for h in range(RET_HEADS):
        r_s[h] = new_r[h]
    o = jnp.concatenate(outs, axis=1)
    dlt = o - head_mean(o)
    var = head_mean(dlt * dlt)
    y_ref[...] = (dlt * lax.rsqrt(var + EPS) * gate).astype(y_ref.dtype)

    @pl.when(c == pl.num_programs(1) - 1)
    def _():
        rf_ref[0] = r_s[...]


def _ret_mixer(q, k, v, g, r0, pos, tc):
    b, l, _ = q.shape
    nc = l // tc
    inv = 1.0 / (10000.0 ** jnp.linspace(0.0, 1.0, RET_DK // 2, dtype=F32))
    ang = pos.astype(F32)[:, None] * inv[None]
    cos = jnp.tile(jnp.concatenate([jnp.cos(ang), jnp.cos(ang)], 1), (1, RET_HEADS))
    sin = jnp.tile(jnp.concatenate([-jnp.sin(ang), jnp.sin(ang)], 1), (1, RET_HEADS))
    log_gamma = jnp.log(1.0 - 2.0 ** (-5.0 - jnp.arange(RET_HEADS, dtype=F32)))
    n = jnp.arange(tc, dtype=F32)
    diff = n[:, None] - n[None, :]
    causal = diff >= 0
    dmat = jnp.where(causal[None], jnp.exp(jnp.where(causal, diff, 0.0)[None] * log_gamma[:, None, None]), 0.0)
    per_lane = lambda m: jnp.repeat(m, RET_DK, axis=1)
    inter = per_lane(jnp.exp((n[:, None] + 1.0) * log_gamma[None]))
    zeta = per_lane(jnp.exp((tc - 1.0 - n)[:, None] * log_gamma[None]))
    gc = jnp.broadcast_to(jnp.exp(tc * log_gamma)[:, None, None], (RET_HEADS, RET_DK, RET_DK))
    lane = jnp.arange(W_HALF)
    partner = jnp.where(lane % RET_DK < RET_DK // 2, lane + RET_DK // 2, lane - RET_DK // 2)
    swap = (lane[:, None] == partner[None, :]).astype(BF16)
    avg = ((lane[:, None] // RET_DK == lane[None, :] // RET_DK) / RET_DK).astype(BF16)
    seq = pl.BlockSpec((None, tc, W_HALF), lambda i, j: (i, j, 0))
    tab = pl.BlockSpec((tc, W_HALF), lambda i, j: (j, 0))
    full = lambda a: pl.BlockSpec(a.shape, lambda i, j: (0,) * a.ndim)
    st = pl.BlockSpec((1, RET_HEADS, RET_DK, RET_DK), lambda i, j: (i, 0, 0, 0))
    return pl.pallas_call(
        functools.partial(_ret_kernel, tc=tc),
        grid=(b, nc),
        in_specs=[seq, seq, seq, seq, tab, tab, full(dmat), full(inter), full(zeta), full(gc),
                  full(swap), full(avg), st],
        out_specs=[seq, st],
        out_shape=[jax.ShapeDtypeStruct((b, l, W_HALF), BF16),
                   jax.ShapeDtypeStruct((b, RET_HEADS, RET_DK, RET_DK), F32)],
        scratch_shapes=[pltpu.VMEM((RET_HEADS, RET_DK, RET_DK), F32)],
        compiler_params=_params("parallel", "arbitrary"),
        name="ret_mixer",
    )(q, k, v, g, cos, sin, dmat, inter, zeta, gc, swap, avg, r0)


def _attn_kernel(lq1_ref, lk1_ref, lq2_ref, lk2_ref, ng_ref, qs_ref, qd_ref, ks_ref, vs_ref, kd_ref,
                 vd_ref, osb_ref, od_ref, vst_ref, vdt_ref, *, bq, q_base, kv_lo, n_key_blocks, lam_init):
    qi = pl.program_id(1)

    @pl.when(qi == 0)
    def _():
        def flip(kb, carry):
            vst_ref[kb] = vs_ref[kb].astype(F32).T.astype(BF16)
            vdt_ref[kb] = vd_ref[kb].astype(F32).T.astype(BF16)
            return carry
        lax.fori_loop(0, n_key_blocks, flip, 0)

    q0 = q_base + qi * bq
    nkb = jnp.minimum(jnp.right_shift(q0 + bq + KEY_BLOCK - 1, KEY_SHIFT), n_key_blocks)
    qpos = q0 + lax.broadcasted_iota(I32, (KEY_BLOCK, bq), 1)
    koff = lax.broadcasted_iota(I32, (KEY_BLOCK, bq), 0)
    qchunk = jnp.right_shift(qpos, CHUNK_SHIFT)
    row = lax.broadcasted_iota(I32, (KEY_BLOCK, KEY_BLOCK), 0)
    col = lax.broadcasted_iota(I32, (KEY_BLOCK, KEY_BLOCK), 1)
    newer_keys = (col > row).astype(BF16)

    qs = (qs_ref[...].astype(F32) * (SB_DH ** -0.5)).astype(BF16)
    qd = (qd_ref[...].astype(F32) * (DIFF_DH ** -0.5)).astype(BF16)
    sb_q = [qs[:, h * SB_DH:(h + 1) * SB_DH] for h in range(SB_HEADS)]
    d_q = [qd[:, c * DIFF_DH:(c + 1) * DIFF_DH] for c in range(2 * DIFF_HEADS)]

    def body(t, carry, edge, nblk, j0):
        sb_acc, sb_tail, d_m, d_l, d_a = carry
        blocks = range(nblk)
        js = [j0 + nblk * t + i for i in blocks]
        kbs = [nkb - 1 - j for j in js]
        kblks = [ks_ref[kb] for kb in kbs]
        dblks = [kd_ref[j] for j in js]
        if edge:
            masks, dmasks = [], []
            for i in blocks:
                kpos = kbs[i] * KEY_BLOCK + koff
                masks.append((kpos >= kv_lo) & (kpos < qpos))
                dpos = js[i] * KEY_BLOCK + koff
                dmasks.append((dpos >= kv_lo) & (jnp.right_shift(dpos, CHUNK_SHIFT) <= qchunk))
            keep = lambda m, x, other: jnp.where(m, x, other)
        else:
            masks = dmasks = [None] * nblk
            keep = lambda m, x, other: x
        zs = [[lax.dot_general(kblks[i][:, h * SB_DH:(h + 1) * SB_DH], sb_q[h], NT_DIMS,
                               preferred_element_type=F32) for h in range(SB_HEADS)] for i in blocks]
        ss = [[lax.dot_general(dblks[i][:, c * DIFF_DH:(c + 1) * DIFF_DH], d_q[c], NT_DIMS,
                               preferred_element_type=F32) for c in range(2 * DIFF_HEADS)] for i in blocks]
        log_betas = [[None] * SB_HEADS for _ in blocks]
        splits = [[None] * SB_HEADS for _ in blocks]
        tails = [[None] * SB_HEADS for _ in blocks]
        new_tail = []
        for h in range(SB_HEADS):
            tail = sb_tail[h]
            for i in blocks:
                z = zs[i][h]
                sp = jnp.log(1.0 + jnp.exp(-jnp.abs(z)))
                log_betas[i][h] = jnp.minimum(z, 0.0) - sp
                cost = keep(masks[i], z - log_betas[i][h], 0.0)
                splits[i][h] = cost.astype(BF16)
                tails[i][h] = tail
                tail = tail + jnp.sum(cost, axis=0, keepdims=True)
            new_tail.append(tail)
        new_m, new_l, alphas = [], [], []
        probs = [[None] * (2 * DIFF_HEADS) for _ in blocks]
        for c in range(2 * DIFF_HEADS):
            s = [keep(dmasks[i], ss[i][c], NEG_BIG) for i in blocks]
            mn = d_m[c]
            for i in blocks:
                mn = jnp.maximum(mn, jnp.max(s[i], axis=0, keepdims=True))
            al = jnp.exp(d_m[c] - mn)
            l = al * d_l[c]
            for i in blocks:
                p = jnp.exp(s[i] - mn)
                l = l + jnp.sum(p, axis=0, keepdims=True)
                probs[i][c] = p.astype(BF16)
            new_m.append(mn)
            new_l.append(l)
            alphas.append(al)
        newer = [[jnp.dot(newer_keys, splits[i][h], preferred_element_type=F32)
                  for h in range(SB_HEADS)] for i in blocks]
        new_a = []
        for c in range(2 * DIFF_HEADS):
            a = alphas[c] * d_a[c]
            vsl = slice((c // 2) * 2 * DIFF_DH, (c // 2 + 1) * 2 * DIFF_DH)
            for i in blocks:
                a = a + jnp.dot(vdt_ref[js[i], vsl, :], probs[i][c], preferred_element_type=F32)
            new_a.append(a)
        ws = [[keep(masks[i], jnp.exp(log_betas[i][h] - newer[i][h] - tails[i][h]), 0.0).astype(BF16)
               for h in range(SB_HEADS)] for i in blocks]
        new_acc = []
        for h in range(SB_HEADS):
            acc = sb_acc[h]
            for i in blocks:
                acc = acc + jnp.dot(vst_ref[kbs[i], h * SB_DH:(h + 1) * SB_DH, :], ws[i][h],
                                    preferred_element_type=F32)
            new_acc.append(acc)
        return tuple(new_acc), tuple(new_tail), tuple(new_m), tuple(new_l), tuple(new_a)

    zrow = jnp.zeros((1, bq), F32)
    init = (tuple(jnp.zeros((SB_DH, bq), F32) for _ in range(SB_HEADS)),
            tuple(zrow for _ in range(SB_HEADS)),
            tuple(jnp.full((1, bq), NEG_BIG, F32) for _ in range(2 * DIFF_HEADS)),
            tuple(zrow for _ in range(2 * DIFF_HEADS)),
            tuple(jnp.zeros((2 * DIFF_DH, bq), F32) for _ in range(2 * DIFF_HEADS)))
    edge_body = functools.partial(body, edge=True, nblk=1, j0=0)
    n_pairs = jnp.right_shift(jnp.maximum(nkb - 2, 0), 1)
    carry = lax.fori_loop(0, jnp.minimum(nkb, 1), edge_body, init)
    carry = lax.fori_loop(0, n_pairs, functools.partial(body, edge=False, nblk=2, j0=1), carry)
    carry = lax.fori_loop(1 + 2 * n_pairs, nkb - 1, functools.partial(body, edge=False, nblk=1, j0=0), carry)
    sb_acc, _, _, d_l, d_a = lax.fori_loop(jnp.maximum(nkb - 1, 1), nkb, edge_body, carry)

    osb_ref[...] = jnp.concatenate(sb_acc, axis=0).T.astype(osb_ref.dtype)
    lam = (jnp.exp(jnp.sum(lq1_ref[...] * lk1_ref[...], axis=1, keepdims=True))
           - jnp.exp(jnp.sum(lq2_ref[...] * lk2_ref[...], axis=1, keepdims=True)) + lam_init)
    outs = []
    for h in range(DIFF_HEADS):
        o = d_a[2 * h] / d_l[2 * h] - lam * (d_a[2 * h + 1] / d_l[2 * h + 1])
        o = o * lax.rsqrt(jnp.mean(o * o, axis=0, keepdims=True) + EPS) * ng_ref[...]
        outs.append(o * (1.0 - lam_init))
    od_ref[...] = jnp.concatenate(outs, axis=0).T.astype(od_ref.dtype)


def _attention(qs, qd, ks, vs, kd, vd, lam_vecs, norm_g, *, bq, q_base, kv_lo, lam_init):
    b, lq, _ = qs.shape
    lk = ks.shape[1]
    nkb = lk // KEY_BLOCK
    assert bq == KEY_BLOCK and q_base % KEY_BLOCK == 0 and 0 <= kv_lo < KEY_BLOCK
    kblocks = lambda a: a.reshape(b, nkb, KEY_BLOCK, W_HALF)
    qspec = pl.BlockSpec((None, bq, W_HALF), lambda i, j: (i, j, 0))
    kspec = pl.BlockSpec((None, nkb, KEY_BLOCK, W_HALF), lambda i, j: (i, 0, 0, 0))
    vec = pl.BlockSpec((1, DIFF_DH), lambda i, j: (0, 0))
    return pl.pallas_call(
        functools.partial(_attn_kernel, bq=bq, q_base=q_base, kv_lo=kv_lo,
                          n_key_blocks=nkb, lam_init=lam_init),
        grid=(b, lq // bq),
        in_specs=[vec, vec, vec, vec, pl.BlockSpec((2 * DIFF_DH, 1), lambda i, j: (0, 0)),
                  qspec, qspec, kspec, kspec, kspec, kspec],
        out_specs=[qspec, qspec],
        out_shape=[jax.ShapeDtypeStruct((b, lq, W_HALF), BF16), jax.ShapeDtypeStruct((b, lq, W_HALF), BF16)],
        scratch_shapes=[pltpu.VMEM((nkb, W_HALF, KEY_BLOCK), BF16), pltpu.VMEM((nkb, W_HALF, KEY_BLOCK), BF16)],
        compiler_params=_params("parallel", "arbitrary"),
        name="sb_diff_attention",
    )(*lam_vecs, norm_g, qs, qd, kblocks(ks), kblocks(vs), kblocks(kd), kblocks(vd))


def _outproj_ln_kernel(yap_ref, ybp_ref, yas_ref, ybs_ref, h_ref, w_ref, g_ref, b_ref, o_ref, *,
                       alpha, prompt_blocks):
    is_prompt = pl.program_id(0) < prompt_blocks
    ya = jnp.where(is_prompt, yap_ref[...], yas_ref[...])
    yb = jnp.where(is_prompt, ybp_ref[...], ybs_ref[...])
    m = (jnp.dot(ya, w_ref[:W_HALF, :], preferred_element_type=F32)
         + jnp.dot(yb, w_ref[W_HALF:, :], preferred_element_type=F32))
    o_ref[...] = _layer_norm(alpha * h_ref[...] + m, g_ref[...], b_ref[...])


def _outproj_ln(ya_p, yb_p, ya_s, yb_s, h, w, g, b, alpha):
    n = h.shape[0]
    n_p, n_s = ya_p.shape[0], ya_s.shape[0]
    tm = _tile(math.gcd(n_p, n_s))
    pb = n_p // tm
    row = pl.BlockSpec((tm, D_MODEL), lambda i: (i, 0))
    prompt = pl.BlockSpec((tm, W_HALF), lambda i: (jnp.minimum(i, pb - 1), 0))
    sample = pl.BlockSpec((tm, W_HALF), lambda i: (jnp.maximum(i - pb, 0), 0))
    full = lambda a: pl.BlockSpec(a.shape, lambda i: (0, 0))
    return pl.pallas_call(
        functools.partial(_outproj_ln_kernel, alpha=alpha, prompt_blocks=pb),
        grid=(n // tm,),
        in_specs=[prompt, prompt, sample, sample, row, full(w), full(g), full(b)],
        out_specs=row,
        out_shape=jax.ShapeDtypeStruct((n, D_MODEL), F32),
        compiler_params=_params("parallel"),
        name="out_proj_ln",
    )(ya_p, yb_p, ya_s, yb_s, h, w, g, b)


def _router_kernel(x_ref, wt_ref, bias_ref, eidx_ref, gate_ref, rank_ref, cnt_ref, run_s, *, tm):
    i = pl.program_id(0)

    @pl.when(i == 0)
    def _():
        run_s[...] = jnp.zeros_like(run_s)

    def split(a):
        hi = a.astype(BF16)
        return hi, (a - hi.astype(F32)).astype(BF16)

    w_hi, w_lo = split(wt_ref[...])
    x_hi, x_lo = split(x_ref[...])
    nt = lambda a, b: lax.dot_general(a, b, NT_DIMS, preferred_element_type=F32)
    logits = nt(w_hi, x_hi) + (nt(w_hi, x_lo) + nt(w_lo, x_hi))
    scores = jax.nn.sigmoid(logits)
    sel = scores + bias_ref[...]
    ninf = -jnp.inf

    sel3 = sel.reshape(N_GROUPS, GROUP_SIZE, tm)
    within = lax.broadcasted_iota(I32, (N_GROUPS, GROUP_SIZE, tm), 1)
    m1 = jnp.max(sel3, axis=1, keepdims=True)
    first = jnp.min(jnp.where(sel3 == m1, within, GROUP_SIZE), axis=1, keepdims=True)
    m2 = jnp.max(jnp.where(within == first, ninf, sel3), axis=1, keepdims=True)
    gscore = (m1 + m2).reshape(N_GROUPS, tm)

    gid = lax.broadcasted_iota(I32, (N_GROUPS, tm), 0)
    gsel = jnp.zeros((N_GROUPS, tm), jnp.bool_)
    cur = gscore
    for _ in range(TOPK_GROUPS):
        m = jnp.max(cur, axis=0, keepdims=True)
        f = jnp.min(jnp.where(cur == m, gid, N_GROUPS), axis=0, keepdims=True)
        pick = gid == f
        gsel = gsel | pick
        cur = jnp.where(pick, ninf, cur)
    emask = jnp.broadcast_to(gsel.reshape(N_GROUPS, 1, tm), (N_GROUPS, GROUP_SIZE, tm)).reshape(N_EXPERTS, tm)

    eid = lax.broadcasted_iota(I32, (N_EXPERTS, tm), 0)
    cur = jnp.where(emask, sel, ninf)
    picks, idxs, gates = [], [], []
    for _ in range(TOP_K):
        m = jnp.max(cur, axis=0, keepdims=True)
        f = jnp.min(jnp.where(cur == m, eid, N_EXPERTS), axis=0, keepdims=True)
        pick = eid == f
        picks.append(pick)
        idxs.append(f)
        gates.append(jnp.sum(jnp.where(pick, scores, 0.0), axis=0, keepdims=True))
        cur = jnp.where(pick, ninf, cur)
    gsum = gates[0]
    for gk in gates[1:]:
        gsum = gsum + gk

    chosen = picks[0]
    for pk in picks[1:]:
        chosen = chosen | pk
    chosen_f = chosen.astype(F32)
    r_ = lax.broadcasted_iota(I32, (tm, tm), 0)
    c_ = lax.broadcasted_iota(I32, (tm, tm), 1)
    earlier = (r_ < c_).astype(BF16)
    before = run_s[...] + jnp.dot(chosen_f.astype(BF16), earlier, preferred_element_type=F32)
    zrow_i = jnp.zeros((1, tm), I32)
    zrow_f = jnp.zeros((1, tm), F32)
    ranks = [jnp.sum(jnp.where(pk, before, 0.0), axis=0, keepdims=True).astype(I32) for pk in picks]
    eidx_ref[...] = jnp.concatenate(idxs + [zrow_i, zrow_i], axis=0)
    gate_ref[...] = jnp.concatenate([gk / gsum * ROUTED_SCALE for gk in gates] + [zrow_f, zrow_f], axis=0)
    rank_ref[...] = jnp.concatenate(ranks + [zrow_i, zrow_i], axis=0)
    run_s[...] = run_s[...] + jnp.sum(chosen_f, axis=1, keepdims=True)
    cnt_ref[...] = jnp.broadcast_to(run_s[...], cnt_ref.shape)


def _router(x, wt, bias):
    n = x.shape[0]
    tm = _tile(n, (512, 256, 128))
    lane_blk = pl.BlockSpec((8, tm), lambda i: (0, i))
    return pl.pallas_call(
        functools.partial(_router_kernel, tm=tm),
        grid=(n // tm,),
        in_specs=[pl.BlockSpec((tm, D_MODEL), lambda i: (i, 0)),
                  pl.BlockSpec(wt.shape, lambda i: (0, 0)),
                  pl.BlockSpec(bias.shape, lambda i: (0, 0))],
        out_specs=[lane_blk, lane_blk, lane_blk, pl.BlockSpec((N_EXPERTS, 128), lambda i: (0, 0))],
        out_shape=[jax.ShapeDtypeStruct((8, n), I32), jax.ShapeDtypeStruct((8, n), F32),
                   jax.ShapeDtypeStruct((8, n), I32), jax.ShapeDtypeStruct((N_EXPERTS, 128), F32)],
        scratch_shapes=[pltpu.VMEM((N_EXPERTS, 1), F32)],
        compiler_params=_params("arbitrary"),
        name="moe_router",
    )(x, wt, bias)


ROW_SUB = D_MODEL // 2 // LANES
HIGH_HALF = -65536
DMA_PRIORITIES = 2


def _pack_rows(dst_ref, x, r0=0):
    rows = x.shape[0]
    bits = lambda v: lax.bitcast_convert_type(v.astype(BF16).astype(F32), I32)
    w = (lax.shift_right_logical(bits(x[:, :D_MODEL // 2]), jnp.full((), 16, I32))
         | (bits(x[:, D_MODEL // 2:]) & HIGH_HALF))
    for c in range(ROW_SUB):
        dst_ref[pl.ds(r0 * ROW_SUB + c, rows, stride=ROW_SUB), :] = w[:, c * LANES:(c + 1) * LANES]


def _unpack_rows(src_ref, rows, r0=0):
    ws = [src_ref[pl.ds(r0 * ROW_SUB + c, rows, stride=ROW_SUB), :] for c in range(ROW_SUB)]
    lo = [lax.bitcast_convert_type(lax.shift_left(w, jnp.full((), 16, I32)), F32) for w in ws]
    hi = [lax.bitcast_convert_type(w & HIGH_HALF, F32) for w in ws]
    return jnp.concatenate(lo + hi, axis=1)


def _dispatch_kernel(*refs, tm):
    n_idx = TOP_K * tm // LANES
    dest_refs, (x_ref, o_hbm, buf, sem) = refs[:n_idx], refs[n_idx:]
    i = pl.program_id(0)
    slot = i % 2
    stage = buf.at[slot]
    _pack_rows(stage, x_ref[...])

    def wait_all():
        for _ in range(TOP_K):
            pltpu.make_async_copy(stage, o_hbm.at[pl.ds(0, tm * ROW_SUB)], sem).wait()

    @pl.when(i > 0)
    def _():
        wait_all()

    for k in range(TOP_K):
        for blk in range(tm // LANES):
            def start(t, carry, k=k, blk=blk):
                for prio in range(DMA_PRIORITIES):
                    jj = t * DMA_PRIORITIES + prio
                    d = pl.multiple_of(dest_refs[k * (tm // LANES) + blk][0, 0, jj] * ROW_SUB, ROW_SUB)
                    s = pl.multiple_of((blk * LANES + jj) * ROW_SUB, ROW_SUB)
                    pltpu.make_async_copy(stage.at[pl.ds(s, ROW_SUB)], o_hbm.at[pl.ds(d, ROW_SUB)],
                                          sem).start(priority=prio)
                return carry
            lax.fori_loop(0, LANES // DMA_PRIORITIES, start, 0, unroll=4)

    @pl.when(i == pl.num_programs(0) - 1)
    def _():
        wait_all()


def _slot_spec():
    return pl.BlockSpec((1, 1, LANES), lambda i: (i, 0, 0), memory_space=pltpu.SMEM)


def _dispatch(x, dests, n_rows, tm):
    n = x.shape[0]
    return pl.pallas_call(
        functools.partial(_dispatch_kernel, tm=tm),
        grid=(n // tm,),
        in_specs=[_slot_spec() for _ in dests] + [pl.BlockSpec((tm, D_MODEL), lambda i: (i, 0))],
        out_specs=pl.BlockSpec(memory_space=pl.ANY),
        out_shape=jax.ShapeDtypeStruct((n_rows * ROW_SUB, LANES), I32),
        scratch_shapes=[pltpu.VMEM((2, tm * ROW_SUB, LANES), I32), pltpu.SemaphoreType.DMA(())],
        compiler_params=_params("arbitrary"),
        name="moe_dispatch",
    )(*dests, x)


def _ffn_kernel(be_ref, nv_ref, xs_ref, w1_ref, w3_ref, w2_ref, o_ref, wb1, wb3, wb2):
    i = pl.program_id(0)

    @pl.when((i == 0) | (be_ref[i] != be_ref[jnp.maximum(i - 1, 0)]))
    def _():
        wb1[...] = w1_ref[0].astype(BF16)
        wb3[...] = w3_ref[0].astype(BF16)
        wb2[...] = w2_ref[0].astype(BF16)

    x = _unpack_rows(xs_ref, EXPERT_BLOCK)
    live = lax.broadcasted_iota(I32, (EXPERT_BLOCK, 1), 0) < nv_ref[i]
    x = jnp.where(live, x, 0.0).astype(BF16)
    a = jnp.dot(x, wb1[...], preferred_element_type=F32)
    b = jnp.dot(x, wb3[...], preferred_element_type=F32)
    hdn = (a * jax.nn.sigmoid(a) * b).astype(BF16)
    _pack_rows(o_ref, jnp.dot(hdn, wb2[...], preferred_element_type=F32))


def _expert_ffn(xs, blk_e, blk_valid, w1, w3, w2):
    nb = xs.shape[0] // (EXPERT_BLOCK * ROW_SUB)
    de = w1.shape[-1]
    rows = pl.BlockSpec((EXPERT_BLOCK * ROW_SUB, LANES), lambda i, be, nv: (i, 0))
    grid_spec = pltpu.PrefetchScalarGridSpec(
        num_scalar_prefetch=2,
        grid=(nb,),
        in_specs=[rows,
                  pl.BlockSpec((1, D_MODEL, de), lambda i, be, nv: (be[i], 0, 0)),
                  pl.BlockSpec((1, D_MODEL, de), lambda i, be, nv: (be[i], 0, 0)),
                  pl.BlockSpec((1, de, D_MODEL), lambda i, be, nv: (be[i], 0, 0))],
        out_specs=rows,
        scratch_shapes=[pltpu.VMEM((D_MODEL, de), BF16), pltpu.VMEM((D_MODEL, de), BF16),
                        pltpu.VMEM((de, D_MODEL), BF16)],
    )
    return pl.pallas_call(
        _ffn_kernel,
        grid_spec=grid_spec,
        out_shape=jax.ShapeDtypeStruct(xs.shape, I32),
        compiler_params=_params("arbitrary"),
        name="moe_expert_ffn",
    )(blk_e, blk_valid, xs, w1, w3, w2)


def _combine_kernel(*refs, tm, alpha):
    n_idx = TOP_K * tm // LANES
    dest_refs = refs[:n_idx]
    gate_ref, x_ref, ys_hbm, sw1_ref, sw3_ref, sw2_ref, g_ref, b_ref, o_ref, buf, sem = refs[n_idx:]
    for k in range(TOP_K):
        for blk in range(tm // LANES):
            def start(t, carry, k=k, blk=blk):
                for prio in range(DMA_PRIORITIES):
                    jj = t * DMA_PRIORITIES + prio
                    s = pl.multiple_of(dest_refs[k * (tm // LANES) + blk][0, 0, jj] * ROW_SUB, ROW_SUB)
                    d = pl.multiple_of((k * tm + blk * LANES + jj) * ROW_SUB, ROW_SUB)
                    pltpu.make_async_copy(ys_hbm.at[pl.ds(s, ROW_SUB)], buf.at[pl.ds(d, ROW_SUB)],
                                          sem.at[k]).start(priority=prio)
                return carry
            lax.fori_loop(0, LANES // DMA_PRIORITIES, start, 0, unroll=4)

    x = x_ref[...]
    xb = x.astype(BF16)
    a = jnp.dot(xb, sw1_ref[...], preferred_element_type=F32)
    b = jnp.dot(xb, sw3_ref[...], preferred_element_type=F32)
    shared = jnp.dot((a * jax.nn.sigmoid(a) * b).astype(BF16), sw2_ref[...], preferred_element_type=F32)

    gate = gate_ref[...]
    routed = shared
    for k in range(TOP_K):
        grp = pl.ds(k * tm * ROW_SUB, tm * ROW_SUB)
        pltpu.make_async_copy(ys_hbm.at[pl.ds(0, tm * ROW_SUB)], buf.at[grp], sem.at[k]).wait()
        routed = routed + gate[:, k:k + 1] * _unpack_rows(buf, tm, k * tm)
    o_ref[...] = _layer_norm(alpha * x + routed, g_ref[...], b_ref[...])


def _combine(dests, gate_t, x, ys, sw1, sw3, sw2, g, b, alpha, tm):
    n = x.shape[0]
    full = lambda a: pl.BlockSpec(a.shape, lambda i: (0,) * a.ndim)
    return pl.pallas_call(
        functools.partial(_combine_kernel, tm=tm, alpha=alpha),
        grid=(n // tm,),
        in_specs=[_slot_spec() for _ in dests] + [
                  pl.BlockSpec((tm, 8), lambda i: (i, 0)),
                  pl.BlockSpec((tm, D_MODEL), lambda i: (i, 0)),
                  pl.BlockSpec(memory_space=pl.ANY),
                  full(sw1), full(sw3), full(sw2), full(g), full(b)],
        out_specs=pl.BlockSpec((tm, D_MODEL), lambda i: (i, 0)),
        out_shape=jax.ShapeDtypeStruct((n, D_MODEL), F32),
        scratch_shapes=[pltpu.VMEM((TOP_K * tm * ROW_SUB, LANES), I32), pltpu.SemaphoreType.DMA((TOP_K,))],
        compiler_params=_params("arbitrary"),
        name="moe_combine_ln",
    )(*dests, gate_t, x, ys, sw1, sw3, sw2, g, b)


def _moe(x, rw_t, r_bias, w1, w3, w2, sw1, sw3, sw2, g, b, alpha):
    n = x.shape[0]
    eidx, gate, rank, cnt = _router(x, rw_t, r_bias)
    counts = cnt[:, 0].astype(I32)
    padded = (counts + EXPERT_BLOCK - 1) // EXPERT_BLOCK * EXPERT_BLOCK
    pad_end = jnp.cumsum(padded)
    pad_start = pad_end - padded
    onehot = eidx[:, :, None] == jnp.arange(N_EXPERTS, dtype=I32)[None, None, :]
    dest = jnp.sum(jnp.where(onehot, pad_start[None, None, :], 0), axis=-1) + rank
    n_rows = -(-(n * TOP_K + N_EXPERTS * (EXPERT_BLOCK - 1)) // EXPERT_BLOCK) * EXPERT_BLOCK
    nb = n_rows // EXPERT_BLOCK
    blk_lo = jnp.arange(nb, dtype=I32) * EXPERT_BLOCK
    blk_e = jnp.minimum(jnp.sum((pad_end[None, :] <= blk_lo[:, None]).astype(I32), axis=1), N_EXPERTS - 1)
    blk_valid = jnp.clip(jnp.take(pad_start + counts, blk_e) - blk_lo, 0, EXPERT_BLOCK).astype(I32)
    tm = _tile(n, (256, 128))
    runs = dest.reshape(8, n // tm, tm // LANES, 1, LANES)
    dests = [runs[k, :, r] for k in range(TOP_K) for r in range(tm // LANES)]
    xs = _dispatch(x, dests, n_rows, tm)
    ys = _expert_ffn(xs, blk_e, blk_valid, w1, w3, w2)
    return _combine(dests, gate.T, x, ys, sw1, sw3, sw2, g, b, alpha, tm)


def kernel(x_prompt, x_sample, cache_sb_k, cache_sb_v, cache_diff_k, cache_diff_v, state_s5_re, state_s5_im, state_ret, meta_tokens, w_in_e, s5_lam_re, s5_lam_im, s5_log_dt, s5_b_re, s5_b_im, s5_c_re, s5_c_im, s5_d, s5_w_glu, w_out_e, w_in_o, diff_lq1, diff_lk1, diff_lq2, diff_lk2, diff_norm_g, w_out_o, ln1_g, ln1_b, ln2_g, ln2_b, router_w, router_bias, exp_w1, exp_w3, exp_w2, sh_w1, sh_w3, sh_w2):
    bsz, seq, _ = x_prompt.shape
    bsz_s, t_s, _ = x_sample.shape
    past = cache_sb_k.shape[2]
    depth = ln1_g.shape[0]
    alpha = (2 * depth) ** 0.25
    lp = PAD + N_META + seq
    n_p = bsz * lp
    n_s = bsz_s * t_s
    lt = N_META + seq

    meta = jnp.broadcast_to(meta_tokens[None].astype(F32), (bsz, N_META, D_MODEL))
    hp = jnp.concatenate([jnp.zeros((bsz, PAD, D_MODEL), F32), meta, x_prompt], 1)
    h = jnp.concatenate([hp.reshape(n_p, D_MODEL), x_sample.reshape(n_s, D_MODEL)], 0)

    outs_p = {k: [] for k in ("sbk", "sbv", "dk", "dv", "s5r", "s5i", "ret")}
    outs_s = {k: [] for k in ("sbk", "sbv", "dk", "dv", "s5r", "s5i", "ret")}
    half = lambda c0, dt=F32: (c0 * W_HALF, W_HALF, dt)

    for layer in range(depth):
        i = layer // 2
        if layer % 2 == 0:
            w_in = w_in_e[i]
            splits = [half(c) for c in range(5)]
            u_p, q_p, k_p, v_p, g_p = _project(h, w_in.astype(BF16), splits, 0, n_p)
            u_s, q_s, k_s, v_s, g_s = _project(h, w_in, splits, n_p, n_s, precise=True)
            s5p = _s5_prepare(s5_lam_re[i], s5_lam_im[i], s5_log_dt[i], s5_b_re[i], s5_b_im[i],
                              s5_c_re[i], s5_c_im[i], s5_d[i], s5_w_glu[i])
            seq_p = lambda a: a.reshape(bsz, lp, W_HALF)
            seq_s = lambda a: a.reshape(bsz_s, t_s, W_HALF)
            z5 = jnp.zeros((bsz, S5_WIDTH), F32)
            y5_p, fr_p, fi_p = _s5_mixer(seq_p(u_p), z5, z5, s5p, Q_BLOCK, False)
            y5_s, fr_s, fi_s = _s5_mixer(seq_s(u_s), state_s5_re[i].reshape(bsz_s, S5_WIDTH).astype(F32),
                                         state_s5_im[i].reshape(bsz_s, S5_WIDTH).astype(F32), s5p, t_s, True)
            yr_p, rt_p = _ret_mixer(seq_p(q_p), seq_p(k_p), seq_p(v_p), seq_p(g_p),
                                    jnp.zeros((bsz, RET_HEADS, RET_DK, RET_DK), F32),
                                    jnp.arange(lp) - PAD, Q_BLOCK)
            yr_s, rt_s = _ret_mixer(seq_s(q_s), seq_s(k_s), seq_s(v_s), seq_s(g_s),
                                    state_ret[i].astype(F32), past + jnp.arange(t_s), t_s)
            mixed = (y5_p.reshape(n_p, W_HALF), yr_p.reshape(n_p, W_HALF),
                     y5_s.reshape(n_s, W_HALF), yr_s.reshape(n_s, W_HALF))
            w_out = w_out_e[i]
            st = lambda a, nb_: a.reshape(nb_, S5_GROUPS, S5_STATE)
            outs_p["s5r"].append(st(fr_p, bsz)); outs_p["s5i"].append(st(fi_p, bsz)); outs_p["ret"].append(rt_p)
            outs_s["s5r"].append(st(fr_s, bsz_s)); outs_s["s5i"].append(st(fi_s, bsz_s)); outs_s["ret"].append(rt_s)
        else:
            splits = [half(0, BF16), half(1), half(2), half(3, BF16), half(4), half(5),
                      half(1, BF16), half(2, BF16), half(4, BF16), half(5, BF16)]
            w_in = w_in_o[i].astype(BF16)
            (qs, qd, ksb, vsb, kdb, vdb, ksf, vsf, kdf, vdf) = _project_prompt_kv(h, w_in, bsz, lp)
            (qs_s, ksf_s, vsf_s, qd_s, kdf_s, vdf_s, ksb_s, vsb_s, kdb_s, vdb_s) = _project(h, w_in, splits, n_p, n_s)
            lam_init = 0.8 - 0.6 * math.exp(-0.3 * layer)
            lam_vecs = [v[i].reshape(1, DIFF_DH).astype(F32) for v in (diff_lq1, diff_lk1, diff_lq2, diff_lk2)]
            norm_g = diff_norm_g[i].reshape(2 * DIFF_DH, 1).astype(F32)
            seq_p = lambda a: a.reshape(bsz, lp, W_HALF)
            seq_s = lambda a: a.reshape(bsz_s, t_s, W_HALF)
            osb_p, od_p = _attention(seq_p(qs), seq_p(qd), seq_p(ksb), seq_p(vsb), seq_p(kdb), seq_p(vdb),
                                     lam_vecs, norm_g, bq=Q_BLOCK, q_base=0, kv_lo=PAD, lam_init=lam_init)
            bq_s = -(-t_s // Q_BLOCK) * Q_BLOCK
            lk_s = -(-(past + bq_s) // KEY_BLOCK) * KEY_BLOCK
            qpad = lambda a: jnp.pad(seq_s(a), ((0, 0), (0, bq_s - t_s), (0, 0)))

            def keys(cache, new):
                c = cache.reshape(bsz_s, past, W_HALF).astype(BF16)
                z = jnp.zeros((bsz_s, lk_s - past - t_s, W_HALF), BF16)
                return jnp.concatenate([c, seq_s(new), z], 1)
            osb_s, od_s = _attention(qpad(qs_s), qpad(qd_s), keys(cache_sb_k[i], ksb_s), keys(cache_sb_v[i], vsb_s),
                                     keys(cache_diff_k[i], kdb_s), keys(cache_diff_v[i], vdb_s),
                                     lam_vecs, norm_g, bq=bq_s, q_base=past, kv_lo=0, lam_init=lam_init)
            mixed = (osb_p.reshape(n_p, W_HALF), od_p.reshape(n_p, W_HALF),
                     osb_s[:, :t_s].reshape(n_s, W_HALF), od_s[:, :t_s].reshape(n_s, W_HALF))
            w_out = w_out_o[i]
            kv_p = lambda a, hh, dd: a.reshape(bsz, lt, hh, dd)
            kv_s = lambda a, hh, dd: a.reshape(bsz_s, t_s, hh, dd)
            outs_p["sbk"].append(kv_p(ksf, SB_HEADS, SB_DH)); outs_p["sbv"].append(kv_p(vsf, SB_HEADS, SB_DH))
            outs_p["dk"].append(kv_p(kdf, DIFF_HEADS, 2 * DIFF_DH)); outs_p["dv"].append(kv_p(vdf, DIFF_HEADS, 2 * DIFF_DH))
            outs_s["sbk"].append(kv_s(ksf_s, SB_HEADS, SB_DH)); outs_s["sbv"].append(kv_s(vsf_s, SB_HEADS, SB_DH))
            outs_s["dk"].append(kv_s(kdf_s, DIFF_HEADS, 2 * DIFF_DH)); outs_s["dv"].append(kv_s(vdf_s, DIFF_HEADS, 2 * DIFF_DH))

        row = lambda a: a[layer].reshape(1, D_MODEL).astype(F32)
        h = _outproj_ln(*mixed, h, w_out.astype(BF16), row(ln1_g), row(ln1_b), alpha)
        h = _moe(h, router_w[layer].T.astype(F32), router_bias[layer].reshape(N_EXPERTS, 1).astype(F32),
                 exp_w1[layer].astype(F32), exp_w3[layer].astype(F32), exp_w2[layer].astype(F32),
                 sh_w1[layer].astype(BF16), sh_w3[layer].astype(BF16), sh_w2[layer].astype(BF16),
                 row(ln2_g), row(ln2_b), alpha)

    y_prompt = h[:n_p].reshape(bsz, lp, D_MODEL)[:, PAD + N_META:]
    y_sample = h[n_p:].reshape(bsz_s, t_s, D_MODEL)
    order = ("sbk", "sbv", "dk", "dv", "s5r", "s5i", "ret")
    return ((y_prompt, y_sample) + tuple(jnp.stack(outs_p[k]) for k in order)
            + tuple(jnp.stack(outs_s[k]) for k in order))
```
